```python
import jax
import jax.numpy as jnp
from jax import lax
import numpy as np

D_MODEL = 2048
BATCH = 1
SEQ = 16384
DEPTH = 2

GRID_W = 64
CTX_LEN = 256

CONV_W = D_MODEL // 4
ATT_HEAD_DIM = 64
ATT_HEADS = (3 * D_MODEL // 8) // ATT_HEAD_DIM
ATT_KV_HEADS = 4
ATT_W = ATT_HEADS * ATT_HEAD_DIM
RWKV_HEAD_DIM = 64
RWKV_HEADS = (3 * D_MODEL // 8) // RWKV_HEAD_DIM
RWKV_W = RWKV_HEADS * RWKV_HEAD_DIM
MIX_W = CONV_W + ATT_W + RWKV_W

CONV_K = 3
WINDOW = 128
ATT_BLOCK = 128
ROPE_THETA = 10000.0
ROPE_FREQS = ATT_HEAD_DIM // 4
DECAY_RANK = 64
ICLR_RANK = 64
GATE_RANK = 128
RWKV_GN_EPS = 64e-5

N_EXPERTS = 32
N_EXPERT_GROUPS = 4
EXPERTS_PER_GROUP = N_EXPERTS // N_EXPERT_GROUPS
TOP_K = 2
D_EXPERT = 768
MOE_BLOCK = 128

RMS_EPS = 1e-6
NEG_INF = -1e30

IN_SPLITS = (CONV_W, CONV_W, CONV_W,
             ATT_HEADS * ATT_HEAD_DIM, ATT_KV_HEADS * ATT_HEAD_DIM, ATT_KV_HEADS * ATT_HEAD_DIM,
             RWKV_W, RWKV_W, RWKV_W,
             DECAY_RANK, DECAY_RANK, ICLR_RANK, ICLR_RANK, GATE_RANK)
IN_W = sum(IN_SPLITS)

kernel_name = "hybrid_conv_swa_rwkv7_moe_dit"


def _rms_norm(x, g):
    x32 = x.astype(jnp.float32)
    y = x32 * lax.rsqrt(jnp.mean(x32 * x32, axis=-1, keepdims=True) + RMS_EPS)
    return (y * g.astype(jnp.float32)).astype(x.dtype)


def _modulate(h, shift, scale):
    return h * (1.0 + scale) + shift


def _split_cols(p):
    idx = np.cumsum(IN_SPLITS)[:-1].tolist()
    return jnp.split(p, idx, axis=-1)


def _short_conv(b_gate, c_gate, h, conv_w):
    u = c_gate * h
    y = lax.conv_general_dilated(u, conv_w[:, None, :].astype(u.dtype), window_strides=(1,),
                                 padding=[(CONV_K // 2, CONV_K // 2)],
                                 dimension_numbers=('NWC', 'WIO', 'NWC'),
                                 feature_group_count=CONV_W)
    return b_gate * y


def _axial_rope_tables(rows):
    n = rows * GRID_W
    row = jnp.repeat(jnp.arange(rows, dtype=jnp.int32), GRID_W)
    col = jnp.arange(n, dtype=jnp.int32) % GRID_W
    inv = ROPE_THETA ** (-jnp.arange(ROPE_FREQS, dtype=jnp.float32) / ROPE_FREQS)
    ang_r = row[:, None].astype(jnp.float32) * inv[None, :]
    ang_c = col[:, None].astype(jnp.float32) * inv[None, :]
    return (jnp.cos(ang_r), jnp.sin(ang_r), jnp.cos(ang_c), jnp.sin(ang_c))


def _rotate(x, cos, sin):
    f = x.shape[-1] // 2
    x1, x2 = x[..., :f], x[..., f:]
    cs = cos[None, :, None, :].astype(x.dtype)
    sn = sin[None, :, None, :].astype(x.dtype)
    return jnp.concatenate([x1 * cs - x2 * sn, x1 * sn + x2 * cs], axis=-1)


def _axial_rope(x, tabs):
    cr, sr, cc, sc = tabs
    half = ATT_HEAD_DIM // 2
    return jnp.concatenate([_rotate(x[..., :half], cr, sr), _rotate(x[..., half:], cc, sc)], axis=-1)


def _window_attention(q, k, v, k_ctx, v_ctx, sinks):
    bsz, n = q.shape[0], q.shape[1]
    nb = n // ATT_BLOCK
    grp = ATT_HEADS // ATT_KV_HEADS
    qb = q.reshape(bsz, nb, ATT_BLOCK, ATT_KV_HEADS, grp, ATT_HEAD_DIM)

    def band(t):
        tp = jnp.pad(t, ((0, 0), (ATT_BLOCK, ATT_BLOCK), (0, 0), (0, 0)))
        tp = tp.reshape(bsz, nb + 2, ATT_BLOCK, ATT_KV_HEADS, ATT_HEAD_DIM)
        return jnp.concatenate([tp[:, :-2], tp[:, 1:-1], tp[:, 2:]], axis=2)

    kb, vb = band(k), band(v)
    scale = ATT_HEAD_DIM ** -0.5
    s_loc = jnp.einsum('bnqkgd,bnskd->bnkgqs', qb, kb).astype(jnp.float32) * scale
    blk = jnp.arange(nb)[:, None, None] * ATT_BLOCK
    qpos = blk + jnp.arange(ATT_BLOCK)[None, :, None]
    kpos = blk - ATT_BLOCK + jnp.arange(3 * ATT_BLOCK)[None, None, :]
    valid = (jnp.abs(kpos - qpos) <= WINDOW) & (kpos >= 0) & (kpos < n)
    s_loc = jnp.where(valid[None, :, None, None], s_loc, NEG_INF)
    s_ctx = jnp.einsum('bnqkgd,bckd->bnkgqc', qb, k_ctx).astype(jnp.float32) * scale
    sink = jnp.broadcast_to(
        sinks.astype(jnp.float32).reshape(ATT_KV_HEADS, grp)[None, None, :, :, None, None],
        s_loc.shape[:-1] + (1,))
    probs = jax.nn.softmax(jnp.concatenate([sink, s_ctx, s_loc], axis=-1), axis=-1)
    n_ctx = k_ctx.shape[1]
    p_ctx = probs[..., 1:1 + n_ctx].astype(v.dtype)
    p_loc = probs[..., 1 + n_ctx:].astype(v.dtype)
    o = (jnp.einsum('bnkgqc,bckd->bnqkgd', p_ctx, v_ctx)
         + jnp.einsum('bnkgqs,bnskd->bnqkgd', p_loc, vb))
    return o.reshape(bsz, n, ATT_W)


def _ctx_attention(q, k, v, sinks):
    bsz, length = q.shape[0], q.shape[1]
    grp = ATT_HEADS // ATT_KV_HEADS
    qg = q.reshape(bsz, length, ATT_KV_HEADS, grp, ATT_HEAD_DIM)
    s = jnp.einsum('bqkgd,bskd->bkgqs', qg, k).astype(jnp.float32) * (ATT_HEAD_DIM ** -0.5)
    sink = jnp.broadcast_to(
        sinks.astype(jnp.float32).reshape(ATT_KV_HEADS, grp)[None, :, :, None, None],
        s.shape[:-1] + (1,))
    p = jax.nn.softmax(jnp.concatenate([sink, s], axis=-1), axis=-1)[..., 1:].astype(v.dtype)
    o = jnp.einsum('bkgqs,bskd->bqkgd', p, v)
    return o.reshape(bsz, length, ATT_W)


def _heads(t):
    return t.reshape(t.shape[0], t.shape[1], RWKV_HEADS, RWKV_HEAD_DIM)


def _rwkv_prep(r, k, v, wd, ad, k_k, k_a, decay_base, decay_up, iclr_base, iclr_up):
    f32 = jnp.float32
    r, k, v = r.astype(f32), k.astype(f32), v.astype(f32)
    kk = _heads(k * k_k.astype(f32))
    kk = kk / jnp.maximum(jnp.sqrt(jnp.sum(kk * kk, axis=-1, keepdims=True)), 1e-12)
    dirs = []
    for d in range(2):
        z = (decay_base[d] + jnp.tanh(wd[d]) @ decay_up[d]).astype(f32)
        w_log = -jax.nn.softplus(-z) - 0.5
        decay = jnp.exp(-jnp.exp(w_log))
        a = jax.nn.sigmoid((iclr_base[d] + ad[d] @ iclr_up[d]).astype(f32))
        k_d = k * (1.0 + (a - 1.0) * k_a.astype(f32))
        dirs.append((_heads(decay), _heads(k_d), kk * _heads(a)))
    return _heads(r), _heads(v), kk, dirs


def _rwkv_scan(s0, r, decay, k, v, kk, b, reverse, emit):
    def step(S, inp):
        r_t, w_t, k_t, v_t, kk_t, b_t = inp
        sk = jnp.einsum('bhij,bhj->bhi', S, kk_t)
        S = S * w_t[:, :, None, :] - sk[..., None] * b_t[:, :, None, :] + v_t[..., None] * k_t[:, :, None, :]
        y = jnp.einsum('bhij,bhj->bhi', S, r_t) if emit else None
        return S, y

    xs = tuple(jnp.moveaxis(t, 1, 0) for t in (r, decay, k, v, kk, b))
    S, ys = lax.scan(step, s0, xs, reverse=reverse)
    return S, (jnp.moveaxis(ys, 0, 1) if emit else None)


def _rwkv_out(y, r, v, k_dirs, gd, r_k, lnx_w, lnx_b, gate_up, dtype):
    f32 = jnp.float32
    bsz, length = y.shape[0], y.shape[1]
    mu = jnp.mean(y, axis=-1, keepdims=True)
    var = jnp.mean(jnp.square(y - mu), axis=-1, keepdims=True)
    yn = ((y - mu) * lax.rsqrt(var + RWKV_GN_EPS)).reshape(bsz, length, RWKV_W)
    yn = yn * lnx_w.astype(f32) + lnx_b.astype(f32)
    rk = r_k.astype(f32)
    bonus = sum(jnp.sum(r * kd * rk, axis=-1, keepdims=True) * v for kd in k_dirs)
    o = yn + bonus.reshape(bsz, length, RWKV_W)
    g = jax.nn.sigmoid(gd) @ gate_up
    return (o * g.astype(f32)).astype(dtype)


def _token_mixers(p_ctx, p_lat, rope_tabs, conv_w, sinks, decay_base, decay_up, iclr_base, iclr_up,
                  gate_up, k_k, k_a, r_k, lnx_w, lnx_b, with_ctx_out):
    (cb_c, cc_c, ch_c, q_c, k_c, v_c, rr_c, rk_c, rv_c,
     wdf_c, wdb_c, adf_c, adb_c, gd_c) = _split_cols(p_ctx)
    (cb_l, cc_l, ch_l, q_l, k_l, v_l, rr_l, rk_l, rv_l,
     wdf_l, wdb_l, adf_l, adb_l, gd_l) = _split_cols(p_lat)
    bsz = p_lat.shape[0]
    dtype = p_lat.dtype

    def ahd(t, nh):
        return t.reshape(t.shape[0], t.shape[1], nh, ATT_HEAD_DIM)

    a_lat = _short_conv(cb_l, cc_l, ch_l, conv_w)

    kc, vc = ahd(k_c, ATT_KV_HEADS), ahd(v_c, ATT_KV_HEADS)
    ql = _axial_rope(ahd(q_l, ATT_HEADS), rope_tabs)
    kl = _axial_rope(ahd(k_l, ATT_KV_HEADS), rope_tabs)
    b_lat = _window_attention(ql, kl, ahd(v_l, ATT_KV_HEADS), kc, vc, sinks)

    r_c, vv_c, kk_c, dirs_c = _rwkv_prep(rr_c, rk_c, rv_c, (wdf_c, wdb_c), (adf_c, adb_c), k_k, k_a,
                                         decay_base, decay_up, iclr_base, iclr_up)
    r_l, vv_l, kk_l, dirs_l = _rwkv_prep(rr_l, rk_l, rv_l, (wdf_l, wdb_l), (adf_l, adb_l), k_k, k_a,
                                         decay_base, decay_up, iclr_base, iclr_up)
    s0 = jnp.zeros((bsz, RWKV_HEADS, RWKV_HEAD_DIM, RWKV_HEAD_DIM), jnp.float32)
    dec_cf, k_cf, b_cf = dirs_c[0]
    dec_cb, k_cb, b_cb = dirs_c[1]
    s_f, yc_f = _rwkv_scan(s0, r_c, dec_cf, k_cf, vv_c, kk_c, b_cf, reverse=False, emit=with_ctx_out)
    s_b, yc_b = _rwkv_scan(s0, r_c, dec_cb, k_cb, vv_c, kk_c, b_cb, reverse=True, emit=with_ctx_out)
    dec_lf, k_lf, b_lf = dirs_l[0]
    dec_lb, k_lb, b_lb = dirs_l[1]
    _, yl_f = _rwkv_scan(s_f, r_l, dec_lf, k_lf, vv_l, kk_l, b_lf, reverse=False, emit=True)
    _, yl_b = _rwkv_scan(s_b, r_l, dec_lb, k_lb, vv_l, kk_l, b_lb, reverse=True, emit=True)
    c_lat = _rwkv_out(yl_f + yl_b, r_l, vv_l, (k_lf, k_lb), gd_l, r_k, lnx_w, lnx_b, gate_up, dtype)

    mix_lat = jnp.concatenate([a_lat, b_lat, c_lat], axis=-1)
    if not with_ctx_out:
        return mix_lat, None
    a_ctx = _short_conv(cb_c, cc_c, ch_c, conv_w)
    b_ctx = _ctx_attention(ahd(q_c, ATT_HEADS), kc, vc, sinks)
    c_ctx_out = _rwkv_out(yc_f + yc_b, r_c, vv_c, (k_cf, k_cb), gd_c, r_k, lnx_w, lnx_b, gate_up, dtype)
    mix_ctx = jnp.concatenate([a_ctx, b_ctx, c_ctx_out], axis=-1)
    return mix_lat, mix_ctx


def _moe(h, router_w, router_b, w_gate, w_up, w_down):
    n_tok, d = h.shape
    scores = jax.nn.sigmoid((h @ router_w).astype(jnp.float32))
    sel = scores + router_b.astype(jnp.float32)
    grp = sel.reshape(n_tok, N_EXPERT_GROUPS, EXPERTS_PER_GROUP)
    grp_score = jnp.sum(lax.top_k(grp, 2)[0], axis=-1)
    best = jnp.argmax(grp_score, axis=-1)
    in_grp = (jnp.arange(N_EXPERT_GROUPS)[None, :] == best[:, None])[:, :, None]
    masked = jnp.where(in_grp, grp, NEG_INF).reshape(n_tok, N_EXPERTS)
    idx = lax.top_k(masked, TOP_K)[1]
    gate = jnp.take_along_axis(scores, idx, axis=1)
    gate = gate / jnp.sum(gate, axis=-1, keepdims=True)

    n_asg = n_tok * TOP_K
    e_flat = idx.reshape(n_asg)
    tok_flat = jnp.arange(n_asg, dtype=jnp.int32) // TOP_K
    w_flat = gate.reshape(n_asg)
    order = jnp.argsort(e_flat)
    e_sorted = e_flat[order]
    counts = jnp.bincount(e_flat, length=N_EXPERTS)
    starts = jnp.cumsum(counts) - counts
    padded = (counts + MOE_BLOCK - 1) // MOE_BLOCK * MOE_BLOCK
    pends = jnp.cumsum(padded)
    pstarts = pends - padded
    dest = pstarts[e_sorted] + (jnp.arange(n_asg) - starts[e_sorted])
    n_blk = (n_asg + MOE_BLOCK - 1) // MOE_BLOCK + N_EXPERTS
    cap = n_blk * MOE_BLOCK
    buf_tok = jnp.full((cap,), n_tok, jnp.int32).at[dest].set(tok_flat[order])
    buf_w = jnp.zeros((cap,), h.dtype).at[dest].set(w_flat[order].astype(h.dtype))
    block_e = jnp.minimum(jnp.searchsorted(pends, jnp.arange(n_blk) * MOE_BLOCK, side='right'),
                          N_EXPERTS - 1)
    h_pad = jnp.concatenate([h, jnp.zeros((1, d), h.dtype)], axis=0)
    xb = h_pad[buf_tok].reshape(n_blk, MOE_BLOCK, d)

    def expert_block(args):
        xblk, e = args
        g = xblk @ w_gate[e]
        u = xblk @ w_up[e]
        return (jax.nn.silu(g) * u) @ w_down[e]

    yb = lax.map(expert_block, (xb, block_e)).reshape(cap, d)
    out = jnp.zeros((n_tok + 1, d), h.dtype).at[buf_tok].add(yb * buf_w[:, None])
    return out[:n_tok]


def setup_inputs(seed: int = 0) -> dict:
    key = jax.random.key(seed)
    ks = iter(jax.random.split(key, 40))
    D = D_MODEL

    def nrm(shape, scale):
        return scale * jax.random.normal(next(ks), shape, jnp.float32)

    return {
        "x": nrm((BATCH, SEQ, D), 1.0),
        "c": nrm((BATCH, D), 1.0),
        "ctx": nrm((BATCH, CTX_LEN, D), 1.0),
        "c_ctx": nrm((D,), 1.0),
        "ada_w": nrm((DEPTH, D, 6 * D), 0.5 * D ** -0.5),
        "ada_b": nrm((DEPTH, 6 * D), 0.02),
        "norm1_g": 1.0 + nrm((DEPTH, D), 0.02),
        "norm2_g": 1.0 + nrm((DEPTH, D), 0.02),
        "w_in": nrm((DEPTH, D, IN_W), D ** -0.5),
        "w_out": nrm((DEPTH, MIX_W, D), MIX_W ** -0.5),
        "conv_w": nrm((DEPTH, CONV_K, CONV_W), CONV_K ** -0.5),
        "attn_sinks": nrm((DEPTH, ATT_HEADS), 0.5),
        "decay_base": nrm((DEPTH, 2, RWKV_W), 0.5) - 0.5,
        "decay_up": nrm((DEPTH, 2, DECAY_RANK, RWKV_W), 0.5 * DECAY_RANK ** -0.5),
        "iclr_base": nrm((DEPTH, 2, RWKV_W), 0.5),
        "iclr_up": nrm((DEPTH, 2, ICLR_RANK, RWKV_W), 0.5 * ICLR_RANK ** -0.5),
        "gate_up": nrm((DEPTH, GATE_RANK, RWKV_W), GATE_RANK ** -0.5),
        "k_k": 0.85 + nrm((DEPTH, RWKV_W), 0.02),
        "k_a": 1.0 + nrm((DEPTH, RWKV_W), 0.02),
        "r_k": nrm((DEPTH, RWKV_HEADS, RWKV_HEAD_DIM), 0.1),
        "lnx_w": 1.0 + nrm((DEPTH, RWKV_W), 0.02),
        "lnx_b": nrm((DEPTH, RWKV_W), 0.02),
        "router_w": nrm((D, N_EXPERTS), D ** -0.5),
        "router_b": nrm((N_EXPERTS,), 0.01),
        "expert_gate": nrm((DEPTH, N_EXPERTS, D, D_EXPERT), D ** -0.5),
        "expert_up": nrm((DEPTH, N_EXPERTS, D, D_EXPERT), D ** -0.5),
        "expert_down": nrm((DEPTH, N_EXPERTS, D_EXPERT, D), D_EXPERT ** -0.5),
        "final_norm_g": 1.0 + nrm((D,), 0.02),
    }


def reference(x, c, ctx, c_ctx, ada_w, ada_b, norm1_g, norm2_g, w_in, w_out, conv_w, attn_sinks,
              decay_base, decay_up, iclr_base, iclr_up, gate_up, k_k, k_a, r_k, lnx_w, lnx_b,
              router_w, router_b, expert_gate, expert_up, expert_down, final_norm_g):
    bsz, n_lat, d = x.shape
    rows = n_lat // GRID_W
    rope_tabs = _axial_rope_tables(rows)
    silu_c = jax.nn.silu(c)
    silu_cc = jax.nn.silu(c_ctx)
    for l in range(DEPTH):
        last = l == DEPTH - 1
        sh1, sc1, g1, sh2, sc2, g2 = [m[:, None, :] for m in
                                      jnp.split(silu_c @ ada_w[l] + ada_b[l], 6, axis=-1)]
        csh1, csc1, cg1, csh2, csc2, cg2 = jnp.split(silu_cc @ ada_w[l] + ada_b[l], 6, axis=-1)

        h_lat = _modulate(_rms_norm(x, norm1_g[l]), sh1, sc1)
        h_ctx = _modulate(_rms_norm(ctx, norm1_g[l]), csh1, csc1)
        mix_lat, mix_ctx = _token_mixers(h_ctx @ w_in[l], h_lat @ w_in[l], rope_tabs, conv_w[l],
                                         attn_sinks[l], decay_base[l], decay_up[l], iclr_base[l],
                                         iclr_up[l], gate_up[l], k_k[l], k_a[l], r_k[l], lnx_w[l],
                                         lnx_b[l], with_ctx_out=not last)
        x = x + g1 * (mix_lat @ w_out[l])

        h2_lat = _modulate(_rms_norm(x, norm2_g[l]), sh2, sc2).reshape(bsz * n_lat, d)
        if last:
            y_lat = _moe(h2_lat, router_w, router_b, expert_gate[l], expert_up[l], expert_down[l])
        else:
            ctx = ctx + cg1 * (mix_ctx @ w_out[l])
            h2_ctx = _modulate(_rms_norm(ctx, norm2_g[l]), csh2, csc2).reshape(-1, d)
            n_ctx_tok = h2_ctx.shape[0]
            y_all = _moe(jnp.concatenate([h2_ctx, h2_lat], axis=0), router_w, router_b,
                         expert_gate[l], expert_up[l], expert_down[l])
            ctx = ctx + cg2 * y_all[:n_ctx_tok].reshape(ctx.shape)
            y_lat = y_all[n_ctx_tok:]
        x = x + g2 * y_lat.reshape(x.shape)
    return _rms_norm(x, final_norm_g)
```

```python
import functools

import numpy as np
import jax
import jax.numpy as jnp
from jax import lax
from jax.experimental import pallas as pl
from jax.experimental.pallas import tpu as pltpu

F32 = jnp.float32
BF16 = jnp.bfloat16
HIGHEST = lax.Precision.HIGHEST

D_MODEL = 2048
GRID_W = 64
CONV_W = D_MODEL // 4
CONV_K = 3
HEAD_DIM = 64
ATT_HEADS = 12
ATT_KV_HEADS = 4
ATT_GROUP = ATT_HEADS // ATT_KV_HEADS
ATT_W = ATT_HEADS * HEAD_DIM
ATT_KV_W = ATT_KV_HEADS * HEAD_DIM
RWKV_HEADS = 12
RWKV_W = RWKV_HEADS * HEAD_DIM
WINDOW = 128
ATT_BLOCK = 128
ROPE_THETA = 10000.0
ROPE_FREQS = HEAD_DIM // 4
DECAY_RANK = 64
ICLR_RANK = 64
GATE_RANK = 128
RWKV_GN_EPS = 64e-5
N_EXPERTS = 32
N_EXPERT_GROUPS = 4
EXPERTS_PER_GROUP = N_EXPERTS // N_EXPERT_GROUPS
TOP_K = 2
D_EXPERT = 768
RMS_EPS = 1e-6
NEG_INF = -1e30

COL_Q = 0
COL_R = 768
COL_RK = 1536
COL_RV = 2304
COL_CB = 3072
COL_CC = 3584
COL_CH = 4096
COL_K = 4608
COL_V = 4864
COL_LR = 5120
LR_W = 512
IN_W_PAD = 5632

RWKV_CHUNK = 64
NORM_ROWS = 256
VMEM_LIMIT = 56 * 1024 * 1024


def _params(n_axes):
    return pltpu.CompilerParams(dimension_semantics=("arbitrary",) * n_axes,
                                vmem_limit_bytes=VMEM_LIMIT)


def _row_tile(n_rows, candidates):
    for t in candidates:
        if n_rows % t == 0:
            return t
    raise ValueError(f"no row tile for {n_rows}")


def _in_proj_perm():
    o_cb, o_cc, o_ch = 0, 512, 1024
    o_q, o_k, o_v = 1536, 2304, 2560
    o_r, o_rk, o_rv = 2816, 3584, 4352
    o_lr = 5120
    segs = [(o_q, 768), (o_r, 768), (o_rk, 768), (o_rv, 768), (o_cb, 512), (o_cc, 512), (o_ch, 512),
            (o_k, 256), (o_v, 256), (o_lr, 384)]
    return np.concatenate([np.arange(o, o + w) for o, w in segs])


def _ada_kernel(s_ref, w_ref, b_ref, o_ref):
    o_ref[0] = jnp.dot(s_ref[...], w_ref[0], precision=HIGHEST, preferred_element_type=F32) + b_ref[0]


def _ada_mod(c, c_ctx, ada_w, ada_b):
    depth, d, n = ada_w.shape
    s = jnp.zeros((8, d), F32).at[0].set(c[0]).at[1].set(c_ctx)
    s = s * jax.nn.sigmoid(s)
    tn = 1024
    return pl.pallas_call(
        _ada_kernel,
        grid=(depth, n // tn),
        in_specs=[pl.BlockSpec((8, d), lambda l, j: (0, 0)),
                  pl.BlockSpec((1, d, tn), lambda l, j: (l, 0, j)),
                  pl.BlockSpec((1, 1, tn), lambda l, j: (l, 0, j))],
        out_specs=pl.BlockSpec((1, 8, tn), lambda l, j: (l, 0, j)),
        out_shape=jax.ShapeDtypeStruct((depth, 8, n), F32),
        compiler_params=_params(2),
        name="ada_mod",
    )(s, ada_w, ada_b.reshape(depth, 1, n))


def _norm_mod(x, g, mod, row0, n_ctx):
    ms = jnp.mean(x * x, axis=-1, keepdims=True)
    y = x * lax.rsqrt(ms + RMS_EPS) * g
    row = row0 + lax.broadcasted_iota(jnp.int32, (x.shape[0], 1), 0)
    is_ctx = row < n_ctx
    shift = jnp.where(is_ctx, mod[2:3, :], mod[0:1, :])
    scale = jnp.where(is_ctx, mod[3:4, :], mod[1:2, :])
    return y * (1.0 + scale) + shift


def _nmm_kernel(x_ref, g_ref, mod_ref, w_ref, o_ref, h_ref, *, n_ctx, tm):
    i = pl.program_id(0)

    @pl.when(pl.program_id(1) == 0)
    def _():
        for r0 in range(0, tm, NORM_ROWS):
            rows = slice(r0, r0 + NORM_ROWS)
            h_ref[rows, :] = _norm_mod(x_ref[rows, :], g_ref[...], mod_ref[...], i * tm + r0, n_ctx).astype(BF16)

    o_ref[...] = jnp.dot(h_ref[...], w_ref[...], preferred_element_type=F32)


def _norm_mod_matmul(xs, g, mod, w_bf16, n_ctx):
    n_rows, d = xs.shape
    n_out = w_bf16.shape[1]
    tm = _row_tile(n_rows, (1280, 1024, 512, 256))
    tn = 512
    return pl.pallas_call(
        functools.partial(_nmm_kernel, n_ctx=n_ctx, tm=tm),
        grid=(n_rows // tm, n_out // tn),
        in_specs=[pl.BlockSpec((tm, d), lambda i, j: (i, 0)),
                  pl.BlockSpec((1, d), lambda i, j: (0, 0)),
                  pl.BlockSpec((8, d), lambda i, j: (0, 0)),
                  pl.BlockSpec((d, tn), lambda i, j: (0, j))],
        out_specs=pl.BlockSpec((tm, tn), lambda i, j: (i, j)),
        out_shape=jax.ShapeDtypeStruct((n_rows, n_out), F32),
        scratch_shapes=[pltpu.VMEM((tm, d), BF16)],
        compiler_params=_params(2),
        name="norm_mod_in_proj",
    )(xs, g.reshape(1, d), mod, w_bf16)


def _conv_kernel(cb_ref, cc_ref, ch_ref, ccp_ref, chp_ref, ccn_ref, chn_ref, w_ref, o_ref, *, n_ctx, n_rows, tm):
    i = pl.program_id(0)
    u = cc_ref[...] * ch_ref[...]
    u_prev_row = ccp_ref[7:8, :] * chp_ref[7:8, :]
    u_next_row = ccn_ref[0:1, :] * chn_ref[0:1, :]
    loc = lax.broadcasted_iota(jnp.int32, (tm, 1), 0)
    row = i * tm + loc
    up = jnp.where(loc == 0, u_prev_row, pltpu.roll(u, 1, axis=0))
    dn = jnp.where(loc == tm - 1, u_next_row, pltpu.roll(u, tm - 1, axis=0))
    up = jnp.where((row == 0) | (row == n_ctx), 0.0, up)
    dn = jnp.where((row == n_ctx - 1) | (row == n_rows - 1), 0.0, dn)
    w = w_ref[...]
    y = w[0:1, :] * up + w[1:2, :] * u + w[2:3, :] * dn
    o_ref[...] = (cb_ref[...] * y).astype(o_ref.dtype)


def _short_conv(p, conv_w, n_ctx):
    n_rows = p.shape[0]
    tm = _row_tile(n_rows, (1280, 1024, 512, 256))
    r8 = tm // 8
    last8 = n_rows // 8 - 1
    wpad = jnp.zeros((8, CONV_W), F32).at[:CONV_K].set(conv_w)
    blk = lambda c: pl.BlockSpec((tm, CONV_W), lambda i, c=c: (i, c))
    prev = lambda c: pl.BlockSpec((8, CONV_W), lambda i, c=c: (jnp.maximum(i * r8 - 1, 0), c))
    nxt = lambda c: pl.BlockSpec((8, CONV_W), lambda i, c=c: (jnp.minimum((i + 1) * r8, last8), c))
    cb, cc, ch = COL_CB // CONV_W, COL_CC // CONV_W, COL_CH // CONV_W
    return pl.pallas_call(
        functools.partial(_conv_kernel, n_ctx=n_ctx, n_rows=n_rows, tm=tm),
        grid=(n_rows // tm,),
        in_specs=[blk(cb), blk(cc), blk(ch), prev(cc), prev(ch), nxt(cc), nxt(ch),
                  pl.BlockSpec((8, CONV_W), lambda i: (0, 0))],
        out_specs=pl.BlockSpec((tm, CONV_W), lambda i: (i, 0)),
        out_shape=jax.ShapeDtypeStruct((n_rows, CONV_W), BF16),
        compiler_params=_params(1),
        name="short_conv",
    )(p, p, p, p, p, p, p, wpad)


def _swap_halves(x):
    n = x.shape[1]
    lane = lax.broadcasted_iota(jnp.int32, x.shape, 1)
    fwd = pltpu.roll(x, n - ROPE_FREQS, axis=1)
    bwd = pltpu.roll(x, ROPE_FREQS, axis=1)
    return jnp.where((lane % (2 * ROPE_FREQS)) < ROPE_FREQS, fwd, bwd)


def _rope_kernel(q_ref, k_ref, v_ref, cos_ref, sin_ref, qo_ref, ko_ref, vo_ref):
    cos = cos_ref[...]
    sin = sin_ref[...]
    q = q_ref[...]
    k = k_ref[...]
    cos_q = jnp.concatenate([cos] * (ATT_W // 128), axis=1)
    sin_q = jnp.concatenate([sin] * (ATT_W // 128), axis=1)
    cos_k = jnp.concatenate([cos] * (ATT_KV_W // 128), axis=1)
    sin_k = jnp.concatenate([sin] * (ATT_KV_W // 128), axis=1)
    qo_ref[...] = ((q * cos_q + _swap_halves(q) * sin_q) * (HEAD_DIM ** -0.5)).astype(BF16)
    ko_ref[...] = (k * cos_k + _swap_halves(k) * sin_k).astype(BF16)
    vo_ref[...] = v_ref[...].astype(BF16)


def _rope_tables(n_ctx, n_lat):
    row = jnp.repeat(jnp.arange(n_lat // GRID_W, dtype=jnp.int32), GRID_W)
    col = jnp.arange(n_lat, dtype=jnp.int32) % GRID_W
    inv = ROPE_THETA ** (-jnp.arange(ROPE_FREQS, dtype=F32) / ROPE_FREQS)
    ang_r = row[:, None].astype(F32) * inv[None, :]
    ang_c = col[:, None].astype(F32) * inv[None, :]
    cr, sr, cc, sc = jnp.cos(ang_r), jnp.sin(ang_r), jnp.cos(ang_c), jnp.sin(ang_c)
    cos = jnp.concatenate([cr, cr, cc, cc], axis=1)
    sin = jnp.concatenate([-sr, sr, -sc, sc], axis=1)
    cos = jnp.concatenate([jnp.ones((n_ctx, HEAD_DIM), F32), cos], axis=0)
    sin = jnp.concatenate([jnp.zeros((n_ctx, HEAD_DIM), F32), sin], axis=0)
    return jnp.tile(cos, (1, 2)), jnp.tile(sin, (1, 2))


def _rope_qkv(p, cos, sin):
    n_rows = p.shape[0]
    tm = _row_tile(n_rows, (1280, 1024, 512, 256))
    return pl.pallas_call(
        _rope_kernel,
        grid=(n_rows // tm,),
        in_specs=[pl.BlockSpec((tm, ATT_W), lambda i: (i, COL_Q // ATT_W)),
                  pl.BlockSpec((tm, ATT_KV_W), lambda i: (i, COL_K // ATT_KV_W)),
                  pl.BlockSpec((tm, ATT_KV_W), lambda i: (i, COL_V // ATT_KV_W)),
                  pl.BlockSpec((tm, 128), lambda i: (i, 0)),
                  pl.BlockSpec((tm, 128), lambda i: (i, 0))],
        out_specs=[pl.BlockSpec((tm, ATT_W), lambda i: (i, 0)),
                   pl.BlockSpec((tm, ATT_KV_W), lambda i: (i, 0)),
                   pl.BlockSpec((tm, ATT_KV_W), lambda i: (i, 0))],
        out_shape=[jax.ShapeDtypeStruct((n_rows, ATT_W), BF16),
                   jax.ShapeDtypeStruct((n_rows, ATT_KV_W), BF16),
                   jax.ShapeDtypeStruct((n_rows, ATT_KV_W), BF16)],
        compiler_params=_params(1),
        name="rope_qkv",
    )(p, p, p, cos, sin)


def _attn_kernel(sink_ref, q_ref, kp_ref, kc_ref, kn_ref, vp_ref, vc_ref, vn_ref, kx_ref, vx_ref, o_ref,
                 *, n_ctx, n_lat):
    i = pl.program_id(0)
    blk = ATT_BLOCK
    ctx_blocks = n_ctx // blk
    is_lat = i >= ctx_blocks
    base = (i - ctx_blocks) * blk
    qi = lax.broadcasted_iota(jnp.int32, (blk, n_ctx + 3 * blk), 0)
    ci = lax.broadcasted_iota(jnp.int32, (blk, n_ctx + 3 * blk), 1)
    rel = ci - n_ctx - blk
    kpos = base + rel
    band = (jnp.abs(rel - qi) <= WINDOW) & (kpos >= 0) & (kpos < n_lat) & is_lat
    valid = (ci < n_ctx) | band
    valid = jnp.concatenate([valid] * ATT_GROUP, axis=0)

    q = q_ref[...]
    outs = [None] * ATT_HEADS
    for g in range(ATT_KV_HEADS):
        ks = slice(g * HEAD_DIM, (g + 1) * HEAD_DIM)
        kk = jnp.concatenate([kx_ref[:, ks], kp_ref[:, ks], kc_ref[:, ks], kn_ref[:, ks]], axis=0)
        vv = jnp.concatenate([vx_ref[:, ks], vp_ref[:, ks], vc_ref[:, ks], vn_ref[:, ks]], axis=0)
        heads = [g * ATT_GROUP + j for j in range(ATT_GROUP)]
        qq = jnp.concatenate([q[:, h * HEAD_DIM:(h + 1) * HEAD_DIM] for h in heads], axis=0)
        s = lax.dot_general(qq, kk, (((1,), (1,)), ((), ())), preferred_element_type=F32)
        s = jnp.where(valid, s, NEG_INF)
        sink = jnp.concatenate([jnp.full((blk, 1), sink_ref[h], F32) for h in heads], axis=0)
        m = jnp.maximum(jnp.max(s, axis=-1, keepdims=True), sink)
        e = jnp.exp(s - m)
        den = jnp.sum(e, axis=-1, keepdims=True) + jnp.exp(sink - m)
        o = jnp.dot(e.astype(BF16), vv, preferred_element_type=F32) / den
        for j, h in enumerate(heads):
            outs[h] = o[j * blk:(j + 1) * blk, :]
    o_ref[...] = jnp.concatenate(outs, axis=1).astype(o_ref.dtype)


def _window_attention(qr, kr, vb, sinks, n_ctx, n_lat):
    n_rows = qr.shape[0]
    nb = n_rows // ATT_BLOCK
    q_spec = pl.BlockSpec((ATT_BLOCK, ATT_W), lambda i: (i, 0))
    prev = pl.BlockSpec((ATT_BLOCK, ATT_KV_W), lambda i: (jnp.maximum(i - 1, 0), 0))
    cur = pl.BlockSpec((ATT_BLOCK, ATT_KV_W), lambda i: (i, 0))
    nxt = pl.BlockSpec((ATT_BLOCK, ATT_KV_W), lambda i: (jnp.minimum(i + 1, nb - 1), 0))
    ctx = pl.BlockSpec((n_ctx, ATT_KV_W), lambda i: (0, 0))
    return pl.pallas_call(
        functools.partial(_attn_kernel, n_ctx=n_ctx, n_lat=n_lat),
        grid=(nb,),
        in_specs=[pl.BlockSpec(memory_space=pltpu.SMEM), q_spec, prev, cur, nxt, prev, cur, nxt, ctx, ctx],
        out_specs=pl.BlockSpec((ATT_BLOCK, ATT_W), lambda i: (i, 0)),
        out_shape=jax.ShapeDtypeStruct((n_rows, ATT_W), BF16),
        compiler_params=_params(1),
        name="window_attention",
    )(sinks.astype(F32), qr, kr, kr, kr, vb, vb, vb, kr, vb)


def _per_head(x, fn):
    return jnp.concatenate([fn(x[:, h * HEAD_DIM:(h + 1) * HEAD_DIM]) for h in range(RWKV_HEADS)], axis=1)


def _dot_nt(a, b, precision=None):
    return lax.dot_general(a, b, (((1,), (1,)), ((), ())), precision=precision, preferred_element_type=F32)


def _dot_tn(a, b, precision=None):
    return lax.dot_general(a, b, (((0,), (0,)), ((), ())), precision=precision, preferred_element_type=F32)


def _dot(a, b, precision=None):
    return jnp.dot(a, b, precision=precision, preferred_element_type=F32)


def _iclr(ad, ibase, iup):
    return jax.nn.sigmoid(ibase + _dot(ad.astype(BF16), iup.astype(BF16)))


def _rwkv_direction(d, r, k, v, lr, dbase_ref, dup_ref, ibase_ref, iup_ref, kk_scale, k_a, s_ref, y_ref):
    c = r.shape[0]
    prec = HIGHEST
    wd = lr[:, d * DECAY_RANK:(d + 1) * DECAY_RANK]
    ad = lr[:, 2 * DECAY_RANK + d * ICLR_RANK:2 * DECAY_RANK + (d + 1) * ICLR_RANK]
    z = dbase_ref[d:d + 1, :] + _dot(jnp.tanh(wd).astype(BF16), dup_ref[d].astype(BF16))
    logw = -np.float32(np.exp(-0.5)) * jax.nn.sigmoid(z)
    a = _iclr(ad, ibase_ref[d:d + 1, :], iup_ref[d])
    kd = k * (1.0 + (a - 1.0) * k_a)
    b = kk_scale * a

    ti = lax.broadcasted_iota(jnp.int32, (c, c), 0)
    si = lax.broadcasted_iota(jnp.int32, (c, c), 1)
    if d == 0:
        before = si < ti
    else:
        before = si > ti
    upto = before | (si == ti)
    cum = _dot(upto.astype(F32), logw, HIGHEST)
    last = c - 1 if d == 0 else 0
    cum_end = cum[last:last + 1, :]
    w_in = jnp.exp(cum)
    w_inv = jnp.exp(-cum)
    alpha = jnp.exp(cum - logw) * kk_scale
    beta = b * w_inv
    kappa = kd * w_inv
    rho = r * w_in
    w_tail = jnp.exp(cum_end - cum)
    kappa_e = kd * w_tail
    beta_e = b * w_tail
    w_end = jnp.exp(cum_end)
    eye = (si == ti).astype(F32)

    ys = []
    for h in range(RWKV_HEADS):
        hs = slice(h * HEAD_DIM, (h + 1) * HEAD_DIM)
        al, be, ka, rh, vh = alpha[:, hs], beta[:, hs], kappa[:, hs], rho[:, hs], v[:, hs]
        s0 = s_ref[d, h]
        ab = jnp.where(before, _dot_nt(al, be, prec), 0.0)
        ak = jnp.where(before, _dot_nt(al, ka, prec), 0.0)
        rb = jnp.where(upto, _dot_nt(rh, be, prec), 0.0)
        rk = jnp.where(upto, _dot_nt(rh, ka, prec), 0.0)
        npow = -ab
        tinv = eye + npow
        for _ in range(5):
            npow = _dot(npow, npow, prec)
            tinv = tinv + _dot(tinv, npow, prec)
        rhs = _dot_nt(al, s0, prec) + _dot(ak, vh, prec)
        u = _dot(tinv, rhs, prec)
        ys.append(_dot_nt(rh, s0, prec) + _dot(rk, vh, prec) - _dot(rb, u, prec))
        s_ref[d, h] = (s0 * w_end[:, hs] + _dot_tn(vh, kappa_e[:, hs], prec) - _dot_tn(u, beta_e[:, hs], prec))
    y_ref[...] = jnp.concatenate(ys, axis=1)


def _rwkv_scan_kernel(rf_ref, kf_ref, vf_ref, lf_ref, rb_ref, kb_ref, vb_ref, lb_ref,
                      dbase_ref, dup_ref, ibase_ref, iup_ref, kk_ref, ka_ref, yf_ref, yb_ref, s_ref):
    @pl.when(pl.program_id(0) == 0)
    def _():
        s_ref[...] = jnp.zeros_like(s_ref)

    k_k = kk_ref[...]
    k_a = ka_ref[...]
    for d, (r_ref, k_ref, v_ref, l_ref, y_ref) in enumerate(
            ((rf_ref, kf_ref, vf_ref, lf_ref, yf_ref), (rb_ref, kb_ref, vb_ref, lb_ref, yb_ref))):
        k = k_ref[...]
        kk = k * k_k
        kk = _per_head(kk, lambda x: x / jnp.maximum(jnp.sqrt(jnp.sum(x * x, axis=-1, keepdims=True)), 1e-12))
        _rwkv_direction(d, r_ref[...], k, v_ref[...], l_ref[...], dbase_ref, dup_ref, ibase_ref, iup_ref,
                        kk, k_a, s_ref, y_ref)


def _rwkv_scan(p, decay_base, decay_up, iclr_base, iclr_up, k_k, k_a, n_ctx):
    n_rows = p.shape[0]
    c = RWKV_CHUNK
    n_chunks = n_rows // c
    ctx_chunks = n_ctx // c

    def fwd(g):
        return g

    def bwd(g):
        return jnp.where(g < ctx_chunks, ctx_chunks - 1 - g, ctx_chunks + n_chunks - 1 - g)

    def specs(order):
        return [pl.BlockSpec((c, RWKV_W), lambda g: (order(g), COL_R // RWKV_W)),
                pl.BlockSpec((c, RWKV_W), lambda g: (order(g), COL_RK // RWKV_W)),
                pl.BlockSpec((c, RWKV_W), lambda g: (order(g), COL_RV // RWKV_W)),
                pl.BlockSpec((c, LR_W), lambda g: (order(g), COL_LR // LR_W))]

    full = lambda shape: pl.BlockSpec(shape, lambda g: (0,) * len(shape))
    return pl.pallas_call(
        _rwkv_scan_kernel,
        grid=(n_chunks,),
        in_specs=specs(fwd) + specs(bwd) + [
            full((2, RWKV_W)), full((2, DECAY_RANK, RWKV_W)), full((2, RWKV_W)), full((2, ICLR_RANK, RWKV_W)),
            full((1, RWKV_W)), full((1, RWKV_W))],
        out_specs=[pl.BlockSpec((c, RWKV_W), lambda g: (fwd(g), 0)),
                   pl.BlockSpec((c, RWKV_W), lambda g: (bwd(g), 0))],
        out_shape=[jax.ShapeDtypeStruct((n_rows, RWKV_W), F32)] * 2,
        scratch_shapes=[pltpu.VMEM((2, RWKV_HEADS, HEAD_DIM, HEAD_DIM), F32)],
        compiler_params=_params(1),
        name="rwkv7_chunk_scan",
    )(p, p, p, p, p, p, p, p, decay_base, decay_up, iclr_base, iclr_up,
      k_k.reshape(1, RWKV_W), k_a.reshape(1, RWKV_W))


def _rwkv_out_kernel(yf_ref, yb_ref, r_ref, k_ref, v_ref, lr_ref, ibase_ref, iup_ref, ka_ref, rk_ref,
                     lw_ref, lb_ref, gup_ref, o_ref):
    y = yf_ref[...] + yb_ref[...]
    r = r_ref[...]
    k = k_ref[...]
    v = v_ref[...]
    lr = lr_ref[...]
    k_a = ka_ref[...]

    def group_norm(x):
        mu = jnp.mean(x, axis=-1, keepdims=True)
        xc = x - mu
        var = jnp.mean(xc * xc, axis=-1, keepdims=True)
        return xc * lax.rsqrt(var + RWKV_GN_EPS)

    yn = _per_head(y, group_norm) * lw_ref[...] + lb_ref[...]
    kd_sum = jnp.zeros_like(k)
    for d in range(2):
        ad = lr[:, 2 * DECAY_RANK + d * ICLR_RANK:2 * DECAY_RANK + (d + 1) * ICLR_RANK]
        a = _iclr(ad, ibase_ref[d:d + 1, :], iup_ref[d])
        kd_sum = kd_sum + k * (1.0 + (a - 1.0) * k_a)
    rkk = r * kd_sum * rk_ref[...]
    bonus = _per_head(rkk, lambda x: jnp.broadcast_to(jnp.sum(x, axis=-1, keepdims=True), x.shape)) * v
    gd = lr[:, 4 * DECAY_RANK:4 * DECAY_RANK + GATE_RANK]
    gate = _dot(jax.nn.sigmoid(gd).astype(BF16), gup_ref[...].astype(BF16))
    o_ref[...] = ((yn + bonus) * gate).astype(o_ref.dtype)


def _rwkv_out(yf, yb, p, iclr_base, iclr_up, k_a, r_k, lnx_w, lnx_b, gate_up):
    n_rows = p.shape[0]
    tm = 256
    row = lambda w, cb: pl.BlockSpec((tm, w), lambda i, cb=cb: (i, cb))
    full = lambda shape: pl.BlockSpec(shape, lambda i: (0,) * len(shape))
    vec = lambda t: t.reshape(1, RWKV_W)
    return pl.pallas_call(
        _rwkv_out_kernel,
        grid=(n_rows // tm,),
        in_specs=[row(RWKV_W, 0), row(RWKV_W, 0), row(RWKV_W, COL_R // RWKV_W), row(RWKV_W, COL_RK // RWKV_W),
                  row(RWKV_W, COL_RV // RWKV_W), row(LR_W, COL_LR // LR_W),
                  full((2, RWKV_W)), full((2, ICLR_RANK, RWKV_W)), full((1, RWKV_W)), full((1, RWKV_W)),
                  full((1, RWKV_W)), full((1, RWKV_W)), full((GATE_RANK, RWKV_W))],
        out_specs=pl.BlockSpec((tm, RWKV_W), lambda i: (i, 0)),
        out_shape=jax.ShapeDtypeStruct((n_rows, RWKV_W), BF16),
        compiler_params=_params(1),
        name="rwkv7_out",
    )(yf, yb, p, p, p, p, iclr_base, iclr_up, vec(k_a), vec(r_k), vec(lnx_w), vec(lnx_b), gate_up)


def _out_proj_kernel(a_ref, b_ref, c_ref, wa_ref, wb_ref, wc_ref, x_ref, gate_ref, o_ref, *, n_ctx, tm):
    acc = _dot(a_ref[...], wa_ref[...]) + _dot(b_ref[...], wb_ref[...]) + _dot(c_ref[...], wc_ref[...])
    row = pl.program_id(0) * tm + lax.broadcasted_iota(jnp.int32, (tm, 1), 0)
    gate = jnp.where(row < n_ctx, gate_ref[1:2, :], gate_ref[0:1, :])
    o_ref[...] = x_ref[...] + gate * acc


def _out_proj(a, b, c, w_out_bf16, xs, gate, n_ctx):
    n_rows, d = xs.shape
    tm = _row_tile(n_rows, (1280, 1024, 512, 256))
    tn = 512
    wa, wb, wc = w_out_bf16[:CONV_W], w_out_bf16[CONV_W:CONV_W + ATT_W], w_out_bf16[CONV_W + ATT_W:]
    return pl.pallas_call(
        functools.partial(_out_proj_kernel, n_ctx=n_ctx, tm=tm),
        grid=(n_rows // tm, d // tn),
        in_specs=[pl.BlockSpec((tm, CONV_W), lambda i, j: (i, 0)),
                  pl.BlockSpec((tm, ATT_W), lambda i, j: (i, 0)),
                  pl.BlockSpec((tm, RWKV_W), lambda i, j: (i, 0)),
                  pl.BlockSpec((CONV_W, tn), lambda i, j: (0, j)),
                  pl.BlockSpec((ATT_W, tn), lambda i, j: (0, j)),
                  pl.BlockSpec((RWKV_W, tn), lambda i, j: (0, j)),
                  pl.BlockSpec((tm, tn), lambda i, j: (i, j)),
                  pl.BlockSpec((8, tn), lambda i, j: (0, j))],
        out_specs=pl.BlockSpec((tm, tn), lambda i, j: (i, j)),
        out_shape=jax.ShapeDtypeStruct((n_rows, d), F32),
        compiler_params=_params(2),
        name="out_proj_residual",
    )(a, b, c, wa, wb, wc, xs, gate)


def _first_max(x, idx):
    m = jnp.max(x, axis=0, keepdims=True)
    first = jnp.min(jnp.where(x == m, idx, N_EXPERTS), axis=0, keepdims=True)
    return m, first


def _router_kernel(x_ref, g_ref, mod_ref, rw_ref, rb_ref, h_ref, idx_ref, gate_ref, *, n_ctx, tm):
    h = _norm_mod(x_ref[...], g_ref[...], mod_ref[...], pl.program_id(0) * tm, n_ctx)
    h_ref[...] = h.astype(BF16)
    logits = _dot_nt(rw_ref[...], h, HIGHEST)
    scores = jax.nn.sigmoid(logits)
    sel = scores + rb_ref[...]
    eidx = lax.broadcasted_iota(jnp.int32, sel.shape, 0)
    best = best_score = None
    for g in range(N_EXPERT_GROUPS):
        rows = slice(g * EXPERTS_PER_GROUP, (g + 1) * EXPERTS_PER_GROUP)
        x = sel[rows]
        xi = g * EXPERTS_PER_GROUP + lax.broadcasted_iota(jnp.int32, x.shape, 0)
        m1, i1 = _first_max(x, xi)
        m2, _ = _first_max(jnp.where(xi == i1, -jnp.inf, x), xi)
        score = m1 + m2
        if g == 0:
            best, best_score = jnp.zeros_like(i1), score
        else:
            better = score > best_score
            best = jnp.where(better, g, best)
            best_score = jnp.where(better, score, best_score)
    masked = jnp.where(eidx // EXPERTS_PER_GROUP == best, sel, NEG_INF)
    _, e1 = _first_max(masked, eidx)
    _, e2 = _first_max(jnp.where(eidx == e1, -jnp.inf, masked), eidx)
    g1 = jnp.sum(jnp.where(eidx == e1, scores, 0.0), axis=0, keepdims=True)
    g2 = jnp.sum(jnp.where(eidx == e2, scores, 0.0), axis=0, keepdims=True)
    idx_ref[0:1, :] = e1
    idx_ref[1:2, :] = e2
    gate_ref[0:1, :] = g1 / (g1 + g2)
    gate_ref[1:2, :] = g2 / (g1 + g2)


def _route(xs, g, mod, router_w, router_b, n_ctx):
    n_rows, d = xs.shape
    tm = 256
    return pl.pallas_call(
        functools.partial(_router_kernel, n_ctx=n_ctx, tm=tm),
        grid=(n_rows // tm,),
        in_specs=[pl.BlockSpec((tm, d), lambda i: (i, 0)),
                  pl.BlockSpec((1, d), lambda i: (0, 0)),
                  pl.BlockSpec((8, d), lambda i: (0, 0)),
                  pl.BlockSpec((N_EXPERTS, d), lambda i: (0, 0)),
                  pl.BlockSpec((N_EXPERTS, 1), lambda i: (0, 0))],
        out_specs=[pl.BlockSpec((tm, d), lambda i: (i, 0)),
                   pl.BlockSpec((TOP_K, tm), lambda i: (0, i)),
                   pl.BlockSpec((TOP_K, tm), lambda i: (0, i))],
        out_shape=[jax.ShapeDtypeStruct((n_rows, d), BF16),
                   jax.ShapeDtypeStruct((TOP_K, n_rows), jnp.int32),
                   jax.ShapeDtypeStruct((TOP_K, n_rows), F32)],
        compiler_params=_params(1),
        name="moe_norm_route",
    )(xs, g.reshape(1, d), mod, router_w.T, router_b.reshape(N_EXPERTS, 1))


def _expert_kernel(be_ref, nused_ref, x_ref, wg_ref, wu_ref, wd_ref, o_ref):
    i = pl.program_id(0)

    @pl.when(i < nused_ref[0])
    def _():
        x = x_ref[...]
        gate = _dot(x, wg_ref[0])
        up = _dot(x, wu_ref[0])
        act = (gate * jax.nn.sigmoid(gate) * up).astype(BF16)
        o_ref[...] = _dot(act, wd_ref[0])

    @pl.when(i >= nused_ref[0])
    def _():
        o_ref[...] = jnp.zeros_like(o_ref)


def _expert_ffn(xg, block_e, n_used, wg, wu, wd, tm):
    cap, d = xg.shape
    grid_spec = pltpu.PrefetchScalarGridSpec(
        num_scalar_prefetch=2,
        grid=(cap // tm,),
        in_specs=[pl.BlockSpec((tm, d), lambda i, be, nu: (i, 0)),
                  pl.BlockSpec((1, d, D_EXPERT), lambda i, be, nu: (be[i], 0, 0)),
                  pl.BlockSpec((1, d, D_EXPERT), lambda i, be, nu: (be[i], 0, 0)),
                  pl.BlockSpec((1, D_EXPERT, d), lambda i, be, nu: (be[i], 0, 0))],
        out_specs=pl.BlockSpec((tm, d), lambda i, be, nu: (i, 0)),
    )
    return pl.pallas_call(
        _expert_kernel,
        grid_spec=grid_spec,
        out_shape=jax.ShapeDtypeStruct((cap, d), F32),
        compiler_params=_params(1),
        name="moe_expert_ffn",
    )(block_e, n_used, xg, wg, wu, wd)


def _combine_kernel(x_ref, y0_ref, y1_ref, gt_ref, g2_ref, fg_ref, o_ref, *, n_ctx, tm, final_norm):
    row = pl.program_id(0) * tm + lax.broadcasted_iota(jnp.int32, (tm, 1), 0)
    g2 = jnp.where(row < n_ctx, g2_ref[1:2, :], g2_ref[0:1, :])
    gt = gt_ref[...]
    y = gt[:, 0:1] * y0_ref[...] + gt[:, 1:2] * y1_ref[...]
    x = x_ref[...] + g2 * y
    if final_norm:
        ms = jnp.mean(x * x, axis=-1, keepdims=True)
        x = x * lax.rsqrt(ms + RMS_EPS) * fg_ref[...]
    o_ref[...] = x


def _combine(xs, y0, y1, gates_t, g2, final_g, n_ctx, final_norm):
    n_rows, d = xs.shape
    tm = 256
    row = pl.BlockSpec((tm, d), lambda i: (i, 0))
    return pl.pallas_call(
        functools.partial(_combine_kernel, n_ctx=n_ctx, tm=tm, final_norm=final_norm),
        grid=(n_rows // tm,),
        in_specs=[row, row, row, pl.BlockSpec((tm, TOP_K), lambda i: (i, 0)),
                  pl.BlockSpec((8, d), lambda i: (0, 0)), pl.BlockSpec((1, d), lambda i: (0, 0))],
        out_specs=row,
        out_shape=jax.ShapeDtypeStruct((n_rows, d), F32),
        compiler_params=_params(1),
        name="moe_combine_residual",
    )(xs, y0, y1, gates_t, g2, final_g.reshape(1, d))


def _moe(xs, g, mod, g2, router_w, router_b, wg, wu, wd, final_g, n_ctx, final_norm):
    n_rows, d = xs.shape
    h, idx, gates = _route(xs, g, mod, router_w, router_b, n_ctx)
    tm = 256
    n_asg = n_rows * TOP_K
    n_blk = n_asg // tm + N_EXPERTS
    cap = n_blk * tm
    e_flat = idx.T.reshape(n_asg)
    onehot = (e_flat[:, None] == jnp.arange(N_EXPERTS)[None, :]).astype(jnp.int32)
    rank = jnp.take_along_axis(jnp.cumsum(onehot, axis=0), e_flat[:, None], axis=1)[:, 0] - 1
    counts = jnp.sum(onehot, axis=0)
    padded = (counts + tm - 1) // tm * tm
    pends = jnp.cumsum(padded)
    pstarts = pends - padded
    dest = pstarts[e_flat] + rank
    buf_tok = jnp.zeros((cap,), jnp.int32).at[dest].set(jnp.arange(n_asg, dtype=jnp.int32) // TOP_K)
    block_e = jnp.minimum(jnp.sum(pends[None, :] <= (jnp.arange(n_blk) * tm)[:, None], axis=1), N_EXPERTS - 1)
    n_used = (pends[-1] // tm).reshape(1)
    xg = jnp.take(h, buf_tok, axis=0)
    yb = _expert_ffn(xg, block_e.astype(jnp.int32), n_used.astype(jnp.int32), wg, wu, wd, tm)
    dest2 = dest.reshape(n_rows, TOP_K)
    y0 = jnp.take(yb, dest2[:, 0], axis=0)
    y1 = jnp.take(yb, dest2[:, 1], axis=0)
    return _combine(xs, y0, y1, gates.T, g2, final_g, n_ctx, final_norm)


def _mod_rows(mod_l, lat_chunks, ctx_chunks):
    d = D_MODEL
    rows = [mod_l[0, c * d:(c + 1) * d] for c in lat_chunks] + [mod_l[1, c * d:(c + 1) * d] for c in ctx_chunks]
    out = jnp.zeros((8, d), F32)
    return out.at[:len(rows)].set(jnp.stack(rows))


def kernel(x, c, ctx, c_ctx, ada_w, ada_b, norm1_g, norm2_g, w_in, w_out, conv_w, attn_sinks, decay_base,
           decay_up, iclr_base, iclr_up, gate_up, k_k, k_a, r_k, lnx_w, lnx_b, router_w, router_b,
           expert_gate, expert_up, expert_down, final_norm_g):
    bsz, n_lat, d = x.shape
    n_ctx = ctx.shape[1]
    depth = ada_w.shape[0]
    assert bsz == 1 and d == D_MODEL and n_ctx % ATT_BLOCK == 0 and n_lat % ATT_BLOCK == 0

    xs = jnp.concatenate([ctx[0], x[0]], axis=0)
    mods = _ada_mod(c, c_ctx, ada_w, ada_b)
    cos, sin = _rope_tables(n_ctx, n_lat)
    perm = _in_proj_perm()

    for l in range(depth):
        last = l == depth - 1
        mod1 = _mod_rows(mods[l], (0, 1), (0, 1))
        gate1 = _mod_rows(mods[l], (2,), (2,))
        mod2 = _mod_rows(mods[l], (3, 4), (3, 4))
        gate2 = _mod_rows(mods[l], (5,), (5,))

        w_in_l = jnp.pad(w_in[l][:, perm], ((0, 0), (0, IN_W_PAD - perm.size))).astype(BF16)
        p = _norm_mod_matmul(xs, norm1_g[l], mod1, w_in_l, n_ctx)

        a_mix = _short_conv(p, conv_w[l], n_ctx)
        qr, kr, vb = _rope_qkv(p, cos, sin)
        b_mix = _window_attention(qr, kr, vb, attn_sinks[l], n_ctx, n_lat)
        yf, yb = _rwkv_scan(p, decay_base[l], decay_up[l], iclr_base[l], iclr_up[l], k_k[l], k_a[l], n_ctx)
        c_mix = _rwkv_out(yf, yb, p, iclr_base[l], iclr_up[l], k_a[l], r_k[l].reshape(-1), lnx_w[l], lnx_b[l],
                          gate_up[l])
        xs = _out_proj(a_mix, b_mix, c_mix, w_out[l].astype(BF16), xs, gate1, n_ctx)

        xs = _moe(xs, norm2_g[l], mod2, gate2, router_w, router_b, expert_gate[l].astype(BF16),
                  expert_up[l].astype(BF16), expert_down[l].astype(BF16), final_norm_g, n_ctx, last)
    return xs[n_ctx:].reshape(bsz, n_lat, d)
```

```python
import functools

import numpy as np
import jax
import jax.numpy as jnp
from jax import lax
from jax.experimental import pallas as pl
from jax.experimental.pallas import tpu as pltpu

F32 = jnp.float32
BF16 = jnp.bfloat16
HIGHEST = lax.Precision.HIGHEST

D_MODEL = 2048
GRID_W = 64
CONV_W = D_MODEL // 4
CONV_K = 3
HEAD_DIM = 64
ATT_HEADS = 12
ATT_KV_HEADS = 4
ATT_GROUP = ATT_HEADS // ATT_KV_HEADS
ATT_W = ATT_HEADS * HEAD_DIM
ATT_KV_W = ATT_KV_HEADS * HEAD_DIM
RWKV_HEADS = 12
RWKV_W = RWKV_HEADS * HEAD_DIM
WINDOW = 128
ATT_BLOCK = 128
ROPE_THETA = 10000.0
ROPE_FREQS = HEAD_DIM // 4
DECAY_RANK = 64
ICLR_RANK = 64
GATE_RANK = 128
RWKV_GN_EPS = 64e-5
N_EXPERTS = 32
N_EXPERT_GROUPS = 4
EXPERTS_PER_GROUP = N_EXPERTS // N_EXPERT_GROUPS
TOP_K = 2
D_EXPERT = 768
RMS_EPS = 1e-6
NEG_INF = -1e30

COL_Q = 0
COL_R = 768
COL_RK = 1536
COL_RV = 2304
COL_CB = 3072
COL_CC = 3584
COL_CH = 4096
COL_K = 4608
COL_V = 4864
COL_LR = 5120
LR_W = 512
IN_W_PAD = 5632

RWKV_CHUNK = 64
HEADS_PER_GROUP = 4
NORM_ROWS = 256
CAST_ROWS = 256
VMEM_LIMIT = 56 * 1024 * 1024
EXPERT_VMEM_LIMIT = 62 * 1024 * 1024


def _params(n_axes):
    return pltpu.CompilerParams(dimension_semantics=("arbitrary",) * n_axes,
                                vmem_limit_bytes=VMEM_LIMIT)


def _row_tile(n_rows, candidates):
    for t in candidates:
        if n_rows % t == 0:
            return t
    raise ValueError(f"no row tile for {n_rows}")


def _in_proj_perm():
    o_cb, o_cc, o_ch = 0, 512, 1024
    o_q, o_k, o_v = 1536, 2304, 2560
    o_r, o_rk, o_rv = 2816, 3584, 4352
    o_lr = 5120
    segs = [(o_q, 768), (o_r, 768), (o_rk, 768), (o_rv, 768), (o_cb, 512), (o_cc, 512), (o_ch, 512),
            (o_k, 256), (o_v, 256), (o_lr, 384)]
    return np.concatenate([np.arange(o, o + w) for o, w in segs])


def _ada_kernel(s_ref, w_ref, b_ref, o_ref):
    o_ref[0] = jnp.dot(s_ref[...], w_ref[0], precision=HIGHEST, preferred_element_type=F32) + b_ref[0]


def _ada_mod(c, c_ctx, ada_w, ada_b):
    depth, d, n = ada_w.shape
    s = jnp.zeros((8, d), F32).at[0].set(c[0]).at[1].set(c_ctx)
    s = s * jax.nn.sigmoid(s)
    tn = 1024
    return pl.pallas_call(
        _ada_kernel,
        grid=(depth, n // tn),
        in_specs=[pl.BlockSpec((8, d), lambda l, j: (0, 0)),
                  pl.BlockSpec((1, d, tn), lambda l, j: (l, 0, j)),
                  pl.BlockSpec((1, 1, tn), lambda l, j: (l, 0, j))],
        out_specs=pl.BlockSpec((1, 8, tn), lambda l, j: (l, 0, j)),
        out_shape=jax.ShapeDtypeStruct((depth, 8, n), F32),
        compiler_params=_params(2),
        name="ada_mod",
    )(s, ada_w, ada_b.reshape(depth, 1, n))


def _norm_mod(x, g, mod, row0, n_ctx):
    ms = jnp.mean(x * x, axis=-1, keepdims=True)
    y = x * lax.rsqrt(ms + RMS_EPS) * g
    row = row0 + lax.broadcasted_iota(jnp.int32, (x.shape[0], 1), 0)
    is_ctx = row < n_ctx
    shift = jnp.where(is_ctx, mod[2:3, :], mod[0:1, :])
    scale = jnp.where(is_ctx, mod[3:4, :], mod[1:2, :])
    return y * (1.0 + scale) + shift


def _nmm_kernel(x_ref, g_ref, mod_ref, w_ref, o_ref, h_ref, *, n_ctx, tm):
    i = pl.program_id(0)

    @pl.when(pl.program_id(1) == 0)
    def _():
        for r0 in range(0, tm, NORM_ROWS):
            rows = slice(r0, r0 + NORM_ROWS)
            h_ref[rows, :] = _norm_mod(x_ref[rows, :], g_ref[...], mod_ref[...], i * tm + r0, n_ctx).astype(BF16)

    o_ref[...] = jnp.dot(h_ref[...], w_ref[...], preferred_element_type=F32)


def _norm_mod_matmul(xs, g, mod, w_bf16, n_ctx):
    n_rows, d = xs.shape
    n_out = w_bf16.shape[1]
    tm = _row_tile(n_rows, (1280, 1024, 512, 256))
    tn = 512
    return pl.pallas_call(
        functools.partial(_nmm_kernel, n_ctx=n_ctx, tm=tm),
        grid=(n_rows // tm, n_out // tn),
        in_specs=[pl.BlockSpec((tm, d), lambda i, j: (i, 0)),
                  pl.BlockSpec((1, d), lambda i, j: (0, 0)),
                  pl.BlockSpec((8, d), lambda i, j: (0, 0)),
                  pl.BlockSpec((d, tn), lambda i, j: (0, j))],
        out_specs=pl.BlockSpec((tm, tn), lambda i, j: (i, j)),
        out_shape=jax.ShapeDtypeStruct((n_rows, n_out), F32),
        scratch_shapes=[pltpu.VMEM((tm, d), BF16)],
        compiler_params=_params(2),
        name="norm_mod_in_proj",
    )(xs, g.reshape(1, d), mod, w_bf16)


def _conv_kernel(cb_ref, cc_ref, ch_ref, ccp_ref, chp_ref, ccn_ref, chn_ref, w_ref, o_ref, *, n_ctx, n_rows, tm):
    i = pl.program_id(0)
    u = cc_ref[...] * ch_ref[...]
    u_prev_row = ccp_ref[7:8, :] * chp_ref[7:8, :]
    u_next_row = ccn_ref[0:1, :] * chn_ref[0:1, :]
    loc = lax.broadcasted_iota(jnp.int32, (tm, 1), 0)
    row = i * tm + loc
    up = jnp.where(loc == 0, u_prev_row, pltpu.roll(u, 1, axis=0))
    dn = jnp.where(loc == tm - 1, u_next_row, pltpu.roll(u, tm - 1, axis=0))
    up = jnp.where((row == 0) | (row == n_ctx), 0.0, up)
    dn = jnp.where((row == n_ctx - 1) | (row == n_rows - 1), 0.0, dn)
    w = w_ref[...]
    y = w[0:1, :] * up + w[1:2, :] * u + w[2:3, :] * dn
    o_ref[...] = (cb_ref[...] * y).astype(o_ref.dtype)


def _short_conv(p, conv_w, n_ctx):
    n_rows = p.shape[0]
    tm = _row_tile(n_rows, (1280, 1024, 512, 256))
    r8 = tm // 8
    last8 = n_rows // 8 - 1
    wpad = jnp.zeros((8, CONV_W), F32).at[:CONV_K].set(conv_w)
    blk = lambda c: pl.BlockSpec((tm, CONV_W), lambda i, c=c: (i, c))
    prev = lambda c: pl.BlockSpec((8, CONV_W), lambda i, c=c: (jnp.maximum(i * r8 - 1, 0), c))
    nxt = lambda c: pl.BlockSpec((8, CONV_W), lambda i, c=c: (jnp.minimum((i + 1) * r8, last8), c))
    cb, cc, ch = COL_CB // CONV_W, COL_CC // CONV_W, COL_CH // CONV_W
    return pl.pallas_call(
        functools.partial(_conv_kernel, n_ctx=n_ctx, n_rows=n_rows, tm=tm),
        grid=(n_rows // tm,),
        in_specs=[blk(cb), blk(cc), blk(ch), prev(cc), prev(ch), nxt(cc), nxt(ch),
                  pl.BlockSpec((8, CONV_W), lambda i: (0, 0))],
        out_specs=pl.BlockSpec((tm, CONV_W), lambda i: (i, 0)),
        out_shape=jax.ShapeDtypeStruct((n_rows, CONV_W), BF16),
        compiler_params=_params(1),
        name="short_conv",
    )(p, p, p, p, p, p, p, wpad)


def _swap_halves(x):
    n = x.shape[1]
    lane = lax.broadcasted_iota(jnp.int32, x.shape, 1)
    fwd = pltpu.roll(x, n - ROPE_FREQS, axis=1)
    bwd = pltpu.roll(x, ROPE_FREQS, axis=1)
    return jnp.where((lane % (2 * ROPE_FREQS)) < ROPE_FREQS, fwd, bwd)


def _rope_kernel(q_ref, k_ref, v_ref, cos_ref, sin_ref, qo_ref, ko_ref, vo_ref):
    cos = cos_ref[...]
    sin = sin_ref[...]
    q = q_ref[...]
    k = k_ref[...]
    cos_q = jnp.concatenate([cos] * (ATT_W // 128), axis=1)
    sin_q = jnp.concatenate([sin] * (ATT_W // 128), axis=1)
    cos_k = jnp.concatenate([cos] * (ATT_KV_W // 128), axis=1)
    sin_k = jnp.concatenate([sin] * (ATT_KV_W // 128), axis=1)
    qo_ref[...] = ((q * cos_q + _swap_halves(q) * sin_q) * (HEAD_DIM ** -0.5)).astype(BF16)
    ko_ref[...] = (k * cos_k + _swap_halves(k) * sin_k).astype(BF16)
    vo_ref[...] = v_ref[...].astype(BF16)


def _rope_tables(n_ctx, n_lat):
    row = jnp.repeat(jnp.arange(n_lat // GRID_W, dtype=jnp.int32), GRID_W)
    col = jnp.arange(n_lat, dtype=jnp.int32) % GRID_W
    inv = ROPE_THETA ** (-jnp.arange(ROPE_FREQS, dtype=F32) / ROPE_FREQS)
    ang_r = row[:, None].astype(F32) * inv[None, :]
    ang_c = col[:, None].astype(F32) * inv[None, :]
    cr, sr, cc, sc = jnp.cos(ang_r), jnp.sin(ang_r), jnp.cos(ang_c), jnp.sin(ang_c)
    cos = jnp.concatenate([cr, cr, cc, cc], axis=1)
    sin = jnp.concatenate([-sr, sr, -sc, sc], axis=1)
    cos = jnp.concatenate([jnp.ones((n_ctx, HEAD_DIM), F32), cos], axis=0)
    sin = jnp.concatenate([jnp.zeros((n_ctx, HEAD_DIM), F32), sin], axis=0)
    return jnp.tile(cos, (1, 2)), jnp.tile(sin, (1, 2))


def _rope_qkv(p, cos, sin):
    n_rows = p.shape[0]
    tm = _row_tile(n_rows, (1280, 1024, 512, 256))
    return pl.pallas_call(
        _rope_kernel,
        grid=(n_rows // tm,),
        in_specs=[pl.BlockSpec((tm, ATT_W), lambda i: (i, COL_Q // ATT_W)),
                  pl.BlockSpec((tm, ATT_KV_W), lambda i: (i, COL_K // ATT_KV_W)),
                  pl.BlockSpec((tm, ATT_KV_W), lambda i: (i, COL_V // ATT_KV_W)),
                  pl.BlockSpec((tm, 128), lambda i: (i, 0)),
                  pl.BlockSpec((tm, 128), lambda i: (i, 0))],
        out_specs=[pl.BlockSpec((tm, ATT_W), lambda i: (i, 0)),
                   pl.BlockSpec((tm, ATT_KV_W), lambda i: (i, 0)),
                   pl.BlockSpec((tm, ATT_KV_W), lambda i: (i, 0))],
        out_shape=[jax.ShapeDtypeStruct((n_rows, ATT_W), BF16),
                   jax.ShapeDtypeStruct((n_rows, ATT_KV_W), BF16),
                   jax.ShapeDtypeStruct((n_rows, ATT_KV_W), BF16)],
        compiler_params=_params(1),
        name="rope_qkv",
    )(p, p, p, cos, sin)


def _attn_kernel(sink_ref, q_ref, kp_ref, kc_ref, kn_ref, vp_ref, vc_ref, vn_ref, kx_ref, vx_ref, o_ref,
                 *, n_ctx, n_lat):
    i = pl.program_id(0)
    blk = ATT_BLOCK
    ctx_blocks = n_ctx // blk
    is_lat = i >= ctx_blocks
    base = (i - ctx_blocks) * blk
    qi = lax.broadcasted_iota(jnp.int32, (blk, n_ctx + 3 * blk), 0)
    ci = lax.broadcasted_iota(jnp.int32, (blk, n_ctx + 3 * blk), 1)
    rel = ci - n_ctx - blk
    kpos = base + rel
    band = (jnp.abs(rel - qi) <= WINDOW) & (kpos >= 0) & (kpos < n_lat) & is_lat
    valid = (ci < n_ctx) | band
    valid = jnp.concatenate([valid] * ATT_GROUP, axis=0)

    q = q_ref[...]
    outs = [None] * ATT_HEADS
    for g in range(ATT_KV_HEADS):
        ks = slice(g * HEAD_DIM, (g + 1) * HEAD_DIM)
        kk = jnp.concatenate([kx_ref[:, ks], kp_ref[:, ks], kc_ref[:, ks], kn_ref[:, ks]], axis=0)
        vv = jnp.concatenate([vx_ref[:, ks], vp_ref[:, ks], vc_ref[:, ks], vn_ref[:, ks]], axis=0)
        heads = [g * ATT_GROUP + j for j in range(ATT_GROUP)]
        qq = jnp.concatenate([q[:, h * HEAD_DIM:(h + 1) * HEAD_DIM] for h in heads], axis=0)
        s = lax.dot_general(qq, kk, (((1,), (1,)), ((), ())), preferred_element_type=F32)
        s = jnp.where(valid, s, NEG_INF)
        sink = jnp.concatenate([jnp.full((blk, 1), sink_ref[h], F32) for h in heads], axis=0)
        m = jnp.maximum(jnp.max(s, axis=-1, keepdims=True), sink)
        e = jnp.exp(s - m)
        den = jnp.sum(e, axis=-1, keepdims=True) + jnp.exp(sink - m)
        o = jnp.dot(e.astype(BF16), vv, preferred_element_type=F32) / den
        for j, h in enumerate(heads):
            outs[h] = o[j * blk:(j + 1) * blk, :]
    o_ref[...] = jnp.concatenate(outs, axis=1).astype(o_ref.dtype)


def _window_attention(qr, kr, vb, sinks, n_ctx, n_lat):
    n_rows = qr.shape[0]
    nb = n_rows // ATT_BLOCK
    q_spec = pl.BlockSpec((ATT_BLOCK, ATT_W), lambda i: (i, 0))
    prev = pl.BlockSpec((ATT_BLOCK, ATT_KV_W), lambda i: (jnp.maximum(i - 1, 0), 0))
    cur = pl.BlockSpec((ATT_BLOCK, ATT_KV_W), lambda i: (i, 0))
    nxt = pl.BlockSpec((ATT_BLOCK, ATT_KV_W), lambda i: (jnp.minimum(i + 1, nb - 1), 0))
    ctx = pl.BlockSpec((n_ctx, ATT_KV_W), lambda i: (0, 0))
    return pl.pallas_call(
        functools.partial(_attn_kernel, n_ctx=n_ctx, n_lat=n_lat),
        grid=(nb,),
        in_specs=[pl.BlockSpec(memory_space=pltpu.SMEM), q_spec, prev, cur, nxt, prev, cur, nxt, ctx, ctx],
        out_specs=pl.BlockSpec((ATT_BLOCK, ATT_W), lambda i: (i, 0)),
        out_shape=jax.ShapeDtypeStruct((n_rows, ATT_W), BF16),
        compiler_params=_params(1),
        name="window_attention",
    )(sinks.astype(F32), qr, kr, kr, kr, vb, vb, vb, kr, vb)


def _per_head(x, fn):
    return jnp.concatenate([fn(x[:, h * HEAD_DIM:(h + 1) * HEAD_DIM]) for h in range(RWKV_HEADS)], axis=1)


def _dot_nt(a, b, precision=None):
    return lax.dot_general(a, b, (((1,), (1,)), ((), ())), precision=precision, preferred_element_type=F32)


def _dot_tn(a, b, precision=None):
    return lax.dot_general(a, b, (((0,), (0,)), ((), ())), precision=precision, preferred_element_type=F32)


def _dot(a, b, precision=None):
    return jnp.dot(a, b, precision=precision, preferred_element_type=F32)


def _iclr(ad, ibase, iup):
    return jax.nn.sigmoid(ibase + _dot(ad.astype(BF16), iup.astype(BF16)))


def _block_diag(y, mask):
    return jnp.where(mask, jnp.concatenate([y] * HEADS_PER_GROUP, axis=0), jnp.zeros((), y.dtype))


def _rwkv_chunk_operands(d, r, k, v, lr, dbase_ref, dup_ref, ibase_ref, iup_ref, kk_scale, k_a):
    c = r.shape[0]
    wd = lr[:, d * DECAY_RANK:(d + 1) * DECAY_RANK]
    ad = lr[:, 2 * DECAY_RANK + d * ICLR_RANK:2 * DECAY_RANK + (d + 1) * ICLR_RANK]
    z = dbase_ref[d:d + 1, :] + _dot(jnp.tanh(wd).astype(BF16), dup_ref[d].astype(BF16))
    logw = -np.float32(np.exp(-0.5)) * jax.nn.sigmoid(z)
    a = _iclr(ad, ibase_ref[d:d + 1, :], iup_ref[d])
    kd = k * (1.0 + (a - 1.0) * k_a)
    b = kk_scale * a

    ti = lax.broadcasted_iota(jnp.int32, (c, c), 0)
    si = lax.broadcasted_iota(jnp.int32, (c, c), 1)
    tri = (si <= ti) if d == 0 else (si >= ti)
    cum = _dot(tri.astype(F32), logw, HIGHEST)
    last = c - 1 if d == 0 else 0
    cum_end = cum[last:last + 1, :]
    w_inv = jnp.exp(-cum)
    w_tail = jnp.exp(cum_end - cum)
    return dict(
        x=jnp.concatenate([(jnp.exp(cum - logw) * kk_scale).astype(BF16), (r * jnp.exp(cum)).astype(BF16)], axis=0),
        beta=(b * w_inv).astype(BF16),
        kappa=(kd * w_inv).astype(BF16),
        tail=jnp.concatenate([(kd * w_tail).astype(BF16), (-b * w_tail).astype(BF16)], axis=0),
        w_end=jnp.exp(cum_end),
        v=v.astype(BF16))


def _rwkv_chunk_update(ops, s_ref, y_refs):
    c = RWKV_CHUNK
    gw = HEADS_PER_GROUP * HEAD_DIM
    n_groups = RWKV_HEADS // HEADS_PER_GROUP
    chains = [(d, g) for g in range(n_groups) for d in range(2)]
    tp = lax.broadcasted_iota(jnp.int32, (c, HEADS_PER_GROUP * c), 0)
    sp = lax.broadcasted_iota(jnp.int32, (c, HEADS_PER_GROUP * c), 1) % c
    before = [sp < tp, sp > tp]
    upto = [sp <= tp, sp >= tp]
    eye = (sp == tp).astype(F32)
    bi = lax.broadcasted_iota(jnp.int32, (gw, gw), 0) // HEAD_DIM
    bj = lax.broadcasted_iota(jnp.int32, (gw, gw), 1) // HEAD_DIM
    diag = bi == bj
    cols = lambda g: slice(g * gw, (g + 1) * gw)

    gram_b, gram_k, sx, s0 = {}, {}, {}, {}
    for ch in chains:
        d, g = ch
        x = ops[d]["x"][:, cols(g)]
        gram_b[ch] = _dot_nt(x, _block_diag(ops[d]["beta"][:, cols(g)], diag))
        gram_k[ch] = _dot_nt(x, _block_diag(ops[d]["kappa"][:, cols(g)], diag))
        s0[ch] = s_ref[d, g]
        sx[ch] = _dot_nt(x, s0[ch].astype(BF16))

    npow = {ch: jnp.where(before[ch[0]], -gram_b[ch][:c], 0.0) for ch in chains}
    tinv = {ch: eye + npow[ch] for ch in chains}
    for _ in range(5):
        for ch in chains:
            nb = npow[ch].astype(BF16)
            npow[ch] = _dot(nb, _block_diag(nb, diag))
        for ch in chains:
            tinv[ch] = tinv[ch] + _dot(tinv[ch].astype(BF16), _block_diag(npow[ch].astype(BF16), diag))

    kv = {}
    for ch in chains:
        d, g = ch
        m2 = jnp.concatenate([jnp.where(before[d], gram_k[ch][:c], 0.0), jnp.where(upto[d], gram_k[ch][c:], 0.0)],
                             axis=0)
        kv[ch] = _dot(m2.astype(BF16), _block_diag(ops[d]["v"][:, cols(g)], diag))
    ub = {}
    for ch in chains:
        rhs = sx[ch][:c] + kv[ch][:c]
        ub[ch] = _dot(tinv[ch].astype(BF16), _block_diag(rhs.astype(BF16), diag)).astype(BF16)
    for ch in chains:
        d, g = ch
        rb = jnp.where(upto[d], gram_b[ch][c:], 0.0).astype(BF16)
        y_refs[d][:, cols(g)] = sx[ch][c:] + kv[ch][c:] - _dot(rb, _block_diag(ub[ch], diag))
    for ch in chains:
        d, g = ch
        upd = _dot_tn(jnp.concatenate([ops[d]["v"][:, cols(g)], ub[ch]], axis=0), ops[d]["tail"][:, cols(g)])
        s_ref[d, g] = s0[ch] * ops[d]["w_end"][:, cols(g)] + jnp.where(diag, upd, 0.0)


def _rwkv_scan_kernel(rf_ref, kf_ref, vf_ref, lf_ref, rb_ref, kb_ref, vb_ref, lb_ref,
                      dbase_ref, dup_ref, ibase_ref, iup_ref, kk_ref, ka_ref, yf_ref, yb_ref, s_ref):
    @pl.when(pl.program_id(0) == 0)
    def _():
        s_ref[...] = jnp.zeros_like(s_ref)

    k_k = kk_ref[...]
    k_a = ka_ref[...]
    ops = []
    for d, (r_ref, k_ref, v_ref, l_ref) in enumerate(
            ((rf_ref, kf_ref, vf_ref, lf_ref), (rb_ref, kb_ref, vb_ref, lb_ref))):
        k = k_ref[...]
        kk = k * k_k
        kk = _per_head(kk, lambda x: x / jnp.maximum(jnp.sqrt(jnp.sum(x * x, axis=-1, keepdims=True)), 1e-12))
        ops.append(_rwkv_chunk_operands(d, r_ref[...], k, v_ref[...], l_ref[...], dbase_ref, dup_ref, ibase_ref,
                                        iup_ref, kk, k_a))
    _rwkv_chunk_update(ops, s_ref, (yf_ref, yb_ref))


def _rwkv_scan(p, decay_base, decay_up, iclr_base, iclr_up, k_k, k_a, n_ctx):
    n_rows = p.shape[0]
    c = RWKV_CHUNK
    n_chunks = n_rows // c
    ctx_chunks = n_ctx // c

    def fwd(g):
        return g

    def bwd(g):
        return jnp.where(g < ctx_chunks, ctx_chunks - 1 - g, ctx_chunks + n_chunks - 1 - g)

    def specs(order):
        return [pl.BlockSpec((c, RWKV_W), lambda g: (order(g), COL_R // RWKV_W)),
                pl.BlockSpec((c, RWKV_W), lambda g: (order(g), COL_RK // RWKV_W)),
                pl.BlockSpec((c, RWKV_W), lambda g: (order(g), COL_RV // RWKV_W)),
                pl.BlockSpec((c, LR_W), lambda g: (order(g), COL_LR // LR_W))]

    full = lambda shape: pl.BlockSpec(shape, lambda g: (0,) * len(shape))
    return pl.pallas_call(
        _rwkv_scan_kernel,
        grid=(n_chunks,),
        in_specs=specs(fwd) + specs(bwd) + [
            full((2, RWKV_W)), full((2, DECAY_RANK, RWKV_W)), full((2, RWKV_W)), full((2, ICLR_RANK, RWKV_W)),
            full((1, RWKV_W)), full((1, RWKV_W))],
        out_specs=[pl.BlockSpec((c, RWKV_W), lambda g: (fwd(g), 0)),
                   pl.BlockSpec((c, RWKV_W), lambda g: (bwd(g), 0))],
        out_shape=[jax.ShapeDtypeStruct((n_rows, RWKV_W), F32)] * 2,
        scratch_shapes=[pltpu.VMEM((2, RWKV_HEADS // HEADS_PER_GROUP, HEADS_PER_GROUP * HEAD_DIM,
                                   HEADS_PER_GROUP * HEAD_DIM), F32)],
        compiler_params=_params(1),
        name="rwkv7_chunk_scan",
    )(p, p, p, p, p, p, p, p, decay_base, decay_up, iclr_base, iclr_up,
      k_k.reshape(1, RWKV_W), k_a.reshape(1, RWKV_W))


def _rwkv_out_kernel(yf_ref, yb_ref, r_ref, k_ref, v_ref, lr_ref, ibase_ref, iup_ref, ka_ref, rk_ref,
                     lw_ref, lb_ref, gup_ref, o_ref):
    y = yf_ref[...] + yb_ref[...]
    r = r_ref[...]
    k = k_ref[...]
    v = v_ref[...]
    lr = lr_ref[...]
    k_a = ka_ref[...]

    def group_norm(x):
        mu = jnp.mean(x, axis=-1, keepdims=True)
        xc = x - mu
        var = jnp.mean(xc * xc, axis=-1, keepdims=True)
        return xc * lax.rsqrt(var + RWKV_GN_EPS)

    yn = _per_head(y, group_norm) * lw_ref[...] + lb_ref[...]
    kd_sum = jnp.zeros_like(k)
    for d in range(2):
        ad = lr[:, 2 * DECAY_RANK + d * ICLR_RANK:2 * DECAY_RANK + (d + 1) * ICLR_RANK]
        a = _iclr(ad, ibase_ref[d:d + 1, :], iup_ref[d])
        kd_sum = kd_sum + k * (1.0 + (a - 1.0) * k_a)
    rkk = r * kd_sum * rk_ref[...]
    bonus = _per_head(rkk, lambda x: jnp.broadcast_to(jnp.sum(x, axis=-1, keepdims=True), x.shape)) * v
    gd = lr[:, 4 * DECAY_RANK:4 * DECAY_RANK + GATE_RANK]
    gate = _dot(jax.nn.sigmoid(gd).astype(BF16), gup_ref[...].astype(BF16))
    o_ref[...] = ((yn + bonus) * gate).astype(o_ref.dtype)


def _rwkv_out(yf, yb, p, iclr_base, iclr_up, k_a, r_k, lnx_w, lnx_b, gate_up):
    n_rows = p.shape[0]
    tm = 256
    row = lambda w, cb: pl.BlockSpec((tm, w), lambda i, cb=cb: (i, cb))
    full = lambda shape: pl.BlockSpec(shape, lambda i: (0,) * len(shape))
    vec = lambda t: t.reshape(1, RWKV_W)
    return pl.pallas_call(
        _rwkv_out_kernel,
        grid=(n_rows // tm,),
        in_specs=[row(RWKV_W, 0), row(RWKV_W, 0), row(RWKV_W, COL_R // RWKV_W), row(RWKV_W, COL_RK // RWKV_W),
                  row(RWKV_W, COL_RV // RWKV_W), row(LR_W, COL_LR // LR_W),
                  full((2, RWKV_W)), full((2, ICLR_RANK, RWKV_W)), full((1, RWKV_W)), full((1, RWKV_W)),
                  full((1, RWKV_W)), full((1, RWKV_W)), full((GATE_RANK, RWKV_W))],
        out_specs=pl.BlockSpec((tm, RWKV_W), lambda i: (i, 0)),
        out_shape=jax.ShapeDtypeStruct((n_rows, RWKV_W), BF16),
        compiler_params=_params(1),
        name="rwkv7_out",
    )(yf, yb, p, p, p, p, iclr_base, iclr_up, vec(k_a), vec(r_k), vec(lnx_w), vec(lnx_b), gate_up)


def _out_proj_kernel(a_ref, b_ref, c_ref, wa_ref, wb_ref, wc_ref, x_ref, gate_ref, o_ref, *, n_ctx, tm):
    acc = _dot(a_ref[...], wa_ref[...]) + _dot(b_ref[...], wb_ref[...]) + _dot(c_ref[...], wc_ref[...])
    row = pl.program_id(0) * tm + lax.broadcasted_iota(jnp.int32, (tm, 1), 0)
    gate = jnp.where(row < n_ctx, gate_ref[1:2, :], gate_ref[0:1, :])
    o_ref[...] = x_ref[...] + gate * acc


def _out_proj(a, b, c, w_out_bf16, xs, gate, n_ctx):
    n_rows, d = xs.shape
    tm = _row_tile(n_rows, (1280, 1024, 512, 256))
    tn = 512
    wa, wb, wc = w_out_bf16[:CONV_W], w_out_bf16[CONV_W:CONV_W + ATT_W], w_out_bf16[CONV_W + ATT_W:]
    return pl.pallas_call(
        functools.partial(_out_proj_kernel, n_ctx=n_ctx, tm=tm),
        grid=(n_rows // tm, d // tn),
        in_specs=[pl.BlockSpec((tm, CONV_W), lambda i, j: (i, 0)),
                  pl.BlockSpec((tm, ATT_W), lambda i, j: (i, 0)),
                  pl.BlockSpec((tm, RWKV_W), lambda i, j: (i, 0)),
                  pl.BlockSpec((CONV_W, tn), lambda i, j: (0, j)),
                  pl.BlockSpec((ATT_W, tn), lambda i, j: (0, j)),
                  pl.BlockSpec((RWKV_W, tn), lambda i, j: (0, j)),
                  pl.BlockSpec((tm, tn), lambda i, j: (i, j)),
                  pl.BlockSpec((8, tn), lambda i, j: (0, j))],
        out_specs=pl.BlockSpec((tm, tn), lambda i, j: (i, j)),
        out_shape=jax.ShapeDtypeStruct((n_rows, d), F32),
        compiler_params=_params(2),
        name="out_proj_residual",
    )(a, b, c, wa, wb, wc, xs, gate)


def _first_max(x, idx):
    m = jnp.max(x, axis=0, keepdims=True)
    first = jnp.min(jnp.where(x == m, idx, N_EXPERTS), axis=0, keepdims=True)
    return m, first


def _router_kernel(x_ref, g_ref, mod_ref, rw_ref, rb_ref, h_ref, idx_ref, gate_ref, rank_ref, cnt_ref, base_ref,
                   *, n_ctx, tm):
    @pl.when(pl.program_id(0) == 0)
    def _():
        base_ref[...] = jnp.zeros_like(base_ref)

    h = _norm_mod(x_ref[...], g_ref[...], mod_ref[...], pl.program_id(0) * tm, n_ctx)
    h_ref[...] = h.astype(BF16)
    logits = _dot_nt(rw_ref[...], h, HIGHEST)
    scores = jax.nn.sigmoid(logits)
    sel = scores + rb_ref[...]
    eidx = lax.broadcasted_iota(jnp.int32, sel.shape, 0)
    best = best_score = None
    for g in range(N_EXPERT_GROUPS):
        rows = slice(g * EXPERTS_PER_GROUP, (g + 1) * EXPERTS_PER_GROUP)
        x = sel[rows]
        xi = g * EXPERTS_PER_GROUP + lax.broadcasted_iota(jnp.int32, x.shape, 0)
        m1, i1 = _first_max(x, xi)
        m2, _ = _first_max(jnp.where(xi == i1, -jnp.inf, x), xi)
        score = m1 + m2
        if g == 0:
            best, best_score = jnp.zeros_like(i1), score
        else:
            better = score > best_score
            best = jnp.where(better, g, best)
            best_score = jnp.where(better, score, best_score)
    masked = jnp.where(eidx // EXPERTS_PER_GROUP == best, sel, NEG_INF)
    _, e1 = _first_max(masked, eidx)
    _, e2 = _first_max(jnp.where(eidx == e1, -jnp.inf, masked), eidx)
    g1 = jnp.sum(jnp.where(eidx == e1, scores, 0.0), axis=0, keepdims=True)
    g2 = jnp.sum(jnp.where(eidx == e2, scores, 0.0), axis=0, keepdims=True)
    idx_ref[0:1, :] = e1
    idx_ref[1:2, :] = e2
    gate_ref[0:1, :] = g1 / (g1 + g2)
    gate_ref[1:2, :] = g2 / (g1 + g2)

    si = lax.broadcasted_iota(jnp.int32, (tm, tm), 0)
    ti = lax.broadcasted_iota(jnp.int32, (tm, tm), 1)
    prefix = (si <= ti).astype(BF16)
    base = base_ref[...]
    for kth, e in enumerate((e1, e2)):
        hit = eidx == e
        seen = _dot(hit.astype(BF16), prefix)
        rank = jnp.sum(jnp.where(hit, seen - 1.0 + base, 0.0), axis=0, keepdims=True)
        rank_ref[kth:kth + 1, :] = rank.astype(jnp.int32)
        base = base + seen[:, tm - 1:tm]
    base_ref[...] = base
    cnt_ref[...] = base.astype(jnp.int32)


def _route(xs, g, mod, router_w, router_b, n_ctx):
    n_rows, d = xs.shape
    tm = 256
    return pl.pallas_call(
        functools.partial(_router_kernel, n_ctx=n_ctx, tm=tm),
        grid=(n_rows // tm,),
        in_specs=[pl.BlockSpec((tm, d), lambda i: (i, 0)),
                  pl.BlockSpec((1, d), lambda i: (0, 0)),
                  pl.BlockSpec((8, d), lambda i: (0, 0)),
                  pl.BlockSpec((N_EXPERTS, d), lambda i: (0, 0)),
                  pl.BlockSpec((N_EXPERTS, 1), lambda i: (0, 0))],
        out_specs=[pl.BlockSpec((tm, d), lambda i: (i, 0)),
                   pl.BlockSpec((TOP_K, tm), lambda i: (0, i)),
                   pl.BlockSpec((TOP_K, tm), lambda i: (0, i)),
                   pl.BlockSpec((TOP_K, tm), lambda i: (0, i)),
                   pl.BlockSpec((N_EXPERTS, 1), lambda i: (0, 0))],
        out_shape=[jax.ShapeDtypeStruct((n_rows, d), BF16),
                   jax.ShapeDtypeStruct((TOP_K, n_rows), jnp.int32),
                   jax.ShapeDtypeStruct((TOP_K, n_rows), F32),
                   jax.ShapeDtypeStruct((TOP_K, n_rows), jnp.int32),
                   jax.ShapeDtypeStruct((N_EXPERTS, 1), jnp.int32)],
        scratch_shapes=[pltpu.VMEM((N_EXPERTS, 1), F32)],
        compiler_params=_params(1),
        name="moe_norm_route",
    )(xs, g.reshape(1, d), mod, router_w.T, router_b.reshape(N_EXPERTS, 1))


def _expert_kernel(be_ref, nused_ref, x_ref, wg_ref, wu_ref, wd_ref, o_ref, wgb_ref, wub_ref, wdb_ref):
    i = pl.program_id(0)
    used = i < nused_ref[0]
    new_expert = (i == 0) | (be_ref[i] != be_ref[jnp.maximum(i - 1, 0)])

    @pl.when(used & new_expert)
    def _():
        for r0 in range(0, D_MODEL, CAST_ROWS):
            rows = slice(r0, r0 + CAST_ROWS)
            wgb_ref[rows, :] = wg_ref[0, rows, :].astype(BF16)
            wub_ref[rows, :] = wu_ref[0, rows, :].astype(BF16)
        for r0 in range(0, D_EXPERT, CAST_ROWS):
            rows = slice(r0, r0 + CAST_ROWS)
            wdb_ref[rows, :] = wd_ref[0, rows, :].astype(BF16)

    @pl.when(used)
    def _():
        x = x_ref[...]
        gate = _dot(x, wgb_ref[...])
        up = _dot(x, wub_ref[...])
        act = (gate * jax.nn.sigmoid(gate) * up).astype(BF16)
        o_ref[...] = _dot(act, wdb_ref[...]).astype(o_ref.dtype)

    @pl.when(i >= nused_ref[0])
    def _():
        o_ref[...] = jnp.zeros_like(o_ref)


def _expert_ffn(xg, block_e, n_used, wg, wu, wd, layer, tm):
    cap, d = xg.shape
    grid_spec = pltpu.PrefetchScalarGridSpec(
        num_scalar_prefetch=2,
        grid=(cap // tm,),
        in_specs=[pl.BlockSpec((tm, d), lambda i, be, nu: (i, 0)),
                  pl.BlockSpec((None, 1, d, D_EXPERT), lambda i, be, nu: (layer, be[i], 0, 0)),
                  pl.BlockSpec((None, 1, d, D_EXPERT), lambda i, be, nu: (layer, be[i], 0, 0)),
                  pl.BlockSpec((None, 1, D_EXPERT, d), lambda i, be, nu: (layer, be[i], 0, 0))],
        out_specs=pl.BlockSpec((tm, d), lambda i, be, nu: (i, 0)),
        scratch_shapes=[pltpu.VMEM((d, D_EXPERT), BF16), pltpu.VMEM((d, D_EXPERT), BF16),
                        pltpu.VMEM((D_EXPERT, d), BF16)],
    )
    return pl.pallas_call(
        _expert_kernel,
        grid_spec=grid_spec,
        out_shape=jax.ShapeDtypeStruct((cap, d), BF16),
        compiler_params=pltpu.CompilerParams(dimension_semantics=("arbitrary",),
                                             vmem_limit_bytes=EXPERT_VMEM_LIMIT),
        name="moe_expert_ffn",
    )(block_e, n_used, xg, wg, wu, wd)


def _combine_kernel(x_ref, y0_ref, y1_ref, gt_ref, g2_ref, fg_ref, o_ref, *, n_ctx, tm, final_norm):
    row = pl.program_id(0) * tm + lax.broadcasted_iota(jnp.int32, (tm, 1), 0)
    g2 = jnp.where(row < n_ctx, g2_ref[1:2, :], g2_ref[0:1, :])
    gt = gt_ref[...]
    y = gt[:, 0:1] * y0_ref[...] + gt[:, 1:2] * y1_ref[...]
    x = x_ref[...] + g2 * y
    if final_norm:
        ms = jnp.mean(x * x, axis=-1, keepdims=True)
        x = x * lax.rsqrt(ms + RMS_EPS) * fg_ref[...]
    o_ref[...] = x


def _combine(xs, y0, y1, gates_t, g2, final_g, n_ctx, final_norm):
    n_rows, d = xs.shape
    tm = 256
    row = pl.BlockSpec((tm, d), lambda i: (i, 0))
    return pl.pallas_call(
        functools.partial(_combine_kernel, n_ctx=n_ctx, tm=tm, final_norm=final_norm),
        grid=(n_rows // tm,),
        in_specs=[row, row, row, pl.BlockSpec((tm, TOP_K), lambda i: (i, 0)),
                  pl.BlockSpec((8, d), lambda i: (0, 0)), pl.BlockSpec((1, d), lambda i: (0, 0))],
        out_specs=row,
        out_shape=jax.ShapeDtypeStruct((n_rows, d), F32),
        compiler_params=_params(1),
        name="moe_combine_residual",
    )(xs, y0, y1, gates_t, g2, final_g.reshape(1, d))


def _moe(xs, g, mod, g2, router_w, router_b, wg, wu, wd, layer, final_g, n_ctx, final_norm):
    n_rows, d = xs.shape
    h, idx, gates, rank, counts = _route(xs, g, mod, router_w, router_b, n_ctx)
    tm = 256
    n_asg = n_rows * TOP_K
    n_blk = n_asg // tm + N_EXPERTS
    cap = n_blk * tm
    counts = counts[:, 0]
    padded = (counts + tm - 1) // tm * tm
    pends = jnp.cumsum(padded)
    pstarts = pends - padded
    dest = pstarts[idx] + rank
    tok = jnp.broadcast_to(jnp.arange(n_rows, dtype=jnp.int32)[None, :], dest.shape)
    buf_tok = jnp.zeros((cap,), jnp.int32).at[dest.reshape(-1)].set(tok.reshape(-1))
    block_e = jnp.minimum(jnp.sum(pends[None, :] <= (jnp.arange(n_blk) * tm)[:, None], axis=1), N_EXPERTS - 1)
    n_used = (pends[-1] // tm).reshape(1)
    xg = jnp.take(h, buf_tok, axis=0)
    yb = _expert_ffn(xg, block_e.astype(jnp.int32), n_used.astype(jnp.int32), wg, wu, wd, layer, tm)
    y0 = jnp.take(yb, dest[0], axis=0)
    y1 = jnp.take(yb, dest[1], axis=0)
    return _combine(xs, y0, y1, gates.T, g2, final_g, n_ctx, final_norm)


def _mod_rows(mod_l, lat_chunks, ctx_chunks):
    d = D_MODEL
    rows = [mod_l[0, c * d:(c + 1) * d] for c in lat_chunks] + [mod_l[1, c * d:(c + 1) * d] for c in ctx_chunks]
    out = jnp.zeros((8, d), F32)
    return out.at[:len(rows)].set(jnp.stack(rows))


def kernel(x, c, ctx, c_ctx, ada_w, ada_b, norm1_g, norm2_g, w_in, w_out, conv_w, attn_sinks, decay_base,
           decay_up, iclr_base, iclr_up, gate_up, k_k, k_a, r_k, lnx_w, lnx_b, router_w, router_b,
           expert_gate, expert_up, expert_down, final_norm_g):
    bsz, n_lat, d = x.shape
    n_ctx = ctx.shape[1]
    depth = ada_w.shape[0]
    assert bsz == 1 and d == D_MODEL and n_ctx % ATT_BLOCK == 0 and n_lat % ATT_BLOCK == 0

    xs = jnp.concatenate([ctx[0], x[0]], axis=0)
    mods = _ada_mod(c, c_ctx, ada_w, ada_b)
    cos, sin = _rope_tables(n_ctx, n_lat)
    perm = _in_proj_perm()

    for l in range(depth):
        last = l == depth - 1
        mod1 = _mod_rows(mods[l], (0, 1), (0, 1))
        gate1 = _mod_rows(mods[l], (2,), (2,))
        mod2 = _mod_rows(mods[l], (3, 4), (3, 4))
        gate2 = _mod_rows(mods[l], (5,), (5,))

        w_in_l = jnp.pad(w_in[l][:, perm], ((0, 0), (0, IN_W_PAD - perm.size))).astype(BF16)
        p = _norm_mod_matmul(xs, norm1_g[l], mod1, w_in_l, n_ctx)

        a_mix = _short_conv(p, conv_w[l], n_ctx)
        qr, kr, vb = _rope_qkv(p, cos, sin)
        b_mix = _window_attention(qr, kr, vb, attn_sinks[l], n_ctx, n_lat)
        yf, yb = _rwkv_scan(p, decay_base[l], decay_up[l], iclr_base[l], iclr_up[l], k_k[l], k_a[l], n_ctx)
        c_mix = _rwkv_out(yf, yb, p, iclr_base[l], iclr_up[l], k_a[l], r_k[l].reshape(-1), lnx_w[l], lnx_b[l],
                          gate_up[l])
        xs = _out_proj(a_mix, b_mix, c_mix, w_out[l].astype(BF16), xs, gate1, n_ctx)

        xs = _moe(xs, norm2_g[l], mod2, gate2, router_w, router_b, expert_gate, expert_up, expert_down, l,
                  final_norm_g, n_ctx, last)
    return xs[n_ctx:].reshape(bsz, n_lat, d)
```

```python
import functools

import numpy as np
import jax
import jax.numpy as jnp
from jax import lax
from jax.experimental import pallas as pl
from jax.experimental.pallas import tpu as pltpu

F32 = jnp.float32
BF16 = jnp.bfloat16
HIGHEST = lax.Precision.HIGHEST

D_MODEL = 2048
GRID_W = 64
CONV_W = D_MODEL // 4
CONV_K = 3
HEAD_DIM = 64
ATT_HEADS = 12
ATT_KV_HEADS = 4
ATT_GROUP = ATT_HEADS // ATT_KV_HEADS
ATT_W = ATT_HEADS * HEAD_DIM
ATT_KV_W = ATT_KV_HEADS * HEAD_DIM
RWKV_HEADS = 12
RWKV_W = RWKV_HEADS * HEAD_DIM
WINDOW = 128
ATT_BLOCK = 128
ROPE_THETA = 10000.0
ROPE_FREQS = HEAD_DIM // 4
DECAY_RANK = 64
ICLR_RANK = 64
GATE_RANK = 128
RWKV_GN_EPS = 64e-5
N_EXPERTS = 32
N_EXPERT_GROUPS = 4
EXPERTS_PER_GROUP = N_EXPERTS // N_EXPERT_GROUPS
TOP_K = 2
D_EXPERT = 768
RMS_EPS = 1e-6
NEG_INF = -1e30

COL_Q = 0
COL_R = 768
COL_RK = 1536
COL_RV = 2304
COL_CB = 3072
COL_CC = 3584
COL_CH = 4096
COL_K = 4608
COL_V = 4864
COL_LR = 5120
LR_W = 512
IN_W_PAD = 5632

RWKV_CHUNK = 64
HEADS_PER_GROUP = 4
NORM_ROWS = 256
CAST_ROWS = 256
VMEM_LIMIT = 56 * 1024 * 1024
EXPERT_VMEM_LIMIT = 62 * 1024 * 1024


def _params(n_axes):
    return pltpu.CompilerParams(dimension_semantics=("arbitrary",) * n_axes,
                                vmem_limit_bytes=VMEM_LIMIT)


def _row_tile(n_rows, candidates):
    for t in candidates:
        if n_rows % t == 0:
            return t
    raise ValueError(f"no row tile for {n_rows}")


def _in_proj_perm():
    o_cb, o_cc, o_ch = 0, 512, 1024
    o_q, o_k, o_v = 1536, 2304, 2560
    o_r, o_rk, o_rv = 2816, 3584, 4352
    o_lr = 5120
    segs = [(o_q, 768), (o_r, 768), (o_rk, 768), (o_rv, 768), (o_cb, 512), (o_cc, 512), (o_ch, 512),
            (o_k, 256), (o_v, 256), (o_lr, 384)]
    return np.concatenate([np.arange(o, o + w) for o, w in segs])


def _ada_kernel(s_ref, w_ref, b_ref, o_ref):
    o_ref[0] = jnp.dot(s_ref[...], w_ref[0], precision=HIGHEST, preferred_element_type=F32) + b_ref[0]


def _ada_mod(c, c_ctx, ada_w, ada_b):
    depth, d, n = ada_w.shape
    s = jnp.zeros((8, d), F32).at[0].set(c[0]).at[1].set(c_ctx)
    s = s * jax.nn.sigmoid(s)
    tn = 1024
    return pl.pallas_call(
        _ada_kernel,
        grid=(depth, n // tn),
        in_specs=[pl.BlockSpec((8, d), lambda l, j: (0, 0)),
                  pl.BlockSpec((1, d, tn), lambda l, j: (l, 0, j)),
                  pl.BlockSpec((1, 1, tn), lambda l, j: (l, 0, j))],
        out_specs=pl.BlockSpec((1, 8, tn), lambda l, j: (l, 0, j)),
        out_shape=jax.ShapeDtypeStruct((depth, 8, n), F32),
        compiler_params=_params(2),
        name="ada_mod",
    )(s, ada_w, ada_b.reshape(depth, 1, n))


def _norm_mod(x, g, mod, row0, n_ctx):
    ms = jnp.mean(x * x, axis=-1, keepdims=True)
    y = x * lax.rsqrt(ms + RMS_EPS) * g
    is_ctx = row0 < n_ctx
    shift = jnp.where(is_ctx, mod[2:3, :], mod[0:1, :])
    scale = jnp.where(is_ctx, mod[3:4, :], mod[1:2, :])
    return y * (1.0 + scale) + shift


def _nmm_kernel(x_ref, g_ref, mod_ref, w_ref, o_ref, h_ref, *, n_ctx, tm):
    i = pl.program_id(0)

    @pl.when(pl.program_id(1) == 0)
    def _():
        for r0 in range(0, tm, NORM_ROWS):
            rows = slice(r0, r0 + NORM_ROWS)
            h_ref[rows, :] = _norm_mod(x_ref[rows, :], g_ref[...], mod_ref[...], i * tm + r0, n_ctx).astype(BF16)

    o_ref[...] = jnp.dot(h_ref[...], w_ref[...], preferred_element_type=F32)


def _norm_mod_matmul(xs, g, mod, w_bf16, n_ctx):
    n_rows, d = xs.shape
    n_out = w_bf16.shape[1]
    tm = _row_tile(n_rows, (1280, 1024, 512, 256))
    tn = 512
    return pl.pallas_call(
        functools.partial(_nmm_kernel, n_ctx=n_ctx, tm=tm),
        grid=(n_rows // tm, n_out // tn),
        in_specs=[pl.BlockSpec((tm, d), lambda i, j: (i, 0)),
                  pl.BlockSpec((1, d), lambda i, j: (0, 0)),
                  pl.BlockSpec((8, d), lambda i, j: (0, 0)),
                  pl.BlockSpec((d, tn), lambda i, j: (0, j))],
        out_specs=pl.BlockSpec((tm, tn), lambda i, j: (i, j)),
        out_shape=jax.ShapeDtypeStruct((n_rows, n_out), F32),
        scratch_shapes=[pltpu.VMEM((tm, d), BF16)],
        compiler_params=_params(2),
        name="norm_mod_in_proj",
    )(xs, g.reshape(1, d), mod, w_bf16)


def _conv_kernel(cb_ref, cc_ref, ch_ref, ccp_ref, chp_ref, ccn_ref, chn_ref, w_ref, o_ref, *, n_ctx, n_rows, tm):
    i = pl.program_id(0)
    u = cc_ref[...] * ch_ref[...]
    u_prev_row = ccp_ref[7:8, :] * chp_ref[7:8, :]
    u_next_row = ccn_ref[0:1, :] * chn_ref[0:1, :]
    loc = lax.broadcasted_iota(jnp.int32, (tm, 1), 0)
    row = i * tm + loc
    up = jnp.where(loc == 0, u_prev_row, pltpu.roll(u, 1, axis=0))
    dn = jnp.where(loc == tm - 1, u_next_row, pltpu.roll(u, tm - 1, axis=0))
    up = jnp.where((row == 0) | (row == n_ctx), 0.0, up)
    dn = jnp.where((row == n_ctx - 1) | (row == n_rows - 1), 0.0, dn)
    w = w_ref[...]
    y = w[0:1, :] * up + w[1:2, :] * u + w[2:3, :] * dn
    o_ref[...] = (cb_ref[...] * y).astype(o_ref.dtype)


def _short_conv(p, conv_w, n_ctx):
    n_rows = p.shape[0]
    tm = _row_tile(n_rows, (1280, 1024, 512, 256))
    r8 = tm // 8
    last8 = n_rows // 8 - 1
    wpad = jnp.zeros((8, CONV_W), F32).at[:CONV_K].set(conv_w)
    blk = lambda c: pl.BlockSpec((tm, CONV_W), lambda i, c=c: (i, c))
    prev = lambda c: pl.BlockSpec((8, CONV_W), lambda i, c=c: (jnp.maximum(i * r8 - 1, 0), c))
    nxt = lambda c: pl.BlockSpec((8, CONV_W), lambda i, c=c: (jnp.minimum((i + 1) * r8, last8), c))
    cb, cc, ch = COL_CB // CONV_W, COL_CC // CONV_W, COL_CH // CONV_W
    return pl.pallas_call(
        functools.partial(_conv_kernel, n_ctx=n_ctx, n_rows=n_rows, tm=tm),
        grid=(n_rows // tm,),
        in_specs=[blk(cb), blk(cc), blk(ch), prev(cc), prev(ch), nxt(cc), nxt(ch),
                  pl.BlockSpec((8, CONV_W), lambda i: (0, 0))],
        out_specs=pl.BlockSpec((tm, CONV_W), lambda i: (i, 0)),
        out_shape=jax.ShapeDtypeStruct((n_rows, CONV_W), BF16),
        compiler_params=_params(1),
        name="short_conv",
    )(p, p, p, p, p, p, p, wpad)


def _swap_halves(x):
    n = x.shape[1]
    lane = lax.broadcasted_iota(jnp.int32, x.shape, 1)
    fwd = pltpu.roll(x, n - ROPE_FREQS, axis=1)
    bwd = pltpu.roll(x, ROPE_FREQS, axis=1)
    return jnp.where((lane % (2 * ROPE_FREQS)) < ROPE_FREQS, fwd, bwd)


def _rope_kernel(q_ref, k_ref, v_ref, cos_ref, sin_ref, qo_ref, ko_ref, vo_ref):
    cos = cos_ref[...]
    sin = sin_ref[...]
    q = q_ref[...]
    k = k_ref[...]
    cos_q = jnp.concatenate([cos] * (ATT_W // 128), axis=1)
    sin_q = jnp.concatenate([sin] * (ATT_W // 128), axis=1)
    cos_k = jnp.concatenate([cos] * (ATT_KV_W // 128), axis=1)
    sin_k = jnp.concatenate([sin] * (ATT_KV_W // 128), axis=1)
    qo_ref[...] = ((q * cos_q + _swap_halves(q) * sin_q) * (HEAD_DIM ** -0.5)).astype(BF16)
    ko_ref[...] = (k * cos_k + _swap_halves(k) * sin_k).astype(BF16)
    vo_ref[...] = v_ref[...].astype(BF16)


def _rope_tables(n_ctx, n_lat):
    row = jnp.repeat(jnp.arange(n_lat // GRID_W, dtype=jnp.int32), GRID_W)
    col = jnp.arange(n_lat, dtype=jnp.int32) % GRID_W
    inv = ROPE_THETA ** (-jnp.arange(ROPE_FREQS, dtype=F32) / ROPE_FREQS)
    ang_r = row[:, None].astype(F32) * inv[None, :]
    ang_c = col[:, None].astype(F32) * inv[None, :]
    cr, sr, cc, sc = jnp.cos(ang_r), jnp.sin(ang_r), jnp.cos(ang_c), jnp.sin(ang_c)
    cos = jnp.concatenate([cr, cr, cc, cc], axis=1)
    sin = jnp.concatenate([-sr, sr, -sc, sc], axis=1)
    cos = jnp.concatenate([jnp.ones((n_ctx, HEAD_DIM), F32), cos], axis=0)
    sin = jnp.concatenate([jnp.zeros((n_ctx, HEAD_DIM), F32), sin], axis=0)
    return jnp.tile(cos, (1, 2)), jnp.tile(sin, (1, 2))


def _rope_qkv(p, cos, sin):
    n_rows = p.shape[0]
    tm = _row_tile(n_rows, (1280, 1024, 512, 256))
    return pl.pallas_call(
        _rope_kernel,
        grid=(n_rows // tm,),
        in_specs=[pl.BlockSpec((tm, ATT_W), lambda i: (i, COL_Q // ATT_W)),
                  pl.BlockSpec((tm, ATT_KV_W), lambda i: (i, COL_K // ATT_KV_W)),
                  pl.BlockSpec((tm, ATT_KV_W), lambda i: (i, COL_V // ATT_KV_W)),
                  pl.BlockSpec((tm, 128), lambda i: (i, 0)),
                  pl.BlockSpec((tm, 128), lambda i: (i, 0))],
        out_specs=[pl.BlockSpec((tm, ATT_W), lambda i: (i, 0)),
                   pl.BlockSpec((tm, ATT_KV_W), lambda i: (i, 0)),
                   pl.BlockSpec((tm, ATT_KV_W), lambda i: (i, 0))],
        out_shape=[jax.ShapeDtypeStruct((n_rows, ATT_W), BF16),
                   jax.ShapeDtypeStruct((n_rows, ATT_KV_W), BF16),
                   jax.ShapeDtypeStruct((n_rows, ATT_KV_W), BF16)],
        compiler_params=_params(1),
        name="rope_qkv",
    )(p, p, p, cos, sin)


def _attn_kernel(sink_ref, q_ref, kp_ref, kc_ref, kn_ref, vp_ref, vc_ref, vn_ref, kx_ref, vx_ref, o_ref,
                 *, n_ctx, n_lat):
    i = pl.program_id(0)
    blk = ATT_BLOCK
    ctx_blocks = n_ctx // blk
    is_lat = i >= ctx_blocks
    base = (i - ctx_blocks) * blk
    qi = lax.broadcasted_iota(jnp.int32, (blk, n_ctx + 3 * blk), 0)
    ci = lax.broadcasted_iota(jnp.int32, (blk, n_ctx + 3 * blk), 1)
    rel = ci - n_ctx - blk
    kpos = base + rel
    band = (jnp.abs(rel - qi) <= WINDOW) & (kpos >= 0) & (kpos < n_lat) & is_lat
    valid = (ci < n_ctx) | band
    valid = jnp.concatenate([valid] * ATT_GROUP, axis=0)

    q = q_ref[...]
    outs = [None] * ATT_HEADS
    for g in range(ATT_KV_HEADS):
        ks = slice(g * HEAD_DIM, (g + 1) * HEAD_DIM)
        kk = jnp.concatenate([kx_ref[:, ks], kp_ref[:, ks], kc_ref[:, ks], kn_ref[:, ks]], axis=0)
        vv = jnp.concatenate([vx_ref[:, ks], vp_ref[:, ks], vc_ref[:, ks], vn_ref[:, ks]], axis=0)
        heads = [g * ATT_GROUP + j for j in range(ATT_GROUP)]
        qq = jnp.concatenate([q[:, h * HEAD_DIM:(h + 1) * HEAD_DIM] for h in heads], axis=0)
        s = lax.dot_general(qq, kk, (((1,), (1,)), ((), ())), preferred_element_type=F32)
        s = jnp.where(valid, s, NEG_INF)
        sink = jnp.concatenate([jnp.full((blk, 1), sink_ref[h], F32) for h in heads], axis=0)
        m = jnp.maximum(jnp.max(s, axis=-1, keepdims=True), sink)
        e = jnp.exp(s - m)
        den = jnp.sum(e, axis=-1, keepdims=True) + jnp.exp(sink - m)
        o = jnp.dot(e.astype(BF16), vv, preferred_element_type=F32) / den
        for j, h in enumerate(heads):
            outs[h] = o[j * blk:(j + 1) * blk, :]
    o_ref[...] = jnp.concatenate(outs, axis=1).astype(o_ref.dtype)


def _window_attention(qr, kr, vb, sinks, n_ctx, n_lat):
    n_rows = qr.shape[0]
    nb = n_rows // ATT_BLOCK
    q_spec = pl.BlockSpec((ATT_BLOCK, ATT_W), lambda i: (i, 0))
    prev = pl.BlockSpec((ATT_BLOCK, ATT_KV_W), lambda i: (jnp.maximum(i - 1, 0), 0))
    cur = pl.BlockSpec((ATT_BLOCK, ATT_KV_W), lambda i: (i, 0))
    nxt = pl.BlockSpec((ATT_BLOCK, ATT_KV_W), lambda i: (jnp.minimum(i + 1, nb - 1), 0))
    ctx = pl.BlockSpec((n_ctx, ATT_KV_W), lambda i: (0, 0))
    return pl.pallas_call(
        functools.partial(_attn_kernel, n_ctx=n_ctx, n_lat=n_lat),
        grid=(nb,),
        in_specs=[pl.BlockSpec(memory_space=pltpu.SMEM), q_spec, prev, cur, nxt, prev, cur, nxt, ctx, ctx],
        out_specs=pl.BlockSpec((ATT_BLOCK, ATT_W), lambda i: (i, 0)),
        out_shape=jax.ShapeDtypeStruct((n_rows, ATT_W), BF16),
        compiler_params=_params(1),
        name="window_attention",
    )(sinks.astype(F32), qr, kr, kr, kr, vb, vb, vb, kr, vb)


def _per_head(x, fn):
    return jnp.concatenate([fn(x[:, h * HEAD_DIM:(h + 1) * HEAD_DIM]) for h in range(RWKV_HEADS)], axis=1)


def _dot_nt(a, b, precision=None):
    return lax.dot_general(a, b, (((1,), (1,)), ((), ())), precision=precision, preferred_element_type=F32)


def _dot_tn(a, b, precision=None):
    return lax.dot_general(a, b, (((0,), (0,)), ((), ())), precision=precision, preferred_element_type=F32)


def _dot(a, b, precision=None):
    return jnp.dot(a, b, precision=precision, preferred_element_type=F32)


def _iclr(ad, ibase, iup):
    return jax.nn.sigmoid(ibase + _dot(ad.astype(BF16), iup.astype(BF16)))


def _block_diag(y, mask):
    return jnp.where(mask, jnp.concatenate([y] * HEADS_PER_GROUP, axis=0), jnp.zeros((), y.dtype))


def _rwkv_chunk_operands(d, r, k, v, lr, dbase_ref, dup_ref, ibase_ref, iup_ref, kk_scale, k_a):
    c = r.shape[0]
    wd = lr[:, d * DECAY_RANK:(d + 1) * DECAY_RANK]
    ad = lr[:, 2 * DECAY_RANK + d * ICLR_RANK:2 * DECAY_RANK + (d + 1) * ICLR_RANK]
    z = dbase_ref[d:d + 1, :] + _dot(jnp.tanh(wd).astype(BF16), dup_ref[d].astype(BF16))
    logw = -np.float32(np.exp(-0.5)) * jax.nn.sigmoid(z)
    a = _iclr(ad, ibase_ref[d:d + 1, :], iup_ref[d])
    kd = k * (1.0 + (a - 1.0) * k_a)
    b = kk_scale * a

    ti = lax.broadcasted_iota(jnp.int32, (c, c), 0)
    si = lax.broadcasted_iota(jnp.int32, (c, c), 1)
    tri = (si <= ti) if d == 0 else (si >= ti)
    cum = _dot(tri.astype(F32), logw, HIGHEST)
    last = c - 1 if d == 0 else 0
    cum_end = cum[last:last + 1, :]
    w_inv = jnp.exp(-cum)
    w_tail = jnp.exp(cum_end - cum)
    return dict(
        x=jnp.concatenate([(jnp.exp(cum - logw) * kk_scale).astype(BF16), (r * jnp.exp(cum)).astype(BF16)], axis=0),
        beta=(b * w_inv).astype(BF16),
        kappa=(kd * w_inv).astype(BF16),
        tail=jnp.concatenate([(kd * w_tail).astype(BF16), (-b * w_tail).astype(BF16)], axis=0),
        w_end=jnp.exp(cum_end),
        v=v.astype(BF16))


def _rwkv_chunk_update(ops, s_ref, y_refs):
    c = RWKV_CHUNK
    gw = HEADS_PER_GROUP * HEAD_DIM
    n_groups = RWKV_HEADS // HEADS_PER_GROUP
    chains = [(d, g) for g in range(n_groups) for d in range(2)]
    tp = lax.broadcasted_iota(jnp.int32, (c, HEADS_PER_GROUP * c), 0)
    sp = lax.broadcasted_iota(jnp.int32, (c, HEADS_PER_GROUP * c), 1) % c
    before = [sp < tp, sp > tp]
    upto = [sp <= tp, sp >= tp]
    eye = (sp == tp).astype(F32)
    bi = lax.broadcasted_iota(jnp.int32, (gw, gw), 0) // HEAD_DIM
    bj = lax.broadcasted_iota(jnp.int32, (gw, gw), 1) // HEAD_DIM
    diag = bi == bj
    cols = lambda g: slice(g * gw, (g + 1) * gw)

    gram_b, gram_k, sx, s0 = {}, {}, {}, {}
    for ch in chains:
        d, g = ch
        x = ops[d]["x"][:, cols(g)]
        gram_b[ch] = _dot_nt(x, _block_diag(ops[d]["beta"][:, cols(g)], diag))
        gram_k[ch] = _dot_nt(x, _block_diag(ops[d]["kappa"][:, cols(g)], diag))
        s0[ch] = s_ref[d, g]
        sx[ch] = _dot_nt(x, s0[ch].astype(BF16))

    npow = {ch: jnp.where(before[ch[0]], -gram_b[ch][:c], 0.0) for ch in chains}
    tinv = {ch: eye + npow[ch] for ch in chains}
    nb = {ch: npow[ch].astype(BF16) for ch in chains}
    nbd = {ch: _block_diag(nb[ch], diag) for ch in chains}
    for _ in range(5):
        for ch in chains:
            nb[ch] = _dot(nb[ch], nbd[ch]).astype(BF16)
            nbd[ch] = _block_diag(nb[ch], diag)
        for ch in chains:
            tinv[ch] = tinv[ch] + _dot(tinv[ch].astype(BF16), nbd[ch])

    kv = {}
    for ch in chains:
        d, g = ch
        m2 = jnp.concatenate([jnp.where(before[d], gram_k[ch][:c], 0.0), jnp.where(upto[d], gram_k[ch][c:], 0.0)],
                             axis=0)
        kv[ch] = _dot(m2.astype(BF16), _block_diag(ops[d]["v"][:, cols(g)], diag))
    ub = {}
    for ch in chains:
        rhs = sx[ch][:c] + kv[ch][:c]
        ub[ch] = _dot(tinv[ch].astype(BF16), _block_diag(rhs.astype(BF16), diag)).astype(BF16)
    for ch in chains:
        d, g = ch
        rb = jnp.where(upto[d], gram_b[ch][c:], 0.0).astype(BF16)
        y_refs[d][:, cols(g)] = sx[ch][c:] + kv[ch][c:] - _dot(rb, _block_diag(ub[ch], diag))
    for ch in chains:
        d, g = ch
        upd = _dot_tn(jnp.concatenate([ops[d]["v"][:, cols(g)], ub[ch]], axis=0), ops[d]["tail"][:, cols(g)])
        s_ref[d, g] = s0[ch] * ops[d]["w_end"][:, cols(g)] + jnp.where(diag, upd, 0.0)


def _rwkv_scan_kernel(rf_ref, kf_ref, vf_ref, lf_ref, rb_ref, kb_ref, vb_ref, lb_ref,
                      dbase_ref, dup_ref, ibase_ref, iup_ref, kk_ref, ka_ref, yf_ref, yb_ref, s_ref):
    @pl.when(pl.program_id(0) == 0)
    def _():
        s_ref[...] = jnp.zeros_like(s_ref)

    k_k = kk_ref[...]
    k_a = ka_ref[...]
    ops = []
    for d, (r_ref, k_ref, v_ref, l_ref) in enumerate(
            ((rf_ref, kf_ref, vf_ref, lf_ref), (rb_ref, kb_ref, vb_ref, lb_ref))):
        k = k_ref[...]
        kk = k * k_k
        kk = _per_head(kk, lambda x: x / jnp.maximum(jnp.sqrt(jnp.sum(x * x, axis=-1, keepdims=True)), 1e-12))
        ops.append(_rwkv_chunk_operands(d, r_ref[...], k, v_ref[...], l_ref[...], dbase_ref, dup_ref, ibase_ref,
                                        iup_ref, kk, k_a))
    _rwkv_chunk_update(ops, s_ref, (yf_ref, yb_ref))


def _rwkv_scan(p, decay_base, decay_up, iclr_base, iclr_up, k_k, k_a, n_ctx):
    n_rows = p.shape[0]
    c = RWKV_CHUNK
    n_chunks = n_rows // c
    ctx_chunks = n_ctx // c

    def fwd(g):
        return g

    def bwd(g):
        return jnp.where(g < ctx_chunks, ctx_chunks - 1 - g, ctx_chunks + n_chunks - 1 - g)

    def specs(order):
        return [pl.BlockSpec((c, RWKV_W), lambda g: (order(g), COL_R // RWKV_W)),
                pl.BlockSpec((c, RWKV_W), lambda g: (order(g), COL_RK // RWKV_W)),
                pl.BlockSpec((c, RWKV_W), lambda g: (order(g), COL_RV // RWKV_W)),
                pl.BlockSpec((c, LR_W), lambda g: (order(g), COL_LR // LR_W))]

    full = lambda shape: pl.BlockSpec(shape, lambda g: (0,) * len(shape))
    return pl.pallas_call(
        _rwkv_scan_kernel,
        grid=(n_chunks,),
        in_specs=specs(fwd) + specs(bwd) + [
            full((2, RWKV_W)), full((2, DECAY_RANK, RWKV_W)), full((2, RWKV_W)), full((2, ICLR_RANK, RWKV_W)),
            full((1, RWKV_W)), full((1, RWKV_W))],
        out_specs=[pl.BlockSpec((c, RWKV_W), lambda g: (fwd(g), 0)),
                   pl.BlockSpec((c, RWKV_W), lambda g: (bwd(g), 0))],
        out_shape=[jax.ShapeDtypeStruct((n_rows, RWKV_W), F32)] * 2,
        scratch_shapes=[pltpu.VMEM((2, RWKV_HEADS // HEADS_PER_GROUP, HEADS_PER_GROUP * HEAD_DIM,
                                   HEADS_PER_GROUP * HEAD_DIM), F32)],
        compiler_params=_params(1),
        name="rwkv7_chunk_scan",
    )(p, p, p, p, p, p, p, p, decay_base, decay_up, iclr_base, iclr_up,
      k_k.reshape(1, RWKV_W), k_a.reshape(1, RWKV_W))


def _head_sums(x):
    gw = HEADS_PER_GROUP * HEAD_DIM
    bi = lax.broadcasted_iota(jnp.int32, (gw, gw), 0) // HEAD_DIM
    bj = lax.broadcasted_iota(jnp.int32, (gw, gw), 1) // HEAD_DIM
    ones = (bi == bj).astype(BF16)
    hi = x.astype(BF16)
    lo = (x - hi.astype(F32)).astype(BF16)
    parts = []
    for g in range(x.shape[1] // gw):
        gs = slice(g * gw, (g + 1) * gw)
        parts.append(_dot(hi[:, gs], ones) + _dot(lo[:, gs], ones))
    return jnp.concatenate(parts, axis=1)


def _rwkv_out_kernel(yf_ref, yb_ref, r_ref, k_ref, v_ref, lr_ref, ibase_ref, iup_ref, ka_ref, rk_ref,
                     lw_ref, lb_ref, gup_ref, o_ref):
    y = yf_ref[...] + yb_ref[...]
    r = r_ref[...]
    k = k_ref[...]
    v = v_ref[...]
    lr = lr_ref[...]
    k_a = ka_ref[...]

    yc = y - _head_sums(y) * (1.0 / HEAD_DIM)
    var = _head_sums(yc * yc) * (1.0 / HEAD_DIM)
    yn = yc * lax.rsqrt(var + RWKV_GN_EPS) * lw_ref[...] + lb_ref[...]
    kd_sum = jnp.zeros_like(k)
    for d in range(2):
        ad = lr[:, 2 * DECAY_RANK + d * ICLR_RANK:2 * DECAY_RANK + (d + 1) * ICLR_RANK]
        a = _iclr(ad, ibase_ref[d:d + 1, :], iup_ref[d])
        kd_sum = kd_sum + k * (1.0 + (a - 1.0) * k_a)
    rkk = r * kd_sum * rk_ref[...]
    bonus = _head_sums(rkk) * v
    gd = lr[:, 4 * DECAY_RANK:4 * DECAY_RANK + GATE_RANK]
    gate = _dot(jax.nn.sigmoid(gd).astype(BF16), gup_ref[...].astype(BF16))
    o_ref[...] = ((yn + bonus) * gate).astype(o_ref.dtype)


def _rwkv_out(yf, yb, p, iclr_base, iclr_up, k_a, r_k, lnx_w, lnx_b, gate_up):
    n_rows = p.shape[0]
    tm = 256
    row = lambda w, cb: pl.BlockSpec((tm, w), lambda i, cb=cb: (i, cb))
    full = lambda shape: pl.BlockSpec(shape, lambda i: (0,) * len(shape))
    vec = lambda t: t.reshape(1, RWKV_W)
    return pl.pallas_call(
        _rwkv_out_kernel,
        grid=(n_rows // tm,),
        in_specs=[row(RWKV_W, 0), row(RWKV_W, 0), row(RWKV_W, COL_R // RWKV_W), row(RWKV_W, COL_RK // RWKV_W),
                  row(RWKV_W, COL_RV // RWKV_W), row(LR_W, COL_LR // LR_W),
                  full((2, RWKV_W)), full((2, ICLR_RANK, RWKV_W)), full((1, RWKV_W)), full((1, RWKV_W)),
                  full((1, RWKV_W)), full((1, RWKV_W)), full((GATE_RANK, RWKV_W))],
        out_specs=pl.BlockSpec((tm, RWKV_W), lambda i: (i, 0)),
        out_shape=jax.ShapeDtypeStruct((n_rows, RWKV_W), BF16),
        compiler_params=_params(1),
        name="rwkv7_out",
    )(yf, yb, p, p, p, p, iclr_base, iclr_up, vec(k_a), vec(r_k), vec(lnx_w), vec(lnx_b), gate_up)


def _out_proj_kernel(a_ref, b_ref, c_ref, wa_ref, wb_ref, wc_ref, x_ref, gate_ref, o_ref, *, n_ctx, tm):
    acc = _dot(a_ref[...], wa_ref[...]) + _dot(b_ref[...], wb_ref[...]) + _dot(c_ref[...], wc_ref[...])
    row = pl.program_id(0) * tm + lax.broadcasted_iota(jnp.int32, (tm, 1), 0)
    gate = jnp.where(row < n_ctx, gate_ref[1:2, :], gate_ref[0:1, :])
    o_ref[...] = x_ref[...] + gate * acc


def _out_proj(a, b, c, w_out_bf16, xs, gate, n_ctx):
    n_rows, d = xs.shape
    tm = _row_tile(n_rows, (1280, 1024, 512, 256))
    tn = 512
    wa, wb, wc = w_out_bf16[:CONV_W], w_out_bf16[CONV_W:CONV_W + ATT_W], w_out_bf16[CONV_W + ATT_W:]
    return pl.pallas_call(
        functools.partial(_out_proj_kernel, n_ctx=n_ctx, tm=tm),
        grid=(n_rows // tm, d // tn),
        in_specs=[pl.BlockSpec((tm, CONV_W), lambda i, j: (i, 0)),
                  pl.BlockSpec((tm, ATT_W), lambda i, j: (i, 0)),
                  pl.BlockSpec((tm, RWKV_W), lambda i, j: (i, 0)),
                  pl.BlockSpec((CONV_W, tn), lambda i, j: (0, j)),
                  pl.BlockSpec((ATT_W, tn), lambda i, j: (0, j)),
                  pl.BlockSpec((RWKV_W, tn), lambda i, j: (0, j)),
                  pl.BlockSpec((tm, tn), lambda i, j: (i, j)),
                  pl.BlockSpec((8, tn), lambda i, j: (0, j))],
        out_specs=pl.BlockSpec((tm, tn), lambda i, j: (i, j)),
        out_shape=jax.ShapeDtypeStruct((n_rows, d), F32),
        compiler_params=_params(2),
        name="out_proj_residual",
    )(a, b, c, wa, wb, wc, xs, gate)


def _first_max(x, idx):
    m = jnp.max(x, axis=0, keepdims=True)
    first = jnp.min(jnp.where(x == m, idx, N_EXPERTS), axis=0, keepdims=True)
    return m, first


def _router_kernel(x_ref, g_ref, mod_ref, rw_ref, rb_ref, h_ref, idx_ref, gate_ref, rank_ref, cnt_ref, base_ref,
                   *, n_ctx, tm):
    @pl.when(pl.program_id(0) == 0)
    def _():
        base_ref[...] = jnp.zeros_like(base_ref)

    h = _norm_mod(x_ref[...], g_ref[...], mod_ref[...], pl.program_id(0) * tm, n_ctx)
    h_ref[...] = h
    logits = _dot_nt(rw_ref[...], h, HIGHEST)
    scores = jax.nn.sigmoid(logits)
    sel = scores + rb_ref[...]
    eidx = lax.broadcasted_iota(jnp.int32, sel.shape, 0)
    best = best_score = None
    for g in range(N_EXPERT_GROUPS):
        rows = slice(g * EXPERTS_PER_GROUP, (g + 1) * EXPERTS_PER_GROUP)
        x = sel[rows]
        xi = g * EXPERTS_PER_GROUP + lax.broadcasted_iota(jnp.int32, x.shape, 0)
        m1, i1 = _first_max(x, xi)
        m2, _ = _first_max(jnp.where(xi == i1, -jnp.inf, x), xi)
        score = m1 + m2
        if g == 0:
            best, best_score = jnp.zeros_like(i1), score
        else:
            better = score > best_score
            best = jnp.where(better, g, best)
            best_score = jnp.where(better, score, best_score)
    masked = jnp.where(eidx // EXPERTS_PER_GROUP == best, sel, NEG_INF)
    _, e1 = _first_max(masked, eidx)
    _, e2 = _first_max(jnp.where(eidx == e1, -jnp.inf, masked), eidx)
    g1 = jnp.sum(jnp.where(eidx == e1, scores, 0.0), axis=0, keepdims=True)
    g2 = jnp.sum(jnp.where(eidx == e2, scores, 0.0), axis=0, keepdims=True)
    idx_ref[0:1, :] = e1
    idx_ref[1:2, :] = e2
    gate_ref[0:1, :] = g1 / (g1 + g2)
    gate_ref[1:2, :] = g2 / (g1 + g2)

    si = lax.broadcasted_iota(jnp.int32, (tm, tm), 0)
    ti = lax.broadcasted_iota(jnp.int32, (tm, tm), 1)
    prefix = (si <= ti).astype(BF16)
    base = base_ref[...]
    for kth, e in enumerate((e1, e2)):
        hit = eidx == e
        seen = _dot(hit.astype(BF16), prefix)
        rank = jnp.sum(jnp.where(hit, seen - 1.0 + base, 0.0), axis=0, keepdims=True)
        rank_ref[kth:kth + 1, :] = rank.astype(jnp.int32)
        base = base + seen[:, tm - 1:tm]
    base_ref[...] = base
    cnt_ref[...] = base.astype(jnp.int32)


def _route(xs, g, mod, router_w, router_b, n_ctx):
    n_rows, d = xs.shape
    tm = 256
    return pl.pallas_call(
        functools.partial(_router_kernel, n_ctx=n_ctx, tm=tm),
        grid=(n_rows // tm,),
        in_specs=[pl.BlockSpec((tm, d), lambda i: (i, 0)),
                  pl.BlockSpec((1, d), lambda i: (0, 0)),
                  pl.BlockSpec((8, d), lambda i: (0, 0)),
                  pl.BlockSpec((N_EXPERTS, d), lambda i: (0, 0)),
                  pl.BlockSpec((N_EXPERTS, 1), lambda i: (0, 0))],
        out_specs=[pl.BlockSpec((tm, d), lambda i: (i, 0)),
                   pl.BlockSpec((TOP_K, tm), lambda i: (0, i)),
                   pl.BlockSpec((TOP_K, tm), lambda i: (0, i)),
                   pl.BlockSpec((TOP_K, tm), lambda i: (0, i)),
                   pl.BlockSpec((N_EXPERTS, 1), lambda i: (0, 0))],
        out_shape=[jax.ShapeDtypeStruct((n_rows, d), F32),
                   jax.ShapeDtypeStruct((TOP_K, n_rows), jnp.int32),
                   jax.ShapeDtypeStruct((TOP_K, n_rows), F32),
                   jax.ShapeDtypeStruct((TOP_K, n_rows), jnp.int32),
                   jax.ShapeDtypeStruct((N_EXPERTS, 1), jnp.int32)],
        scratch_shapes=[pltpu.VMEM((N_EXPERTS, 1), F32)],
        compiler_params=_params(1),
        name="moe_norm_route",
    )(xs, g.reshape(1, d), mod, router_w.T, router_b.reshape(N_EXPERTS, 1))


def _row_copy(src_ref, src_row, dst_ref, dst_row, sem):
    return pltpu.make_async_copy(src_ref.at[pl.ds(src_row, 1), :], dst_ref.at[pl.ds(dst_row, 1), :], sem)


def _dispatch_kernel(dest_ref, h_ref, init_ref, xg_ref, sem, *, tm):
    del init_ref

    def start(t, carry):
        for k in range(TOP_K):
            _row_copy(h_ref, t, xg_ref, dest_ref[0, k, t], sem).start()
        return carry

    def wait(t, carry):
        for k in range(TOP_K):
            _row_copy(h_ref, t, xg_ref, dest_ref[0, k, t], sem).wait()
        return carry

    lax.fori_loop(0, tm, start, 0)
    lax.fori_loop(0, tm, wait, 0)


def _dispatch(hp, dest_tiles, cap, tm):
    n_rows, w = hp.shape
    return pl.pallas_call(
        functools.partial(_dispatch_kernel, tm=tm),
        grid=(n_rows // tm,),
        in_specs=[pl.BlockSpec((1, TOP_K, tm), lambda i: (i, 0, 0), memory_space=pltpu.SMEM),
                  pl.BlockSpec((tm, w), lambda i: (i, 0)),
                  pl.BlockSpec(memory_space=pl.ANY)],
        out_specs=pl.BlockSpec(memory_space=pl.ANY),
        out_shape=jax.ShapeDtypeStruct((cap, w), hp.dtype),
        scratch_shapes=[pltpu.SemaphoreType.DMA(())],
        input_output_aliases={2: 0},
        compiler_params=_params(1),
        name="moe_dispatch_rows",
    )(dest_tiles, hp, jnp.zeros((cap, w), hp.dtype))


def _expert_kernel(be_ref, nused_ref, x_ref, wg_ref, wu_ref, wd_ref, o_ref, wgb_ref, wub_ref, wdb_ref):
    i = pl.program_id(0)
    used = i < nused_ref[0]
    new_expert = (i == 0) | (be_ref[i] != be_ref[jnp.maximum(i - 1, 0)])

    @pl.when(used & new_expert)
    def _():
        for r0 in range(0, D_MODEL, CAST_ROWS):
            rows = slice(r0, r0 + CAST_ROWS)
            wgb_ref[rows, :] = wg_ref[0, rows, :].astype(BF16)
            wub_ref[rows, :] = wu_ref[0, rows, :].astype(BF16)
        for r0 in range(0, D_EXPERT, CAST_ROWS):
            rows = slice(r0, r0 + CAST_ROWS)
            wdb_ref[rows, :] = wd_ref[0, rows, :].astype(BF16)

    @pl.when(used)
    def _():
        x = x_ref[...].astype(BF16)
        gate = _dot(x, wgb_ref[...])
        up = _dot(x, wub_ref[...])
        act = (gate * jax.nn.sigmoid(gate) * up).astype(BF16)
        o_ref[...] = _dot(act, wdb_ref[...])

    @pl.when(i >= nused_ref[0])
    def _():
        o_ref[...] = jnp.zeros_like(o_ref)


def _expert_ffn(xg, block_e, n_used, wg, wu, wd, layer, tm):
    cap, d = xg.shape
    grid_spec = pltpu.PrefetchScalarGridSpec(
        num_scalar_prefetch=2,
        grid=(cap // tm,),
        in_specs=[pl.BlockSpec((tm, d), lambda i, be, nu: (i, 0)),
                  pl.BlockSpec((None, 1, d, D_EXPERT), lambda i, be, nu: (layer, be[i], 0, 0)),
                  pl.BlockSpec((None, 1, d, D_EXPERT), lambda i, be, nu: (layer, be[i], 0, 0)),
                  pl.BlockSpec((None, 1, D_EXPERT, d), lambda i, be, nu: (layer, be[i], 0, 0))],
        out_specs=pl.BlockSpec((tm, d), lambda i, be, nu: (i, 0)),
        scratch_shapes=[pltpu.VMEM((d, D_EXPERT), BF16), pltpu.VMEM((d, D_EXPERT), BF16),
                        pltpu.VMEM((D_EXPERT, d), BF16)],
    )
    return pl.pallas_call(
        _expert_kernel,
        grid_spec=grid_spec,
        out_shape=jax.ShapeDtypeStruct((cap, d), F32),
        compiler_params=pltpu.CompilerParams(dimension_semantics=("arbitrary",),
                                             vmem_limit_bytes=EXPERT_VMEM_LIMIT),
        name="moe_expert_ffn",
    )(block_e, n_used, xg, wg, wu, wd)


def _combine_kernel(dest_ref, x_ref, gt_ref, g2_ref, fg_ref, yb_ref, o_ref, ybuf_ref, sem,
                    *, n_ctx, tm, final_norm):
    def start(t, carry):
        for k in range(TOP_K):
            _row_copy(yb_ref, dest_ref[0, k, t], ybuf_ref.at[k], t, sem).start()
        return carry

    def wait(t, carry):
        for k in range(TOP_K):
            _row_copy(yb_ref, dest_ref[0, k, t], ybuf_ref.at[k], t, sem).wait()
        return carry

    lax.fori_loop(0, tm, start, 0)
    lax.fori_loop(0, tm, wait, 0)

    row = pl.program_id(0) * tm + lax.broadcasted_iota(jnp.int32, (tm, 1), 0)
    g2 = jnp.where(row < n_ctx, g2_ref[1:2, :], g2_ref[0:1, :])
    gt = gt_ref[...]
    y = gt[:, 0:1] * ybuf_ref[0] + gt[:, 1:2] * ybuf_ref[1]
    x = x_ref[...] + g2 * y
    if final_norm:
        ms = jnp.mean(x * x, axis=-1, keepdims=True)
        x = x * lax.rsqrt(ms + RMS_EPS) * fg_ref[...]
    o_ref[...] = x


def _combine(xs, yb, dest_tiles, gates_t, g2, final_g, n_ctx, final_norm, tm):
    n_rows, d = xs.shape
    row = pl.BlockSpec((tm, d), lambda i: (i, 0))
    return pl.pallas_call(
        functools.partial(_combine_kernel, n_ctx=n_ctx, tm=tm, final_norm=final_norm),
        grid=(n_rows // tm,),
        in_specs=[pl.BlockSpec((1, TOP_K, tm), lambda i: (i, 0, 0), memory_space=pltpu.SMEM),
                  row, pl.BlockSpec((tm, TOP_K), lambda i: (i, 0)),
                  pl.BlockSpec((8, d), lambda i: (0, 0)), pl.BlockSpec((1, d), lambda i: (0, 0)),
                  pl.BlockSpec(memory_space=pl.ANY)],
        out_specs=row,
        out_shape=jax.ShapeDtypeStruct((n_rows, d), F32),
        scratch_shapes=[pltpu.VMEM((TOP_K, tm, yb.shape[1]), yb.dtype), pltpu.SemaphoreType.DMA(())],
        compiler_params=_params(1),
        name="moe_combine_residual",
    )(dest_tiles, xs, gates_t, g2, final_g.reshape(1, d), yb)


def _moe(xs, g, mod, g2, router_w, router_b, wg, wu, wd, layer, final_g, n_ctx, final_norm):
    n_rows, d = xs.shape
    h, idx, gates, rank, counts = _route(xs, g, mod, router_w, router_b, n_ctx)
    tm = 256
    n_asg = n_rows * TOP_K
    n_blk = n_asg // tm + N_EXPERTS
    cap = n_blk * tm
    counts = counts[:, 0]
    padded = (counts + tm - 1) // tm * tm
    pends = jnp.cumsum(padded)
    pstarts = pends - padded
    dest = pstarts[idx] + rank
    dest_tiles = dest.reshape(TOP_K, n_rows // tm, tm).transpose(1, 0, 2).astype(jnp.int32)
    block_e = jnp.minimum(jnp.sum(pends[None, :] <= (jnp.arange(n_blk) * tm)[:, None], axis=1), N_EXPERTS - 1)
    n_used = (pends[-1] // tm).reshape(1)
    xg = _dispatch(h, dest_tiles, cap, tm)
    yb = _expert_ffn(xg, block_e.astype(jnp.int32), n_used.astype(jnp.int32), wg, wu, wd, layer, tm)
    return _combine(xs, yb, dest_tiles, gates.T, g2, final_g, n_ctx, final_norm, tm)


def _mod_rows(mod_l, lat_chunks, ctx_chunks):
    d = D_MODEL
    rows = [mod_l[0, c * d:(c + 1) * d] for c in lat_chunks] + [mod_l[1, c * d:(c + 1) * d] for c in ctx_chunks]
    out = jnp.zeros((8, d), F32)
    return out.at[:len(rows)].set(jnp.stack(rows))


def kernel(x, c, ctx, c_ctx, ada_w, ada_b, norm1_g, norm2_g, w_in, w_out, conv_w, attn_sinks, decay_base,
           decay_up, iclr_base, iclr_up, gate_up, k_k, k_a, r_k, lnx_w, lnx_b, router_w, router_b,
           expert_gate, expert_up, expert_down, final_norm_g):
    bsz, n_lat, d = x.shape
    n_ctx = ctx.shape[1]
    depth = ada_w.shape[0]
    assert bsz == 1 and d == D_MODEL and n_ctx % NORM_ROWS == 0 and n_lat % NORM_ROWS == 0

    xs = jnp.concatenate([ctx[0], x[0]], axis=0)
    mods = _ada_mod(c, c_ctx, ada_w, ada_b)
    cos, sin = _rope_tables(n_ctx, n_lat)
    perm = _in_proj_perm()

    for l in range(depth):
        last = l == depth - 1
        mod1 = _mod_rows(mods[l], (0, 1), (0, 1))
        gate1 = _mod_rows(mods[l], (2,), (2,))
        mod2 = _mod_rows(mods[l], (3, 4), (3, 4))
        gate2 = _mod_rows(mods[l], (5,), (5,))

        w_in_l = jnp.pad(w_in[l][:, perm], ((0, 0), (0, IN_W_PAD - perm.size))).astype(BF16)
        p = _norm_mod_matmul(xs, norm1_g[l], mod1, w_in_l, n_ctx)

        a_mix = _short_conv(p, conv_w[l], n_ctx)
        qr, kr, vb = _rope_qkv(p, cos, sin)
        b_mix = _window_attention(qr, kr, vb, attn_sinks[l], n_ctx, n_lat)
        yf, yb = _rwkv_scan(p, decay_base[l], decay_up[l], iclr_base[l], iclr_up[l], k_k[l], k_a[l], n_ctx)
        c_mix = _rwkv_out(yf, yb, p, iclr_base[l], iclr_up[l], k_a[l], r_k[l].reshape(-1), lnx_w[l], lnx_b[l],
                          gate_up[l])
        xs = _out_proj(a_mix, b_mix, c_mix, w_out[l].astype(BF16), xs, gate1, n_ctx)

        xs = _moe(xs, norm2_g[l], mod2, gate2, router_w, router_b, expert_gate, expert_up, expert_down, l,
                  final_norm_g, n_ctx, last)
    return xs[n_ctx:].reshape(bsz, n_lat, d)
```

```python
import functools

import numpy as np
import jax
import jax.numpy as jnp
from jax import lax
from jax.experimental import pallas as pl
from jax.experimental.pallas import tpu as pltpu

F32 = jnp.float32
BF16 = jnp.bfloat16
HIGHEST = lax.Precision.HIGHEST

D_MODEL = 2048
GRID_W = 64
CONV_W = D_MODEL // 4
CONV_K = 3
HEAD_DIM = 64
ATT_HEADS = 12
ATT_KV_HEADS = 4
ATT_GROUP = ATT_HEADS // ATT_KV_HEADS
ATT_W = ATT_HEADS * HEAD_DIM
ATT_KV_W = ATT_KV_HEADS * HEAD_DIM
RWKV_HEADS = 12
RWKV_W = RWKV_HEADS * HEAD_DIM
WINDOW = 128
ATT_BLOCK = 128
ROPE_THETA = 10000.0
ROPE_FREQS = HEAD_DIM // 4
DECAY_RANK = 64
ICLR_RANK = 64
GATE_RANK = 128
RWKV_GN_EPS = 64e-5
N_EXPERTS = 32
N_EXPERT_GROUPS = 4
EXPERTS_PER_GROUP = N_EXPERTS // N_EXPERT_GROUPS
TOP_K = 2
D_EXPERT = 768
RMS_EPS = 1e-6
NEG_INF = -1e30

COL_Q = 0
COL_R = 768
COL_RK = 1536
COL_RV = 2304
COL_CB = 3072
COL_CC = 3584
COL_CH = 4096
COL_K = 4608
COL_V = 4864
COL_LR = 5120
LR_W = 512
IN_W_PAD = 5632

RWKV_CHUNK = 64
HEADS_PER_GROUP = 4
NORM_ROWS = 256
CAST_ROWS = 256
DMA_UNROLL = 8
VMEM_LIMIT = 56 * 1024 * 1024
EXPERT_VMEM_LIMIT = 62 * 1024 * 1024


def _params(n_axes):
    return pltpu.CompilerParams(dimension_semantics=("arbitrary",) * n_axes,
                                vmem_limit_bytes=VMEM_LIMIT)


def _row_tile(n_rows, candidates):
    for t in candidates:
        if n_rows % t == 0:
            return t
    raise ValueError(f"no row tile for {n_rows}")


def _in_proj_weights(w):
    o_cb, o_cc, o_ch = 0, 512, 1024
    o_q, o_k, o_v = 1536, 2304, 2560
    o_r, o_rk, o_rv = 2816, 3584, 4352
    o_lr = 5120
    segs = [(o_q, 768), (o_r, 768), (o_rk, 768), (o_rv, 768), (o_cb, 512), (o_cc, 512), (o_ch, 512),
            (o_k, 256), (o_v, 256), (o_lr, 384)]
    parts = [w[:, o:o + n].astype(BF16) for o, n in segs]
    parts.append(jnp.zeros((w.shape[0], IN_W_PAD - sum(n for _, n in segs)), BF16))
    return jnp.concatenate(parts, axis=1)


def _ada_kernel(s_ref, w_ref, b_ref, o_ref):
    o_ref[0] = jnp.dot(s_ref[...], w_ref[0], precision=HIGHEST, preferred_element_type=F32) + b_ref[0]


def _ada_mod(c, c_ctx, ada_w, ada_b):
    depth, d, n = ada_w.shape
    s = jnp.zeros((8, d), F32).at[0].set(c[0]).at[1].set(c_ctx)
    s = s * jax.nn.sigmoid(s)
    tn = 1024
    return pl.pallas_call(
        _ada_kernel,
        grid=(depth, n // tn),
        in_specs=[pl.BlockSpec((8, d), lambda l, j: (0, 0)),
                  pl.BlockSpec((1, d, tn), lambda l, j: (l, 0, j)),
                  pl.BlockSpec((1, 1, tn), lambda l, j: (l, 0, j))],
        out_specs=pl.BlockSpec((1, 8, tn), lambda l, j: (l, 0, j)),
        out_shape=jax.ShapeDtypeStruct((depth, 8, n), F32),
        compiler_params=_params(2),
        name="ada_mod",
    )(s, ada_w, ada_b.reshape(depth, 1, n))


def _norm_mod(x, g, mod, row0, n_ctx):
    ms = jnp.mean(x * x, axis=-1, keepdims=True)
    y = x * lax.rsqrt(ms + RMS_EPS) * g
    is_ctx = row0 < n_ctx
    shift = jnp.where(is_ctx, mod[2:3, :], mod[0:1, :])
    scale = jnp.where(is_ctx, mod[3:4, :], mod[1:2, :])
    return y * (1.0 + scale) + shift


def _nmm_kernel(x_ref, g_ref, mod_ref, w_ref, o_ref, h_ref, *, n_ctx, tm):
    i = pl.program_id(0)

    @pl.when(pl.program_id(1) == 0)
    def _():
        for r0 in range(0, tm, NORM_ROWS):
            rows = slice(r0, r0 + NORM_ROWS)
            h_ref[rows, :] = _norm_mod(x_ref[rows, :], g_ref[...], mod_ref[...], i * tm + r0, n_ctx).astype(BF16)

    o_ref[...] = jnp.dot(h_ref[...], w_ref[...], preferred_element_type=F32)


def _norm_mod_matmul(xs, g, mod, w_bf16, n_ctx):
    n_rows, d = xs.shape
    n_out = w_bf16.shape[1]
    tm = _row_tile(n_rows, (1280, 1024, 512, 256))
    tn = 512
    return pl.pallas_call(
        functools.partial(_nmm_kernel, n_ctx=n_ctx, tm=tm),
        grid=(n_rows // tm, n_out // tn),
        in_specs=[pl.BlockSpec((tm, d), lambda i, j: (i, 0)),
                  pl.BlockSpec((1, d), lambda i, j: (0, 0)),
                  pl.BlockSpec((8, d), lambda i, j: (0, 0)),
                  pl.BlockSpec((d, tn), lambda i, j: (0, j))],
        out_specs=pl.BlockSpec((tm, tn), lambda i, j: (i, j)),
        out_shape=jax.ShapeDtypeStruct((n_rows, n_out), F32),
        scratch_shapes=[pltpu.VMEM((tm, d), BF16)],
        compiler_params=_params(2),
        name="norm_mod_in_proj",
    )(xs, g.reshape(1, d), mod, w_bf16)


def _conv_kernel(cb_ref, cc_ref, ch_ref, ccp_ref, chp_ref, ccn_ref, chn_ref, w_ref, o_ref, *, n_ctx, n_rows, tm):
    i = pl.program_id(0)
    u = cc_ref[...] * ch_ref[...]
    u_prev_row = ccp_ref[7:8, :] * chp_ref[7:8, :]
    u_next_row = ccn_ref[0:1, :] * chn_ref[0:1, :]
    loc = lax.broadcasted_iota(jnp.int32, (tm, 1), 0)
    row = i * tm + loc
    up = jnp.where(loc == 0, u_prev_row, pltpu.roll(u, 1, axis=0))
    dn = jnp.where(loc == tm - 1, u_next_row, pltpu.roll(u, tm - 1, axis=0))
    up = jnp.where((row == 0) | (row == n_ctx), 0.0, up)
    dn = jnp.where((row == n_ctx - 1) | (row == n_rows - 1), 0.0, dn)
    w = w_ref[...]
    y = w[0:1, :] * up + w[1:2, :] * u + w[2:3, :] * dn
    o_ref[...] = (cb_ref[...] * y).astype(o_ref.dtype)


def _short_conv(p, conv_w, n_ctx):
    n_rows = p.shape[0]
    tm = _row_tile(n_rows, (1280, 1024, 512, 256))
    r8 = tm // 8
    last8 = n_rows // 8 - 1
    wpad = jnp.zeros((8, CONV_W), F32).at[:CONV_K].set(conv_w)
    blk = lambda c: pl.BlockSpec((tm, CONV_W), lambda i, c=c: (i, c))
    prev = lambda c: pl.BlockSpec((8, CONV_W), lambda i, c=c: (jnp.maximum(i * r8 - 1, 0), c))
    nxt = lambda c: pl.BlockSpec((8, CONV_W), lambda i, c=c: (jnp.minimum((i + 1) * r8, last8), c))
    cb, cc, ch = COL_CB // CONV_W, COL_CC // CONV_W, COL_CH // CONV_W
    return pl.pallas_call(
        functools.partial(_conv_kernel, n_ctx=n_ctx, n_rows=n_rows, tm=tm),
        grid=(n_rows // tm,),
        in_specs=[blk(cb), blk(cc), blk(ch), prev(cc), prev(ch), nxt(cc), nxt(ch),
                  pl.BlockSpec((8, CONV_W), lambda i: (0, 0))],
        out_specs=pl.BlockSpec((tm, CONV_W), lambda i: (i, 0)),
        out_shape=jax.ShapeDtypeStruct((n_rows, CONV_W), BF16),
        compiler_params=_params(1),
        name="short_conv",
    )(p, p, p, p, p, p, p, wpad)


def _swap_halves(x):
    n = x.shape[1]
    lane = lax.broadcasted_iota(jnp.int32, x.shape, 1)
    fwd = pltpu.roll(x, n - ROPE_FREQS, axis=1)
    bwd = pltpu.roll(x, ROPE_FREQS, axis=1)
    return jnp.where((lane % (2 * ROPE_FREQS)) < ROPE_FREQS, fwd, bwd)


def _rope_kernel(q_ref, k_ref, v_ref, cos_ref, sin_ref, qo_ref, ko_ref, vo_ref):
    cos = cos_ref[...]
    sin = sin_ref[...]
    q = q_ref[...]
    k = k_ref[...]
    cos_q = jnp.concatenate([cos] * (ATT_W // 128), axis=1)
    sin_q = jnp.concatenate([sin] * (ATT_W // 128), axis=1)
    cos_k = jnp.concatenate([cos] * (ATT_KV_W // 128), axis=1)
    sin_k = jnp.concatenate([sin] * (ATT_KV_W // 128), axis=1)
    qo_ref[...] = ((q * cos_q + _swap_halves(q) * sin_q) * (HEAD_DIM ** -0.5)).astype(BF16)
    ko_ref[...] = (k * cos_k + _swap_halves(k) * sin_k).astype(BF16)
    vo_ref[...] = v_ref[...].astype(BF16)


def _rope_tables(n_ctx, n_lat):
    row = jnp.repeat(jnp.arange(n_lat // GRID_W, dtype=jnp.int32), GRID_W)
    col = jnp.arange(n_lat, dtype=jnp.int32) % GRID_W
    inv = ROPE_THETA ** (-jnp.arange(ROPE_FREQS, dtype=F32) / ROPE_FREQS)
    ang_r = row[:, None].astype(F32) * inv[None, :]
    ang_c = col[:, None].astype(F32) * inv[None, :]
    cr, sr, cc, sc = jnp.cos(ang_r), jnp.sin(ang_r), jnp.cos(ang_c), jnp.sin(ang_c)
    cos = jnp.concatenate([cr, cr, cc, cc], axis=1)
    sin = jnp.concatenate([-sr, sr, -sc, sc], axis=1)
    cos = jnp.concatenate([jnp.ones((n_ctx, HEAD_DIM), F32), cos], axis=0)
    sin = jnp.concatenate([jnp.zeros((n_ctx, HEAD_DIM), F32), sin], axis=0)
    return jnp.tile(cos, (1, 2)), jnp.tile(sin, (1, 2))


def _rope_qkv(p, cos, sin):
    n_rows = p.shape[0]
    tm = _row_tile(n_rows, (1280, 1024, 512, 256))
    return pl.pallas_call(
        _rope_kernel,
        grid=(n_rows // tm,),
        in_specs=[pl.BlockSpec((tm, ATT_W), lambda i: (i, COL_Q // ATT_W)),
                  pl.BlockSpec((tm, ATT_KV_W), lambda i: (i, COL_K // ATT_KV_W)),
                  pl.BlockSpec((tm, ATT_KV_W), lambda i: (i, COL_V // ATT_KV_W)),
                  pl.BlockSpec((tm, 128), lambda i: (i, 0)),
                  pl.BlockSpec((tm, 128), lambda i: (i, 0))],
        out_specs=[pl.BlockSpec((tm, ATT_W), lambda i: (i, 0)),
                   pl.BlockSpec((tm, ATT_KV_W), lambda i: (i, 0)),
                   pl.BlockSpec((tm, ATT_KV_W), lambda i: (i, 0))],
        out_shape=[jax.ShapeDtypeStruct((n_rows, ATT_W), BF16),
                   jax.ShapeDtypeStruct((n_rows, ATT_KV_W), BF16),
                   jax.ShapeDtypeStruct((n_rows, ATT_KV_W), BF16)],
        compiler_params=_params(1),
        name="rope_qkv",
    )(p, p, p, cos, sin)


def _attn_kernel(sink_ref, q_ref, kp_ref, kc_ref, kn_ref, vp_ref, vc_ref, vn_ref, kx_ref, vx_ref, o_ref,
                 *, n_ctx, n_lat):
    i = pl.program_id(0)
    blk = ATT_BLOCK
    ctx_blocks = n_ctx // blk
    is_lat = i >= ctx_blocks
    base = (i - ctx_blocks) * blk
    qi = lax.broadcasted_iota(jnp.int32, (blk, n_ctx + 3 * blk), 0)
    ci = lax.broadcasted_iota(jnp.int32, (blk, n_ctx + 3 * blk), 1)
    rel = ci - n_ctx - blk
    kpos = base + rel
    band = (jnp.abs(rel - qi) <= WINDOW) & (kpos >= 0) & (kpos < n_lat) & is_lat
    valid = (ci < n_ctx) | band
    valid = jnp.concatenate([valid] * ATT_GROUP, axis=0)

    q = q_ref[...]
    outs = [None] * ATT_HEADS
    for g in range(ATT_KV_HEADS):
        ks = slice(g * HEAD_DIM, (g + 1) * HEAD_DIM)
        kk = jnp.concatenate([kx_ref[:, ks], kp_ref[:, ks], kc_ref[:, ks], kn_ref[:, ks]], axis=0)
        vv = jnp.concatenate([vx_ref[:, ks], vp_ref[:, ks], vc_ref[:, ks], vn_ref[:, ks]], axis=0)
        heads = [g * ATT_GROUP + j for j in range(ATT_GROUP)]
        qq = jnp.concatenate([q[:, h * HEAD_DIM:(h + 1) * HEAD_DIM] for h in heads], axis=0)
        s = lax.dot_general(qq, kk, (((1,), (1,)), ((), ())), preferred_element_type=F32)
        s = jnp.where(valid, s, NEG_INF)
        sink = jnp.concatenate([jnp.full((blk, 1), sink_ref[h], F32) for h in heads], axis=0)
        m = jnp.maximum(jnp.max(s, axis=-1, keepdims=True), sink)
        e = jnp.exp(s - m)
        den = jnp.sum(e, axis=-1, keepdims=True) + jnp.exp(sink - m)
        o = jnp.dot(e.astype(BF16), vv, preferred_element_type=F32) / den
        for j, h in enumerate(heads):
            outs[h] = o[j * blk:(j + 1) * blk, :]
    o_ref[...] = jnp.concatenate(outs, axis=1).astype(o_ref.dtype)


def _window_attention(qr, kr, vb, sinks, n_ctx, n_lat):
    n_rows = qr.shape[0]
    nb = n_rows // ATT_BLOCK
    q_spec = pl.BlockSpec((ATT_BLOCK, ATT_W), lambda i: (i, 0))
    prev = pl.BlockSpec((ATT_BLOCK, ATT_KV_W), lambda i: (jnp.maximum(i - 1, 0), 0))
    cur = pl.BlockSpec((ATT_BLOCK, ATT_KV_W), lambda i: (i, 0))
    nxt = pl.BlockSpec((ATT_BLOCK, ATT_KV_W), lambda i: (jnp.minimum(i + 1, nb - 1), 0))
    ctx = pl.BlockSpec((n_ctx, ATT_KV_W), lambda i: (0, 0))
    return pl.pallas_call(
        functools.partial(_attn_kernel, n_ctx=n_ctx, n_lat=n_lat),
        grid=(nb,),
        in_specs=[pl.BlockSpec(memory_space=pltpu.SMEM), q_spec, prev, cur, nxt, prev, cur, nxt, ctx, ctx],
        out_specs=pl.BlockSpec((ATT_BLOCK, ATT_W), lambda i: (i, 0)),
        out_shape=jax.ShapeDtypeStruct((n_rows, ATT_W), BF16),
        compiler_params=_params(1),
        name="window_attention",
    )(sinks.astype(F32), qr, kr, kr, kr, vb, vb, vb, kr, vb)


def _per_head(x, fn):
    return jnp.concatenate([fn(x[:, h * HEAD_DIM:(h + 1) * HEAD_DIM]) for h in range(RWKV_HEADS)], axis=1)


def _dot_nt(a, b, precision=None):
    return lax.dot_general(a, b, (((1,), (1,)), ((), ())), precision=precision, preferred_element_type=F32)


def _dot_tn(a, b, precision=None):
    return lax.dot_general(a, b, (((0,), (0,)), ((), ())), precision=precision, preferred_element_type=F32)


def _dot(a, b, precision=None):
    return jnp.dot(a, b, precision=precision, preferred_element_type=F32)


def _iclr(ad, ibase, iup):
    return jax.nn.sigmoid(ibase + _dot(ad.astype(BF16), iup.astype(BF16)))


def _block_diag(y, mask):
    return jnp.where(mask, jnp.concatenate([y] * HEADS_PER_GROUP, axis=0), jnp.zeros((), y.dtype))


def _rwkv_chunk_operands(d, r, k, v, lr, dbase_ref, dup_ref, ibase_ref, iup_ref, kk_scale, k_a):
    c = r.shape[0]
    wd = lr[:, d * DECAY_RANK:(d + 1) * DECAY_RANK]
    ad = lr[:, 2 * DECAY_RANK + d * ICLR_RANK:2 * DECAY_RANK + (d + 1) * ICLR_RANK]
    z = dbase_ref[d:d + 1, :] + _dot(jnp.tanh(wd).astype(BF16), dup_ref[d].astype(BF16))
    logw = -np.float32(np.exp(-0.5)) * jax.nn.sigmoid(z)
    a = _iclr(ad, ibase_ref[d:d + 1, :], iup_ref[d])
    kd = k * (1.0 + (a - 1.0) * k_a)
    b = kk_scale * a

    ti = lax.broadcasted_iota(jnp.int32, (c, c), 0)
    si = lax.broadcasted_iota(jnp.int32, (c, c), 1)
    tri = (si <= ti) if d == 0 else (si >= ti)
    cum = _dot(tri.astype(F32), logw, HIGHEST)
    last = c - 1 if d == 0 else 0
    cum_end = cum[last:last + 1, :]
    w_inv = jnp.exp(-cum)
    w_tail = jnp.exp(cum_end - cum)
    return dict(
        x=jnp.concatenate([(jnp.exp(cum - logw) * kk_scale).astype(BF16), (r * jnp.exp(cum)).astype(BF16)], axis=0),
        beta=(b * w_inv).astype(BF16),
        kappa=(kd * w_inv).astype(BF16),
        tail=jnp.concatenate([(kd * w_tail).astype(BF16), (-b * w_tail).astype(BF16)], axis=0),
        w_end=jnp.exp(cum_end),
        v=v.astype(BF16))


def _rwkv_chunk_update(ops, s_ref, y_refs, directions):
    c = RWKV_CHUNK
    gw = HEADS_PER_GROUP * HEAD_DIM
    n_groups = RWKV_HEADS // HEADS_PER_GROUP
    chains = [(d, g) for g in range(n_groups) for d in directions]
    tp = lax.broadcasted_iota(jnp.int32, (c, HEADS_PER_GROUP * c), 0)
    sp = lax.broadcasted_iota(jnp.int32, (c, HEADS_PER_GROUP * c), 1) % c
    before = [sp < tp, sp > tp]
    upto = [sp <= tp, sp >= tp]
    eye = (sp == tp).astype(F32)
    bi = lax.broadcasted_iota(jnp.int32, (gw, gw), 0) // HEAD_DIM
    bj = lax.broadcasted_iota(jnp.int32, (gw, gw), 1) // HEAD_DIM
    diag = bi == bj
    cols = lambda g: slice(g * gw, (g + 1) * gw)

    gram_b, gram_k, sx, s0 = {}, {}, {}, {}
    for ch in chains:
        d, g = ch
        x = ops[d]["x"][:, cols(g)]
        gram_b[ch] = _dot_nt(x, _block_diag(ops[d]["beta"][:, cols(g)], diag))
        gram_k[ch] = _dot_nt(x, _block_diag(ops[d]["kappa"][:, cols(g)], diag))
        s0[ch] = s_ref[d, g]
        sx[ch] = _dot_nt(x, s0[ch].astype(BF16))

    npow = {ch: jnp.where(before[ch[0]], -gram_b[ch][:c], 0.0) for ch in chains}
    tinv = {ch: eye + npow[ch] for ch in chains}
    nb = {ch: npow[ch].astype(BF16) for ch in chains}
    nbd = {ch: _block_diag(nb[ch], diag) for ch in chains}
    for _ in range(5):
        for ch in chains:
            nb[ch] = _dot(nb[ch], nbd[ch]).astype(BF16)
            nbd[ch] = _block_diag(nb[ch], diag)
        for ch in chains:
            tinv[ch] = tinv[ch] + _dot(tinv[ch].astype(BF16), nbd[ch])

    kv = {}
    for ch in chains:
        d, g = ch
        m2 = jnp.concatenate([jnp.where(before[d], gram_k[ch][:c], 0.0), jnp.where(upto[d], gram_k[ch][c:], 0.0)],
                             axis=0)
        kv[ch] = _dot(m2.astype(BF16), _block_diag(ops[d]["v"][:, cols(g)], diag))
    ub = {}
    for ch in chains:
        rhs = sx[ch][:c] + kv[ch][:c]
        ub[ch] = _dot(tinv[ch].astype(BF16), _block_diag(rhs.astype(BF16), diag)).astype(BF16)
    for ch in chains:
        d, g = ch
        rb = jnp.where(upto[d], gram_b[ch][c:], 0.0).astype(BF16)
        y_refs[d][:, cols(g)] = sx[ch][c:] + kv[ch][c:] - _dot(rb, _block_diag(ub[ch], diag))
    for ch in chains:
        d, g = ch
        upd = _dot_tn(jnp.concatenate([ops[d]["v"][:, cols(g)], ub[ch]], axis=0), ops[d]["tail"][:, cols(g)])
        s_ref[d, g] = s0[ch] * ops[d]["w_end"][:, cols(g)] + jnp.where(diag, upd, 0.0)


def _rwkv_scan_kernel(rf_ref, kf_ref, vf_ref, lf_ref, rb_ref, kb_ref, vb_ref, lb_ref,
                      dbase_ref, dup_ref, ibase_ref, iup_ref, kk_ref, ka_ref, yf_ref, yb_ref, s_ref):
    @pl.when(pl.program_id(0) == 0)
    def _():
        s_ref[...] = jnp.zeros_like(s_ref)

    k_k = kk_ref[...]
    k_a = ka_ref[...]
    ops = []
    for d, (r_ref, k_ref, v_ref, l_ref) in enumerate(
            ((rf_ref, kf_ref, vf_ref, lf_ref), (rb_ref, kb_ref, vb_ref, lb_ref))):
        k = k_ref[...]
        kk = k * k_k
        kk = _per_head(kk, lambda x: x / jnp.maximum(jnp.sqrt(jnp.sum(x * x, axis=-1, keepdims=True)), 1e-12))
        ops.append(_rwkv_chunk_operands(d, r_ref[...], k, v_ref[...], l_ref[...], dbase_ref, dup_ref, ibase_ref,
                                        iup_ref, kk, k_a))
    _rwkv_chunk_update(ops, s_ref, (yf_ref, yb_ref), (0, 1))


def _rwkv_scan(p, decay_base, decay_up, iclr_base, iclr_up, k_k, k_a, n_ctx):
    n_rows = p.shape[0]
    c = RWKV_CHUNK
    n_chunks = n_rows // c
    ctx_chunks = n_ctx // c

    def fwd(g):
        return g

    def bwd(g):
        return jnp.where(g < ctx_chunks, ctx_chunks - 1 - g, ctx_chunks + n_chunks - 1 - g)

    def specs(order):
        return [pl.BlockSpec((c, RWKV_W), lambda g: (order(g), COL_R // RWKV_W)),
                pl.BlockSpec((c, RWKV_W), lambda g: (order(g), COL_RK // RWKV_W)),
                pl.BlockSpec((c, RWKV_W), lambda g: (order(g), COL_RV // RWKV_W)),
                pl.BlockSpec((c, LR_W), lambda g: (order(g), COL_LR // LR_W))]

    full = lambda shape: pl.BlockSpec(shape, lambda g: (0,) * len(shape))
    return pl.pallas_call(
        _rwkv_scan_kernel,
        grid=(n_chunks,),
        in_specs=specs(fwd) + specs(bwd) + [
            full((2, RWKV_W)), full((2, DECAY_RANK, RWKV_W)), full((2, RWKV_W)), full((2, ICLR_RANK, RWKV_W)),
            full((1, RWKV_W)), full((1, RWKV_W))],
        out_specs=[pl.BlockSpec((c, RWKV_W), lambda g: (fwd(g), 0)),
                   pl.BlockSpec((c, RWKV_W), lambda g: (bwd(g), 0))],
        out_shape=[jax.ShapeDtypeStruct((n_rows, RWKV_W), F32)] * 2,
        scratch_shapes=[pltpu.VMEM((2, RWKV_HEADS // HEADS_PER_GROUP, HEADS_PER_GROUP * HEAD_DIM,
                                   HEADS_PER_GROUP * HEAD_DIM), F32)],
        compiler_params=_params(1),
        name="rwkv7_chunk_scan",
    )(p, p, p, p, p, p, p, p, decay_base, decay_up, iclr_base, iclr_up,
      k_k.reshape(1, RWKV_W), k_a.reshape(1, RWKV_W))


def _head_sums(x):
    gw = HEADS_PER_GROUP * HEAD_DIM
    bi = lax.broadcasted_iota(jnp.int32, (gw, gw), 0) // HEAD_DIM
    bj = lax.broadcasted_iota(jnp.int32, (gw, gw), 1) // HEAD_DIM
    ones = (bi == bj).astype(BF16)
    hi = x.astype(BF16)
    lo = (x - hi.astype(F32)).astype(BF16)
    parts = []
    for g in range(x.shape[1] // gw):
        gs = slice(g * gw, (g + 1) * gw)
        parts.append(_dot(hi[:, gs], ones) + _dot(lo[:, gs], ones))
    return jnp.concatenate(parts, axis=1)


def _rwkv_out_kernel(yf_ref, yb_ref, r_ref, k_ref, v_ref, lr_ref, ibase_ref, iup_ref, ka_ref, rk_ref,
                     lw_ref, lb_ref, gup_ref, o_ref):
    y = yf_ref[...] + yb_ref[...]
    r = r_ref[...]
    k = k_ref[...]
    v = v_ref[...]
    lr = lr_ref[...]
    k_a = ka_ref[...]

    yc = y - _head_sums(y) * (1.0 / HEAD_DIM)
    var = _head_sums(yc * yc) * (1.0 / HEAD_DIM)
    yn = yc * lax.rsqrt(var + RWKV_GN_EPS) * lw_ref[...] + lb_ref[...]
    kd_sum = jnp.zeros_like(k)
    for d in range(2):
        ad = lr[:, 2 * DECAY_RANK + d * ICLR_RANK:2 * DECAY_RANK + (d + 1) * ICLR_RANK]
        a = _iclr(ad, ibase_ref[d:d + 1, :], iup_ref[d])
        kd_sum = kd_sum + k * (1.0 + (a - 1.0) * k_a)
    rkk = r * kd_sum * rk_ref[...]
    bonus = _head_sums(rkk) * v
    gd = lr[:, 4 * DECAY_RANK:4 * DECAY_RANK + GATE_RANK]
    gate = _dot(jax.nn.sigmoid(gd).astype(BF16), gup_ref[...].astype(BF16))
    o_ref[...] = ((yn + bonus) * gate).astype(o_ref.dtype)


def _rwkv_out(yf, yb, p, iclr_base, iclr_up, k_a, r_k, lnx_w, lnx_b, gate_up):
    n_rows = p.shape[0]
    tm = 256
    row = lambda w, cb: pl.BlockSpec((tm, w), lambda i, cb=cb: (i, cb))
    full = lambda shape: pl.BlockSpec(shape, lambda i: (0,) * len(shape))
    vec = lambda t: t.reshape(1, RWKV_W)
    return pl.pallas_call(
        _rwkv_out_kernel,
        grid=(n_rows // tm,),
        in_specs=[row(RWKV_W, 0), row(RWKV_W, 0), row(RWKV_W, COL_R // RWKV_W), row(RWKV_W, COL_RK // RWKV_W),
                  row(RWKV_W, COL_RV // RWKV_W), row(LR_W, COL_LR // LR_W),
                  full((2, RWKV_W)), full((2, ICLR_RANK, RWKV_W)), full((1, RWKV_W)), full((1, RWKV_W)),
                  full((1, RWKV_W)), full((1, RWKV_W)), full((GATE_RANK, RWKV_W))],
        out_specs=pl.BlockSpec((tm, RWKV_W), lambda i: (i, 0)),
        out_shape=jax.ShapeDtypeStruct((n_rows, RWKV_W), BF16),
        compiler_params=_params(1),
        name="rwkv7_out",
    )(yf, yb, p, p, p, p, iclr_base, iclr_up, vec(k_a), vec(r_k), vec(lnx_w), vec(lnx_b), gate_up)


def _out_proj_kernel(a_ref, b_ref, c_ref, wa_ref, wb_ref, wc_ref, x_ref, gate_ref, o_ref, *, n_ctx, tm):
    acc = _dot(a_ref[...], wa_ref[...]) + _dot(b_ref[...], wb_ref[...]) + _dot(c_ref[...], wc_ref[...])
    row = pl.program_id(0) * tm + lax.broadcasted_iota(jnp.int32, (tm, 1), 0)
    gate = jnp.where(row < n_ctx, gate_ref[1:2, :], gate_ref[0:1, :])
    o_ref[...] = x_ref[...] + gate * acc


def _out_proj(a, b, c, w_out_bf16, xs, gate, n_ctx):
    n_rows, d = xs.shape
    tm = _row_tile(n_rows, (1280, 1024, 512, 256))
    tn = 512
    wa, wb, wc = w_out_bf16[:CONV_W], w_out_bf16[CONV_W:CONV_W + ATT_W], w_out_bf16[CONV_W + ATT_W:]
    return pl.pallas_call(
        functools.partial(_out_proj_kernel, n_ctx=n_ctx, tm=tm),
        grid=(n_rows // tm, d // tn),
        in_specs=[pl.BlockSpec((tm, CONV_W), lambda i, j: (i, 0)),
                  pl.BlockSpec((tm, ATT_W), lambda i, j: (i, 0)),
                  pl.BlockSpec((tm, RWKV_W), lambda i, j: (i, 0)),
                  pl.BlockSpec((CONV_W, tn), lambda i, j: (0, j)),
                  pl.BlockSpec((ATT_W, tn), lambda i, j: (0, j)),
                  pl.BlockSpec((RWKV_W, tn), lambda i, j: (0, j)),
                  pl.BlockSpec((tm, tn), lambda i, j: (i, j)),
                  pl.BlockSpec((8, tn), lambda i, j: (0, j))],
        out_specs=pl.BlockSpec((tm, tn), lambda i, j: (i, j)),
        out_shape=jax.ShapeDtypeStruct((n_rows, d), F32),
        compiler_params=_params(2),
        name="out_proj_residual",
    )(a, b, c, wa, wb, wc, xs, gate)


def _first_max(x, idx):
    m = jnp.max(x, axis=0, keepdims=True)
    first = jnp.min(jnp.where(x == m, idx, N_EXPERTS), axis=0, keepdims=True)
    return m, first


def _router_kernel(x_ref, g_ref, mod_ref, rw_ref, rb_ref, h_ref, idx_ref, gate_ref, rank_ref, cnt_ref, base_ref,
                   *, n_ctx, tm):
    @pl.when(pl.program_id(0) == 0)
    def _():
        base_ref[...] = jnp.zeros_like(base_ref)

    h = _norm_mod(x_ref[...], g_ref[...], mod_ref[...], pl.program_id(0) * tm, n_ctx)
    h_ref[...] = h
    logits = _dot_nt(rw_ref[...], h, HIGHEST)
    scores = jax.nn.sigmoid(logits)
    sel = scores + rb_ref[...]
    eidx = lax.broadcasted_iota(jnp.int32, sel.shape, 0)
    best = best_score = None
    for g in range(N_EXPERT_GROUPS):
        rows = slice(g * EXPERTS_PER_GROUP, (g + 1) * EXPERTS_PER_GROUP)
        x = sel[rows]
        xi = g * EXPERTS_PER_GROUP + lax.broadcasted_iota(jnp.int32, x.shape, 0)
        m1, i1 = _first_max(x, xi)
        m2, _ = _first_max(jnp.where(xi == i1, -jnp.inf, x), xi)
        score = m1 + m2
        if g == 0:
            best, best_score = jnp.zeros_like(i1), score
        else:
            better = score > best_score
            best = jnp.where(better, g, best)
            best_score = jnp.where(better, score, best_score)
    masked = jnp.where(eidx // EXPERTS_PER_GROUP == best, sel, NEG_INF)
    _, e1 = _first_max(masked, eidx)
    _, e2 = _first_max(jnp.where(eidx == e1, -jnp.inf, masked), eidx)
    g1 = jnp.sum(jnp.where(eidx == e1, scores, 0.0), axis=0, keepdims=True)
    g2 = jnp.sum(jnp.where(eidx == e2, scores, 0.0), axis=0, keepdims=True)
    idx_ref[0:1, :] = e1
    idx_ref[1:2, :] = e2
    gate_ref[0:1, :] = g1 / (g1 + g2)
    gate_ref[1:2, :] = g2 / (g1 + g2)

    si = lax.broadcasted_iota(jnp.int32, (tm, tm), 0)
    ti = lax.broadcasted_iota(jnp.int32, (tm, tm), 1)
    prefix = (si <= ti).astype(BF16)
    base = base_ref[...]
    for kth, e in enumerate((e1, e2)):
        hit = eidx == e
        seen = _dot(hit.astype(BF16), prefix)
        rank = jnp.sum(jnp.where(hit, seen - 1.0 + base, 0.0), axis=0, keepdims=True)
        rank_ref[kth:kth + 1, :] = rank.astype(jnp.int32)
        base = base + seen[:, tm - 1:tm]
    base_ref[...] = base
    cnt_ref[...] = base.astype(jnp.int32)


def _route(xs, g, mod, router_w, router_b, n_ctx):
    n_rows, d = xs.shape
    tm = 256
    return pl.pallas_call(
        functools.partial(_router_kernel, n_ctx=n_ctx, tm=tm),
        grid=(n_rows // tm,),
        in_specs=[pl.BlockSpec((tm, d), lambda i: (i, 0)),
                  pl.BlockSpec((1, d), lambda i: (0, 0)),
                  pl.BlockSpec((8, d), lambda i: (0, 0)),
                  pl.BlockSpec((N_EXPERTS, d), lambda i: (0, 0)),
                  pl.BlockSpec((N_EXPERTS, 1), lambda i: (0, 0))],
        out_specs=[pl.BlockSpec((tm, d), lambda i: (i, 0)),
                   pl.BlockSpec((TOP_K, tm), lambda i: (0, i)),
                   pl.BlockSpec((TOP_K, tm), lambda i: (0, i)),
                   pl.BlockSpec((TOP_K, tm), lambda i: (0, i)),
                   pl.BlockSpec((N_EXPERTS, 1), lambda i: (0, 0))],
        out_shape=[jax.ShapeDtypeStruct((n_rows, d), F32),
                   jax.ShapeDtypeStruct((TOP_K, n_rows), jnp.int32),
                   jax.ShapeDtypeStruct((TOP_K, n_rows), F32),
                   jax.ShapeDtypeStruct((TOP_K, n_rows), jnp.int32),
                   jax.ShapeDtypeStruct((N_EXPERTS, 1), jnp.int32)],
        scratch_shapes=[pltpu.VMEM((N_EXPERTS, 1), F32)],
        compiler_params=_params(1),
        name="moe_norm_route",
    )(xs, g.reshape(1, d), mod, router_w.T, router_b.reshape(N_EXPERTS, 1))


def _row_copy(src_ref, src_row, dst_ref, dst_row, sem):
    return pltpu.make_async_copy(src_ref.at[pl.ds(src_row, 1), :], dst_ref.at[pl.ds(dst_row, 1), :], sem)


def _dispatch_kernel(dest_ref, h_ref, init_ref, xg_ref, sem, *, tm):
    del init_ref

    def start(t, carry):
        for k in range(TOP_K):
            _row_copy(h_ref, t, xg_ref, dest_ref[0, k, t], sem).start()
        return carry

    lax.fori_loop(0, tm, start, 0, unroll=DMA_UNROLL)
    for _ in range(TOP_K):
        pltpu.make_async_copy(h_ref, xg_ref.at[pl.ds(0, tm), :], sem).wait()


def _dispatch(hp, dest_tiles, cap, tm):
    n_rows, w = hp.shape
    return pl.pallas_call(
        functools.partial(_dispatch_kernel, tm=tm),
        grid=(n_rows // tm,),
        in_specs=[pl.BlockSpec((1, TOP_K, tm), lambda i: (i, 0, 0), memory_space=pltpu.SMEM),
                  pl.BlockSpec((tm, w), lambda i: (i, 0)),
                  pl.BlockSpec(memory_space=pl.ANY)],
        out_specs=pl.BlockSpec(memory_space=pl.ANY),
        out_shape=jax.ShapeDtypeStruct((cap, w), hp.dtype),
        scratch_shapes=[pltpu.SemaphoreType.DMA(())],
        input_output_aliases={2: 0},
        compiler_params=_params(1),
        name="moe_dispatch_rows",
    )(dest_tiles, hp, jnp.zeros((cap, w), hp.dtype))


def _expert_kernel(be_ref, nused_ref, x_ref, wg_ref, wu_ref, wd_ref, o_ref, wgb_ref, wub_ref, wdb_ref):
    i = pl.program_id(0)
    used = i < nused_ref[0]
    new_expert = (i == 0) | (be_ref[i] != be_ref[jnp.maximum(i - 1, 0)])

    @pl.when(used & new_expert)
    def _():
        for r0 in range(0, D_MODEL, CAST_ROWS):
            rows = slice(r0, r0 + CAST_ROWS)
            wgb_ref[rows, :] = wg_ref[0, rows, :].astype(BF16)
            wub_ref[rows, :] = wu_ref[0, rows, :].astype(BF16)
        for r0 in range(0, D_EXPERT, CAST_ROWS):
            rows = slice(r0, r0 + CAST_ROWS)
            wdb_ref[rows, :] = wd_ref[0, rows, :].astype(BF16)

    @pl.when(used)
    def _():
        x = x_ref[...].astype(BF16)
        gate = _dot(x, wgb_ref[...])
        up = _dot(x, wub_ref[...])
        act = (gate * jax.nn.sigmoid(gate) * up).astype(BF16)
        o_ref[...] = _dot(act, wdb_ref[...])

    @pl.when(i >= nused_ref[0])
    def _():
        o_ref[...] = jnp.zeros_like(o_ref)


def _expert_ffn(xg, block_e, n_used, wg, wu, wd, layer, tm):
    cap, d = xg.shape
    grid_spec = pltpu.PrefetchScalarGridSpec(
        num_scalar_prefetch=2,
        grid=(cap // tm,),
        in_specs=[pl.BlockSpec((tm, d), lambda i, be, nu: (i, 0)),
                  pl.BlockSpec((None, 1, d, D_EXPERT), lambda i, be, nu: (layer, be[i], 0, 0)),
                  pl.BlockSpec((None, 1, d, D_EXPERT), lambda i, be, nu: (layer, be[i], 0, 0)),
                  pl.BlockSpec((None, 1, D_EXPERT, d), lambda i, be, nu: (layer, be[i], 0, 0))],
        out_specs=pl.BlockSpec((tm, d), lambda i, be, nu: (i, 0)),
        scratch_shapes=[pltpu.VMEM((d, D_EXPERT), BF16), pltpu.VMEM((d, D_EXPERT), BF16),
                        pltpu.VMEM((D_EXPERT, d), BF16)],
    )
    return pl.pallas_call(
        _expert_kernel,
        grid_spec=grid_spec,
        out_shape=jax.ShapeDtypeStruct((cap, d), F32),
        compiler_params=pltpu.CompilerParams(dimension_semantics=("arbitrary",),
                                             vmem_limit_bytes=EXPERT_VMEM_LIMIT),
        name="moe_expert_ffn",
    )(block_e, n_used, xg, wg, wu, wd)


def _combine_kernel(dest_ref, x_ref, gt_ref, g2_ref, fg_ref, yb_ref, o_ref, ybuf_ref, sem,
                    *, n_ctx, tm, final_norm, first_tile):
    def start(t, carry):
        for k in range(TOP_K):
            _row_copy(yb_ref, dest_ref[0, k, t], ybuf_ref.at[k], t, sem).start()
        return carry

    lax.fori_loop(0, tm, start, 0, unroll=DMA_UNROLL)
    for k in range(TOP_K):
        pltpu.make_async_copy(yb_ref.at[pl.ds(0, tm), :], ybuf_ref.at[k], sem).wait()

    is_ctx = (pl.program_id(0) + first_tile) * tm < n_ctx
    g2 = jnp.where(is_ctx, g2_ref[1:2, :], g2_ref[0:1, :])
    gt = gt_ref[...]
    y = gt[:, 0:1] * ybuf_ref[0] + gt[:, 1:2] * ybuf_ref[1]
    x = x_ref[...] + g2 * y
    if final_norm:
        ms = jnp.mean(x * x, axis=-1, keepdims=True)
        x = x * lax.rsqrt(ms + RMS_EPS) * fg_ref[...]
    o_ref[...] = x


def _combine(xs, yb, dest_tiles, gates_t, g2, final_g, n_ctx, final_norm, tm):
    n_rows, d = xs.shape
    skip = n_ctx // tm if final_norm else 0
    return pl.pallas_call(
        functools.partial(_combine_kernel, n_ctx=n_ctx, tm=tm, final_norm=final_norm, first_tile=skip),
        grid=(n_rows // tm - skip,),
        in_specs=[pl.BlockSpec((1, TOP_K, tm), lambda i: (i + skip, 0, 0), memory_space=pltpu.SMEM),
                  pl.BlockSpec((tm, d), lambda i: (i + skip, 0)),
                  pl.BlockSpec((tm, TOP_K), lambda i: (i + skip, 0)),
                  pl.BlockSpec((8, d), lambda i: (0, 0)), pl.BlockSpec((1, d), lambda i: (0, 0)),
                  pl.BlockSpec(memory_space=pl.ANY)],
        out_specs=pl.BlockSpec((tm, d), lambda i: (i, 0)),
        out_shape=jax.ShapeDtypeStruct((n_rows - skip * tm, d), F32),
        scratch_shapes=[pltpu.VMEM((TOP_K, tm, yb.shape[1]), yb.dtype), pltpu.SemaphoreType.DMA(())],
        compiler_params=_params(1),
        name="moe_combine_residual",
    )(dest_tiles, xs, gates_t, g2, final_g.reshape(1, d), yb)


def _moe(xs, g, mod, g2, router_w, router_b, wg, wu, wd, layer, final_g, n_ctx, final_norm):
    n_rows, d = xs.shape
    h, idx, gates, rank, counts = _route(xs, g, mod, router_w, router_b, n_ctx)
    tm = 256
    n_asg = n_rows * TOP_K
    n_blk = n_asg // tm + N_EXPERTS
    cap = n_blk * tm
    counts = counts[:, 0]
    padded = (counts + tm - 1) // tm * tm
    pends = jnp.cumsum(padded)
    pstarts = pends - padded
    seg_start = jnp.sum(jnp.where(idx[..., None] == jnp.arange(N_EXPERTS), pstarts, 0), axis=-1)
    dest = seg_start + rank
    dest_tiles = dest.reshape(TOP_K, n_rows // tm, tm).transpose(1, 0, 2).astype(jnp.int32)
    block_e = jnp.minimum(jnp.sum(pends[None, :] <= (jnp.arange(n_blk) * tm)[:, None], axis=1), N_EXPERTS - 1)
    n_used = (pends[-1] // tm).reshape(1)
    xg = _dispatch(h, dest_tiles, cap, tm)
    yb = _expert_ffn(xg, block_e.astype(jnp.int32), n_used.astype(jnp.int32), wg, wu, wd, layer, tm)
    return _combine(xs, yb, dest_tiles, gates.T, g2, final_g, n_ctx, final_norm, tm)


def _mod_rows(mod_l, lat_chunks, ctx_chunks):
    d = D_MODEL
    rows = [mod_l[0, c * d:(c + 1) * d] for c in lat_chunks] + [mod_l[1, c * d:(c + 1) * d] for c in ctx_chunks]
    out = jnp.zeros((8, d), F32)
    return out.at[:len(rows)].set(jnp.stack(rows))


def kernel(x, c, ctx, c_ctx, ada_w, ada_b, norm1_g, norm2_g, w_in, w_out, conv_w, attn_sinks, decay_base,
           decay_up, iclr_base, iclr_up, gate_up, k_k, k_a, r_k, lnx_w, lnx_b, router_w, router_b,
           expert_gate, expert_up, expert_down, final_norm_g):
    bsz, n_lat, d = x.shape
    n_ctx = ctx.shape[1]
    depth = ada_w.shape[0]
    assert bsz == 1 and d == D_MODEL and n_ctx % NORM_ROWS == 0 and n_lat % NORM_ROWS == 0

    xs = jnp.concatenate([ctx[0], x[0]], axis=0)
    mods = _ada_mod(c, c_ctx, ada_w, ada_b)
    cos, sin = _rope_tables(n_ctx, n_lat)

    for l in range(depth):
        last = l == depth - 1
        mod1 = _mod_rows(mods[l], (0, 1), (0, 1))
        gate1 = _mod_rows(mods[l], (2,), (2,))
        mod2 = _mod_rows(mods[l], (3, 4), (3, 4))
        gate2 = _mod_rows(mods[l], (5,), (5,))

        p = _norm_mod_matmul(xs, norm1_g[l], mod1, _in_proj_weights(w_in[l]), n_ctx)

        a_mix = _short_conv(p, conv_w[l], n_ctx)
        qr, kr, vb = _rope_qkv(p, cos, sin)
        b_mix = _window_attention(qr, kr, vb, attn_sinks[l], n_ctx, n_lat)
        yf, yb = _rwkv_scan(p, decay_base[l], decay_up[l], iclr_base[l], iclr_up[l], k_k[l], k_a[l], n_ctx)
        c_mix = _rwkv_out(yf, yb, p, iclr_base[l], iclr_up[l], k_a[l], r_k[l].reshape(-1), lnx_w[l], lnx_b[l],
                          gate_up[l])
        xs = _out_proj(a_mix, b_mix, c_mix, w_out[l].astype(BF16), xs, gate1, n_ctx)

        xs = _moe(xs, norm2_g[l], mod2, gate2, router_w, router_b, expert_gate, expert_up, expert_down, l,
                  final_norm_g, n_ctx, last)
    return xs.reshape(bsz, n_lat, d)
```

```python
import functools

import numpy as np
import jax
import jax.numpy as jnp
from jax import lax
from jax.experimental import pallas as pl
from jax.experimental.pallas import tpu as pltpu

F32 = jnp.float32
BF16 = jnp.bfloat16
HIGHEST = lax.Precision.HIGHEST

D_MODEL = 2048
GRID_W = 64
CONV_W = D_MODEL // 4
CONV_K = 3
HEAD_DIM = 64
ATT_HEADS = 12
ATT_KV_HEADS = 4
ATT_GROUP = ATT_HEADS // ATT_KV_HEADS
ATT_W = ATT_HEADS * HEAD_DIM
ATT_KV_W = ATT_KV_HEADS * HEAD_DIM
RWKV_HEADS = 12
RWKV_W = RWKV_HEADS * HEAD_DIM
WINDOW = 128
ATT_BLOCK = 128
ROPE_THETA = 10000.0
ROPE_FREQS = HEAD_DIM // 4
DECAY_RANK = 64
ICLR_RANK = 64
GATE_RANK = 128
RWKV_GN_EPS = 64e-5
N_EXPERTS = 32
N_EXPERT_GROUPS = 4
EXPERTS_PER_GROUP = N_EXPERTS // N_EXPERT_GROUPS
TOP_K = 2
D_EXPERT = 768
RMS_EPS = 1e-6
NEG_INF = -1e30

COL_Q = 0
COL_R = 768
COL_RK = 1536
COL_RV = 2304
COL_CB = 3072
COL_CC = 3584
COL_CH = 4096
COL_K = 4608
COL_V = 4864
COL_LR = 5120
LR_W = 512
IN_W_PAD = 5632

RWKV_CHUNK = 64
HEADS_PER_GROUP = 4
NORM_ROWS = 256
SUBLANES = 8
CAST_ROWS = 256
DMA_UNROLL = 8
VMEM_LIMIT = 56 * 1024 * 1024
EXPERT_VMEM_LIMIT = 62 * 1024 * 1024


def _params(n_axes):
    return pltpu.CompilerParams(dimension_semantics=("arbitrary",) * n_axes,
                                vmem_limit_bytes=VMEM_LIMIT)


def _row_tile(n_rows, candidates):
    for t in candidates:
        if n_rows % t == 0:
            return t
    raise ValueError(f"no row tile for {n_rows}")


def _in_proj_weights(w):
    o_cb, o_cc, o_ch = 0, 512, 1024
    o_q, o_k, o_v = 1536, 2304, 2560
    o_r, o_rk, o_rv = 2816, 3584, 4352
    o_lr = 5120
    segs = [(o_q, 768), (o_r, 768), (o_rk, 768), (o_rv, 768), (o_cb, 512), (o_cc, 512), (o_ch, 512),
            (o_k, 256), (o_v, 256), (o_lr, 384)]
    parts = [w[:, o:o + n].astype(BF16) for o, n in segs]
    parts.append(jnp.zeros((w.shape[0], IN_W_PAD - sum(n for _, n in segs)), BF16))
    return jnp.concatenate(parts, axis=1)


def _ada_kernel(s_ref, w_ref, b_ref, o_ref):
    o_ref[0] = jnp.dot(s_ref[...], w_ref[0], precision=HIGHEST, preferred_element_type=F32) + b_ref[0]


def _ada_mod(c, c_ctx, ada_w, ada_b):
    depth, d, n = ada_w.shape
    s = jnp.zeros((8, d), F32).at[0].set(c[0]).at[1].set(c_ctx)
    s = s * jax.nn.sigmoid(s)
    tn = 1024
    return pl.pallas_call(
        _ada_kernel,
        grid=(depth, n // tn),
        in_specs=[pl.BlockSpec((8, d), lambda l, j: (0, 0)),
                  pl.BlockSpec((1, d, tn), lambda l, j: (l, 0, j)),
                  pl.BlockSpec((1, 1, tn), lambda l, j: (l, 0, j))],
        out_specs=pl.BlockSpec((1, 8, tn), lambda l, j: (l, 0, j)),
        out_shape=jax.ShapeDtypeStruct((depth, 8, n), F32),
        compiler_params=_params(2),
        name="ada_mod",
    )(s, ada_w, ada_b.reshape(depth, 1, n))


def _norm_mod(x, g, mod, row0, n_ctx):
    ms = jnp.mean(x * x, axis=-1, keepdims=True)
    y = x * lax.rsqrt(ms + RMS_EPS) * g
    is_ctx = row0 < n_ctx
    shift = jnp.where(is_ctx, mod[2:3, :], mod[0:1, :])
    scale = jnp.where(is_ctx, mod[3:4, :], mod[1:2, :])
    return y * (1.0 + scale) + shift


def _nmm_kernel(x_ref, g_ref, mod_ref, w_ref, o_ref, h_ref, *, n_ctx, tm):
    i = pl.program_id(0)

    @pl.when(pl.program_id(1) == 0)
    def _():
        for r0 in range(0, tm, NORM_ROWS):
            rows = slice(r0, r0 + NORM_ROWS)
            h_ref[rows, :] = _norm_mod(x_ref[rows, :], g_ref[...], mod_ref[...], i * tm + r0, n_ctx).astype(BF16)

    o_ref[...] = jnp.dot(h_ref[...], w_ref[...], preferred_element_type=F32)


def _norm_mod_matmul(xs, g, mod, w_bf16, n_ctx):
    n_rows, d = xs.shape
    n_out = w_bf16.shape[1]
    tm = _row_tile(n_rows, (1280, 1024, 512, 256))
    tn = 512
    return pl.pallas_call(
        functools.partial(_nmm_kernel, n_ctx=n_ctx, tm=tm),
        grid=(n_rows // tm, n_out // tn),
        in_specs=[pl.BlockSpec((tm, d), lambda i, j: (i, 0)),
                  pl.BlockSpec((1, d), lambda i, j: (0, 0)),
                  pl.BlockSpec((8, d), lambda i, j: (0, 0)),
                  pl.BlockSpec((d, tn), lambda i, j: (0, j))],
        out_specs=pl.BlockSpec((tm, tn), lambda i, j: (i, j)),
        out_shape=jax.ShapeDtypeStruct((n_rows, n_out), F32),
        scratch_shapes=[pltpu.VMEM((tm, d), BF16)],
        compiler_params=_params(2),
        name="norm_mod_in_proj",
    )(xs, g.reshape(1, d), mod, w_bf16)


def _conv_kernel(cb_ref, cc_ref, ch_ref, ccp_ref, chp_ref, ccn_ref, chn_ref, w_ref, o_ref, *, n_ctx, n_rows, tm):
    i = pl.program_id(0)
    u = cc_ref[...] * ch_ref[...]
    u_prev_row = ccp_ref[7:8, :] * chp_ref[7:8, :]
    u_next_row = ccn_ref[0:1, :] * chn_ref[0:1, :]
    loc = lax.broadcasted_iota(jnp.int32, (tm, 1), 0)
    row = i * tm + loc
    up = jnp.where(loc == 0, u_prev_row, pltpu.roll(u, 1, axis=0))
    dn = jnp.where(loc == tm - 1, u_next_row, pltpu.roll(u, tm - 1, axis=0))
    up = jnp.where((row == 0) | (row == n_ctx), 0.0, up)
    dn = jnp.where((row == n_ctx - 1) | (row == n_rows - 1), 0.0, dn)
    w = w_ref[...]
    y = w[0:1, :] * up + w[1:2, :] * u + w[2:3, :] * dn
    o_ref[...] = (cb_ref[...] * y).astype(o_ref.dtype)


def _short_conv(p, conv_w, n_ctx):
    n_rows = p.shape[0]
    tm = _row_tile(n_rows, (1280, 1024, 512, 256))
    r8 = tm // 8
    last8 = n_rows // 8 - 1
    wpad = jnp.zeros((8, CONV_W), F32).at[:CONV_K].set(conv_w)
    blk = lambda c: pl.BlockSpec((tm, CONV_W), lambda i, c=c: (i, c))
    prev = lambda c: pl.BlockSpec((8, CONV_W), lambda i, c=c: (jnp.maximum(i * r8 - 1, 0), c))
    nxt = lambda c: pl.BlockSpec((8, CONV_W), lambda i, c=c: (jnp.minimum((i + 1) * r8, last8), c))
    cb, cc, ch = COL_CB // CONV_W, COL_CC // CONV_W, COL_CH // CONV_W
    return pl.pallas_call(
        functools.partial(_conv_kernel, n_ctx=n_ctx, n_rows=n_rows, tm=tm),
        grid=(n_rows // tm,),
        in_specs=[blk(cb), blk(cc), blk(ch), prev(cc), prev(ch), nxt(cc), nxt(ch),
                  pl.BlockSpec((8, CONV_W), lambda i: (0, 0))],
        out_specs=pl.BlockSpec((tm, CONV_W), lambda i: (i, 0)),
        out_shape=jax.ShapeDtypeStruct((n_rows, CONV_W), BF16),
        compiler_params=_params(1),
        name="short_conv",
    )(p, p, p, p, p, p, p, wpad)


def _swap_halves(x):
    n = x.shape[1]
    lane = lax.broadcasted_iota(jnp.int32, x.shape, 1)
    fwd = pltpu.roll(x, n - ROPE_FREQS, axis=1)
    bwd = pltpu.roll(x, ROPE_FREQS, axis=1)
    return jnp.where((lane % (2 * ROPE_FREQS)) < ROPE_FREQS, fwd, bwd)


def _rope_kernel(q_ref, k_ref, v_ref, cos_ref, sin_ref, qo_ref, ko_ref, vo_ref):
    cos = cos_ref[...]
    sin = sin_ref[...]
    q = q_ref[...]
    k = k_ref[...]
    cos_q = jnp.concatenate([cos] * (ATT_W // 128), axis=1)
    sin_q = jnp.concatenate([sin] * (ATT_W // 128), axis=1)
    cos_k = jnp.concatenate([cos] * (ATT_KV_W // 128), axis=1)
    sin_k = jnp.concatenate([sin] * (ATT_KV_W // 128), axis=1)
    qo_ref[...] = ((q * cos_q + _swap_halves(q) * sin_q) * (HEAD_DIM ** -0.5)).astype(BF16)
    ko_ref[...] = (k * cos_k + _swap_halves(k) * sin_k).astype(BF16)
    vo_ref[...] = v_ref[...].astype(BF16)


def _rope_tables(n_ctx, n_lat):
    row = jnp.repeat(jnp.arange(n_lat // GRID_W, dtype=jnp.int32), GRID_W)
    col = jnp.arange(n_lat, dtype=jnp.int32) % GRID_W
    inv = ROPE_THETA ** (-jnp.arange(ROPE_FREQS, dtype=F32) / ROPE_FREQS)
    ang_r = row[:, None].astype(F32) * inv[None, :]
    ang_c = col[:, None].astype(F32) * inv[None, :]
    cr, sr, cc, sc = jnp.cos(ang_r), jnp.sin(ang_r), jnp.cos(ang_c), jnp.sin(ang_c)
    cos = jnp.concatenate([cr, cr, cc, cc], axis=1)
    sin = jnp.concatenate([-sr, sr, -sc, sc], axis=1)
    cos = jnp.concatenate([jnp.ones((n_ctx, HEAD_DIM), F32), cos], axis=0)
    sin = jnp.concatenate([jnp.zeros((n_ctx, HEAD_DIM), F32), sin], axis=0)
    return jnp.tile(cos, (1, 2)), jnp.tile(sin, (1, 2))


def _rope_qkv(p, cos, sin):
    n_rows = p.shape[0]
    tm = _row_tile(n_rows, (1280, 1024, 512, 256))
    return pl.pallas_call(
        _rope_kernel,
        grid=(n_rows // tm,),
        in_specs=[pl.BlockSpec((tm, ATT_W), lambda i: (i, COL_Q // ATT_W)),
                  pl.BlockSpec((tm, ATT_KV_W), lambda i: (i, COL_K // ATT_KV_W)),
                  pl.BlockSpec((tm, ATT_KV_W), lambda i: (i, COL_V // ATT_KV_W)),
                  pl.BlockSpec((tm, 128), lambda i: (i, 0)),
                  pl.BlockSpec((tm, 128), lambda i: (i, 0))],
        out_specs=[pl.BlockSpec((tm, ATT_W), lambda i: (i, 0)),
                   pl.BlockSpec((tm, ATT_KV_W), lambda i: (i, 0)),
                   pl.BlockSpec((tm, ATT_KV_W), lambda i: (i, 0))],
        out_shape=[jax.ShapeDtypeStruct((n_rows, ATT_W), BF16),
                   jax.ShapeDtypeStruct((n_rows, ATT_KV_W), BF16),
                   jax.ShapeDtypeStruct((n_rows, ATT_KV_W), BF16)],
        compiler_params=_params(1),
        name="rope_qkv",
    )(p, p, p, cos, sin)


def _attn_kernel(sink_ref, bias_ref, q_ref, kp_ref, kc_ref, kn_ref, vp_ref, vc_ref, vn_ref, kx_ref, vx_ref, o_ref,
                 *, n_ctx, n_blocks):
    i = pl.program_id(0)
    blk = ATT_BLOCK
    ctx_blocks = n_ctx // blk
    ci = lax.broadcasted_iota(jnp.int32, (1, n_ctx + 3 * blk), 1)
    no_prev = i <= ctx_blocks
    no_cur = i < ctx_blocks
    no_next = no_cur | (i == n_blocks - 1)
    dead = (((ci >= n_ctx) & (ci < n_ctx + blk) & no_prev)
            | ((ci >= n_ctx + blk) & (ci < n_ctx + 2 * blk) & no_cur)
            | ((ci >= n_ctx + 2 * blk) & no_next))
    bias = bias_ref[...] + jnp.where(dead, NEG_INF, 0.0)

    q = q_ref[...]
    outs = [None] * ATT_HEADS
    for g in range(ATT_KV_HEADS):
        ks = slice(g * HEAD_DIM, (g + 1) * HEAD_DIM)
        kk = jnp.concatenate([kx_ref[:, ks], kp_ref[:, ks], kc_ref[:, ks], kn_ref[:, ks]], axis=0)
        vv = jnp.concatenate([vx_ref[:, ks], vp_ref[:, ks], vc_ref[:, ks], vn_ref[:, ks]], axis=0)
        heads = [g * ATT_GROUP + j for j in range(ATT_GROUP)]
        qq = jnp.concatenate([q[:, h * HEAD_DIM:(h + 1) * HEAD_DIM] for h in heads], axis=0)
        s = lax.dot_general(qq, kk, (((1,), (1,)), ((), ())), preferred_element_type=F32) + bias
        sink = jnp.concatenate([jnp.full((blk, 1), sink_ref[h], F32) for h in heads], axis=0)
        m = jnp.maximum(jnp.max(s, axis=-1, keepdims=True), sink)
        e = jnp.exp(s - m)
        den = jnp.sum(e, axis=-1, keepdims=True) + jnp.exp(sink - m)
        o = jnp.dot(e.astype(BF16), vv, preferred_element_type=F32) / den
        for j, h in enumerate(heads):
            outs[h] = o[j * blk:(j + 1) * blk, :]
    o_ref[...] = jnp.concatenate(outs, axis=1).astype(o_ref.dtype)


def _window_attention(qr, kr, vb, sinks, n_ctx, n_lat):
    n_rows = qr.shape[0]
    nb = n_rows // ATT_BLOCK
    q_spec = pl.BlockSpec((ATT_BLOCK, ATT_W), lambda i: (i, 0))
    prev = pl.BlockSpec((ATT_BLOCK, ATT_KV_W), lambda i: (jnp.maximum(i - 1, 0), 0))
    cur = pl.BlockSpec((ATT_BLOCK, ATT_KV_W), lambda i: (i, 0))
    nxt = pl.BlockSpec((ATT_BLOCK, ATT_KV_W), lambda i: (jnp.minimum(i + 1, nb - 1), 0))
    ctx = pl.BlockSpec((n_ctx, ATT_KV_W), lambda i: (0, 0))
    n_keys = n_ctx + 3 * ATT_BLOCK
    qi = jnp.arange(ATT_BLOCK)[:, None]
    ci = jnp.arange(n_keys)[None, :]
    in_window = (ci < n_ctx) | (jnp.abs(ci - n_ctx - ATT_BLOCK - qi) <= WINDOW)
    bias = jnp.tile(jnp.where(in_window, 0.0, NEG_INF).astype(F32), (ATT_GROUP, 1))
    return pl.pallas_call(
        functools.partial(_attn_kernel, n_ctx=n_ctx, n_blocks=nb),
        grid=(nb,),
        in_specs=[pl.BlockSpec(memory_space=pltpu.SMEM), pl.BlockSpec(bias.shape, lambda i: (0, 0)),
                  q_spec, prev, cur, nxt, prev, cur, nxt, ctx, ctx],
        out_specs=pl.BlockSpec((ATT_BLOCK, ATT_W), lambda i: (i, 0)),
        out_shape=jax.ShapeDtypeStruct((n_rows, ATT_W), BF16),
        compiler_params=_params(1),
        name="window_attention",
    )(sinks.astype(F32), bias, qr, kr, kr, kr, vb, vb, vb, kr, vb)


def _per_head(x, fn):
    return jnp.concatenate([fn(x[:, h * HEAD_DIM:(h + 1) * HEAD_DIM]) for h in range(RWKV_HEADS)], axis=1)


def _dot_nt(a, b, precision=None):
    return lax.dot_general(a, b, (((1,), (1,)), ((), ())), precision=precision, preferred_element_type=F32)


def _dot_tn(a, b, precision=None):
    return lax.dot_general(a, b, (((0,), (0,)), ((), ())), precision=precision, preferred_element_type=F32)


def _dot(a, b, precision=None):
    return jnp.dot(a, b, precision=precision, preferred_element_type=F32)


def _iclr(ad, ibase, iup):
    return jax.nn.sigmoid(ibase + _dot(ad.astype(BF16), iup.astype(BF16)))


def _block_diag(y, mask):
    return jnp.where(mask, jnp.concatenate([y] * HEADS_PER_GROUP, axis=0), jnp.zeros((), y.dtype))


def _rwkv_chunk_operands(d, r, k, v, lr, dbase_ref, dup_ref, ibase_ref, iup_ref, kk_scale, k_a):
    c = r.shape[0]
    wd = lr[:, d * DECAY_RANK:(d + 1) * DECAY_RANK]
    ad = lr[:, 2 * DECAY_RANK + d * ICLR_RANK:2 * DECAY_RANK + (d + 1) * ICLR_RANK]
    z = dbase_ref[d:d + 1, :] + _dot(jnp.tanh(wd).astype(BF16), dup_ref[d].astype(BF16))
    logw = -np.float32(np.exp(-0.5)) * jax.nn.sigmoid(z)
    a = _iclr(ad, ibase_ref[d:d + 1, :], iup_ref[d])
    kd = k * (1.0 + (a - 1.0) * k_a)
    b = kk_scale * a

    ti = lax.broadcasted_iota(jnp.int32, (c, c), 0)
    si = lax.broadcasted_iota(jnp.int32, (c, c), 1)
    tri = (si <= ti) if d == 0 else (si >= ti)
    cum = _dot(tri.astype(F32), logw, HIGHEST)
    last = c - 1 if d == 0 else 0
    cum_end = cum[last:last + 1, :]
    w_inv = jnp.exp(-cum)
    w_tail = jnp.exp(cum_end - cum)
    return dict(
        x=jnp.concatenate([(jnp.exp(cum - logw) * kk_scale).astype(BF16), (r * jnp.exp(cum)).astype(BF16)], axis=0),
        beta=(b * w_inv).astype(BF16),
        kappa=(kd * w_inv).astype(BF16),
        tail=jnp.concatenate([(kd * w_tail).astype(BF16), (-b * w_tail).astype(BF16)], axis=0),
        w_end=jnp.exp(cum_end),
        v=v.astype(BF16))


def _rwkv_chunk_update(ops, s_ref, y_refs, directions):
    c = RWKV_CHUNK
    gw = HEADS_PER_GROUP * HEAD_DIM
    n_groups = RWKV_HEADS // HEADS_PER_GROUP
    chains = [(d, g) for g in range(n_groups) for d in directions]
    tp = lax.broadcasted_iota(jnp.int32, (c, HEADS_PER_GROUP * c), 0)
    sp = lax.broadcasted_iota(jnp.int32, (c, HEADS_PER_GROUP * c), 1) % c
    before = [sp < tp, sp > tp]
    upto = [sp <= tp, sp >= tp]
    eye = (sp == tp).astype(F32)
    bi = lax.broadcasted_iota(jnp.int32, (gw, gw), 0) // HEAD_DIM
    bj = lax.broadcasted_iota(jnp.int32, (gw, gw), 1) // HEAD_DIM
    diag = bi == bj
    cols = lambda g: slice(g * gw, (g + 1) * gw)

    nb, tinv, rb, m2, sx, rhs, y_part = {}, {}, {}, {}, {}, {}, {}
    for ch in chains:
        d, g = ch
        x = ops[d]["x"][:, cols(g)]
        gram_b = _dot_nt(x, _block_diag(ops[d]["beta"][:, cols(g)], diag))
        gram_k = _dot_nt(x, _block_diag(ops[d]["kappa"][:, cols(g)], diag))
        sx[ch] = _dot_nt(x, s_ref[d, g].astype(BF16))
        npow = jnp.where(before[d], -gram_b[:c], 0.0)
        nb[ch] = npow.astype(BF16)
        tinv[ch] = eye + npow
        rb[ch] = jnp.where(upto[d], gram_b[c:], 0.0).astype(BF16)
        m2[ch] = jnp.concatenate([jnp.where(before[d], gram_k[:c], 0.0), jnp.where(upto[d], gram_k[c:], 0.0)],
                                 axis=0).astype(BF16)
    for ch in chains:
        d, g = ch
        kv = sx[ch] + _dot(m2[ch], _block_diag(ops[d]["v"][:, cols(g)], diag))
        rhs[ch] = kv[:c].astype(BF16)
        y_part[ch] = kv[c:]

    nbd = {ch: _block_diag(nb[ch], diag) for ch in chains}
    for _ in range(5):
        for ch in chains:
            nb[ch] = _dot(nb[ch], nbd[ch]).astype(BF16)
            nbd[ch] = _block_diag(nb[ch], diag)
        for ch in chains:
            tinv[ch] = tinv[ch] + _dot(tinv[ch].astype(BF16), nbd[ch])

    ub = {}
    for ch in chains:
        ub[ch] = _dot(tinv[ch].astype(BF16), _block_diag(rhs[ch], diag)).astype(BF16)
    for ch in chains:
        d, g = ch
        y_refs[d][:, cols(g)] = y_part[ch] - _dot(rb[ch], _block_diag(ub[ch], diag))
    for ch in chains:
        d, g = ch
        upd = _dot_tn(jnp.concatenate([ops[d]["v"][:, cols(g)], ub[ch]], axis=0), ops[d]["tail"][:, cols(g)])
        s_ref[d, g] = s_ref[d, g] * ops[d]["w_end"][:, cols(g)] + jnp.where(diag, upd, 0.0)


def _rwkv_scan_kernel(rf_ref, kf_ref, vf_ref, lf_ref, rb_ref, kb_ref, vb_ref, lb_ref,
                      dbase_ref, dup_ref, ibase_ref, iup_ref, kk_ref, ka_ref, yf_ref, yb_ref, s_ref):
    @pl.when(pl.program_id(0) == 0)
    def _():
        s_ref[...] = jnp.zeros_like(s_ref)

    k_k = kk_ref[...]
    k_a = ka_ref[...]
    ops = []
    for d, (r_ref, k_ref, v_ref, l_ref) in enumerate(
            ((rf_ref, kf_ref, vf_ref, lf_ref), (rb_ref, kb_ref, vb_ref, lb_ref))):
        k = k_ref[...]
        kk = k * k_k
        kk = _per_head(kk, lambda x: x / jnp.maximum(jnp.sqrt(jnp.sum(x * x, axis=-1, keepdims=True)), 1e-12))
        ops.append(_rwkv_chunk_operands(d, r_ref[...], k, v_ref[...], l_ref[...], dbase_ref, dup_ref, ibase_ref,
                                        iup_ref, kk, k_a))
    _rwkv_chunk_update(ops, s_ref, (yf_ref, yb_ref), (0, 1))


def _rwkv_scan(p, decay_base, decay_up, iclr_base, iclr_up, k_k, k_a, n_ctx):
    n_rows = p.shape[0]
    c = RWKV_CHUNK
    n_chunks = n_rows // c
    ctx_chunks = n_ctx // c

    def fwd(g):
        return g

    def bwd(g):
        return jnp.where(g < ctx_chunks, ctx_chunks - 1 - g, ctx_chunks + n_chunks - 1 - g)

    def specs(order):
        return [pl.BlockSpec((c, RWKV_W), lambda g: (order(g), COL_R // RWKV_W)),
                pl.BlockSpec((c, RWKV_W), lambda g: (order(g), COL_RK // RWKV_W)),
                pl.BlockSpec((c, RWKV_W), lambda g: (order(g), COL_RV // RWKV_W)),
                pl.BlockSpec((c, LR_W), lambda g: (order(g), COL_LR // LR_W))]

    full = lambda shape: pl.BlockSpec(shape, lambda g: (0,) * len(shape))
    return pl.pallas_call(
        _rwkv_scan_kernel,
        grid=(n_chunks,),
        in_specs=specs(fwd) + specs(bwd) + [
            full((2, RWKV_W)), full((2, DECAY_RANK, RWKV_W)), full((2, RWKV_W)), full((2, ICLR_RANK, RWKV_W)),
            full((1, RWKV_W)), full((1, RWKV_W))],
        out_specs=[pl.BlockSpec((c, RWKV_W), lambda g: (fwd(g), 0)),
                   pl.BlockSpec((c, RWKV_W), lambda g: (bwd(g), 0))],
        out_shape=[jax.ShapeDtypeStruct((n_rows, RWKV_W), F32)] * 2,
        scratch_shapes=[pltpu.VMEM((2, RWKV_HEADS // HEADS_PER_GROUP, HEADS_PER_GROUP * HEAD_DIM,
                                   HEADS_PER_GROUP * HEAD_DIM), F32)],
        compiler_params=_params(1),
        name="rwkv7_chunk_scan",
    )(p, p, p, p, p, p, p, p, decay_base, decay_up, iclr_base, iclr_up,
      k_k.reshape(1, RWKV_W), k_a.reshape(1, RWKV_W))


def _head_sums(x):
    gw = HEADS_PER_GROUP * HEAD_DIM
    bi = lax.broadcasted_iota(jnp.int32, (gw, gw), 0) // HEAD_DIM
    bj = lax.broadcasted_iota(jnp.int32, (gw, gw), 1) // HEAD_DIM
    ones = (bi == bj).astype(BF16)
    hi = x.astype(BF16)
    lo = (x - hi.astype(F32)).astype(BF16)
    parts = []
    for g in range(x.shape[1] // gw):
        gs = slice(g * gw, (g + 1) * gw)
        parts.append(_dot(hi[:, gs], ones) + _dot(lo[:, gs], ones))
    return jnp.concatenate(parts, axis=1)


def _rwkv_out_kernel(yf_ref, yb_ref, r_ref, k_ref, v_ref, lr_ref, ibase_ref, iup_ref, ka_ref, rk_ref,
                     lw_ref, lb_ref, gup_ref, o_ref):
    y = yf_ref[...] + yb_ref[...]
    r = r_ref[...]
    k = k_ref[...]
    v = v_ref[...]
    lr = lr_ref[...]
    k_a = ka_ref[...]

    yc = y - _head_sums(y) * (1.0 / HEAD_DIM)
    var = _head_sums(yc * yc) * (1.0 / HEAD_DIM)
    yn = yc * lax.rsqrt(var + RWKV_GN_EPS) * lw_ref[...] + lb_ref[...]
    kd_sum = jnp.zeros_like(k)
    for d in range(2):
        ad = lr[:, 2 * DECAY_RANK + d * ICLR_RANK:2 * DECAY_RANK + (d + 1) * ICLR_RANK]
        a = _iclr(ad, ibase_ref[d:d + 1, :], iup_ref[d])
        kd_sum = kd_sum + k * (1.0 + (a - 1.0) * k_a)
    rkk = r * kd_sum * rk_ref[...]
    bonus = _head_sums(rkk) * v
    gd = lr[:, 4 * DECAY_RANK:4 * DECAY_RANK + GATE_RANK]
    gate = _dot(jax.nn.sigmoid(gd).astype(BF16), gup_ref[...].astype(BF16))
    o_ref[...] = ((yn + bonus) * gate).astype(o_ref.dtype)


def _rwkv_out(yf, yb, p, iclr_base, iclr_up, k_a, r_k, lnx_w, lnx_b, gate_up):
    n_rows = p.shape[0]
    tm = 256
    row = lambda w, cb: pl.BlockSpec((tm, w), lambda i, cb=cb: (i, cb))
    full = lambda shape: pl.BlockSpec(shape, lambda i: (0,) * len(shape))
    vec = lambda t: t.reshape(1, RWKV_W)
    return pl.pallas_call(
        _rwkv_out_kernel,
        grid=(n_rows // tm,),
        in_specs=[row(RWKV_W, 0), row(RWKV_W, 0), row(RWKV_W, COL_R // RWKV_W), row(RWKV_W, COL_RK // RWKV_W),
                  row(RWKV_W, COL_RV // RWKV_W), row(LR_W, COL_LR // LR_W),
                  full((2, RWKV_W)), full((2, ICLR_RANK, RWKV_W)), full((1, RWKV_W)), full((1, RWKV_W)),
                  full((1, RWKV_W)), full((1, RWKV_W)), full((GATE_RANK, RWKV_W))],
        out_specs=pl.BlockSpec((tm, RWKV_W), lambda i: (i, 0)),
        out_shape=jax.ShapeDtypeStruct((n_rows, RWKV_W), BF16),
        compiler_params=_params(1),
        name="rwkv7_out",
    )(yf, yb, p, p, p, p, iclr_base, iclr_up, vec(k_a), vec(r_k), vec(lnx_w), vec(lnx_b), gate_up)


def _out_proj_kernel(a_ref, b_ref, c_ref, wa_ref, wb_ref, wc_ref, x_ref, gate_ref, o_ref, *, n_ctx, tm):
    acc = _dot(a_ref[...], wa_ref[...]) + _dot(b_ref[...], wb_ref[...]) + _dot(c_ref[...], wc_ref[...])
    row = pl.program_id(0) * tm + lax.broadcasted_iota(jnp.int32, (tm, 1), 0)
    gate = jnp.where(row < n_ctx, gate_ref[1:2, :], gate_ref[0:1, :])
    o_ref[...] = x_ref[...] + gate * acc


def _out_proj(a, b, c, w_out_bf16, xs, gate, n_ctx):
    n_rows, d = xs.shape
    tm = _row_tile(n_rows, (1280, 1024, 512, 256))
    tn = 512
    wa, wb, wc = w_out_bf16[:CONV_W], w_out_bf16[CONV_W:CONV_W + ATT_W], w_out_bf16[CONV_W + ATT_W:]
    return pl.pallas_call(
        functools.partial(_out_proj_kernel, n_ctx=n_ctx, tm=tm),
        grid=(n_rows // tm, d // tn),
        in_specs=[pl.BlockSpec((tm, CONV_W), lambda i, j: (i, 0)),
                  pl.BlockSpec((tm, ATT_W), lambda i, j: (i, 0)),
                  pl.BlockSpec((tm, RWKV_W), lambda i, j: (i, 0)),
                  pl.BlockSpec((CONV_W, tn), lambda i, j: (0, j)),
                  pl.BlockSpec((ATT_W, tn), lambda i, j: (0, j)),
                  pl.BlockSpec((RWKV_W, tn), lambda i, j: (0, j)),
                  pl.BlockSpec((tm, tn), lambda i, j: (i, j)),
                  pl.BlockSpec((8, tn), lambda i, j: (0, j))],
        out_specs=pl.BlockSpec((tm, tn), lambda i, j: (i, j)),
        out_shape=jax.ShapeDtypeStruct((n_rows, d), F32),
        compiler_params=_params(2),
        name="out_proj_residual",
    )(a, b, c, wa, wb, wc, xs, gate)


def _first_max(x, idx):
    m = jnp.max(x, axis=0, keepdims=True)
    first = jnp.min(jnp.where(x == m, idx, N_EXPERTS), axis=0, keepdims=True)
    return m, first


def _router_kernel(x_ref, g_ref, mod_ref, rw_ref, rb_ref, h_ref, idx_ref, gate_ref, rank_ref, cnt_ref, slots_ref,
                   base_ref, *, n_ctx, tm):
    @pl.when(pl.program_id(0) == 0)
    def _():
        base_ref[...] = jnp.zeros_like(base_ref)

    slots_ref[...] = jnp.zeros_like(slots_ref)

    h = _norm_mod(x_ref[...], g_ref[...], mod_ref[...], pl.program_id(0) * tm, n_ctx)
    h_ref[...] = h
    logits = _dot_nt(rw_ref[...], h, HIGHEST)
    scores = jax.nn.sigmoid(logits)
    sel = scores + rb_ref[...]
    eidx = lax.broadcasted_iota(jnp.int32, sel.shape, 0)
    best = best_score = None
    for g in range(N_EXPERT_GROUPS):
        rows = slice(g * EXPERTS_PER_GROUP, (g + 1) * EXPERTS_PER_GROUP)
        x = sel[rows]
        xi = g * EXPERTS_PER_GROUP + lax.broadcasted_iota(jnp.int32, x.shape, 0)
        m1, i1 = _first_max(x, xi)
        m2, _ = _first_max(jnp.where(xi == i1, -jnp.inf, x), xi)
        score = m1 + m2
        if g == 0:
            best, best_score = jnp.zeros_like(i1), score
        else:
            better = score > best_score
            best = jnp.where(better, g, best)
            best_score = jnp.where(better, score, best_score)
    masked = jnp.where(eidx // EXPERTS_PER_GROUP == best, sel, NEG_INF)
    _, e1 = _first_max(masked, eidx)
    _, e2 = _first_max(jnp.where(eidx == e1, -jnp.inf, masked), eidx)
    g1 = jnp.sum(jnp.where(eidx == e1, scores, 0.0), axis=0, keepdims=True)
    g2 = jnp.sum(jnp.where(eidx == e2, scores, 0.0), axis=0, keepdims=True)
    idx_ref[0:1, :] = e1
    idx_ref[1:2, :] = e2
    gate_ref[0:1, :] = g1 / (g1 + g2)
    gate_ref[1:2, :] = g2 / (g1 + g2)

    si = lax.broadcasted_iota(jnp.int32, (tm, tm), 0)
    ti = lax.broadcasted_iota(jnp.int32, (tm, tm), 1)
    prefix = (si <= ti).astype(BF16)
    base = base_ref[...]
    for kth, e in enumerate((e1, e2)):
        hit = eidx == e
        seen = _dot(hit.astype(BF16), prefix)
        rank = jnp.sum(jnp.where(hit, seen - 1.0 + base, 0.0), axis=0, keepdims=True)
        rank_ref[kth:kth + 1, :] = rank.astype(jnp.int32)
        base = base + seen[:, tm - 1:tm]
    base_ref[...] = base
    cnt_ref[...] = base.astype(jnp.int32)


def _route(xs, g, mod, router_w, router_b, n_ctx, cap):
    n_rows, d = xs.shape
    tm = 256
    n_tiles = n_rows // tm
    slab = -(-cap // (n_tiles * SUBLANES)) * SUBLANES
    return pl.pallas_call(
        functools.partial(_router_kernel, n_ctx=n_ctx, tm=tm),
        grid=(n_rows // tm,),
        in_specs=[pl.BlockSpec((tm, d), lambda i: (i, 0)),
                  pl.BlockSpec((1, d), lambda i: (0, 0)),
                  pl.BlockSpec((8, d), lambda i: (0, 0)),
                  pl.BlockSpec((N_EXPERTS, d), lambda i: (0, 0)),
                  pl.BlockSpec((N_EXPERTS, 1), lambda i: (0, 0))],
        out_specs=[pl.BlockSpec((tm, d), lambda i: (i, 0)),
                   pl.BlockSpec((TOP_K, tm), lambda i: (0, i)),
                   pl.BlockSpec((TOP_K, tm), lambda i: (0, i)),
                   pl.BlockSpec((TOP_K, tm), lambda i: (0, i)),
                   pl.BlockSpec((N_EXPERTS, 1), lambda i: (0, 0)),
                   pl.BlockSpec((slab, d), lambda i: (i, 0))],
        out_shape=[jax.ShapeDtypeStruct((n_rows, d), F32),
                   jax.ShapeDtypeStruct((TOP_K, n_rows), jnp.int32),
                   jax.ShapeDtypeStruct((TOP_K, n_rows), F32),
                   jax.ShapeDtypeStruct((TOP_K, n_rows), jnp.int32),
                   jax.ShapeDtypeStruct((N_EXPERTS, 1), jnp.int32),
                   jax.ShapeDtypeStruct((n_tiles * slab, d), F32)],
        scratch_shapes=[pltpu.VMEM((N_EXPERTS, 1), F32)],
        compiler_params=_params(1),
        name="moe_norm_route",
    )(xs, g.reshape(1, d), mod, router_w.T, router_b.reshape(N_EXPERTS, 1))


def _row_copy(src_ref, src_row, dst_ref, dst_row, sem):
    return pltpu.make_async_copy(src_ref.at[pl.ds(src_row, 1), :], dst_ref.at[pl.ds(dst_row, 1), :], sem)


def _dispatch_kernel(dest_ref, h_ref, init_ref, xg_ref, sem, *, tm):
    del init_ref

    def start(t, carry):
        for k in range(TOP_K):
            _row_copy(h_ref, t, xg_ref, dest_ref[0, k, t], sem).start()
        return carry

    lax.fori_loop(0, tm, start, 0, unroll=DMA_UNROLL)
    for _ in range(TOP_K):
        pltpu.make_async_copy(h_ref, xg_ref.at[pl.ds(0, tm), :], sem).wait()


def _dispatch(hp, dest_tiles, slots, tm):
    n_rows, w = hp.shape
    return pl.pallas_call(
        functools.partial(_dispatch_kernel, tm=tm),
        grid=(n_rows // tm,),
        in_specs=[pl.BlockSpec((1, TOP_K, tm), lambda i: (i, 0, 0), memory_space=pltpu.SMEM),
                  pl.BlockSpec((tm, w), lambda i: (i, 0)),
                  pl.BlockSpec(memory_space=pl.ANY)],
        out_specs=pl.BlockSpec(memory_space=pl.ANY),
        out_shape=jax.ShapeDtypeStruct(slots.shape, slots.dtype),
        scratch_shapes=[pltpu.SemaphoreType.DMA(())],
        input_output_aliases={2: 0},
        compiler_params=_params(1),
        name="moe_dispatch_rows",
    )(dest_tiles, hp, slots)


def _expert_kernel(be_ref, nused_ref, x_ref, wg_ref, wu_ref, wd_ref, o_ref, wgb_ref, wub_ref, wdb_ref):
    i = pl.program_id(0)
    used = i < nused_ref[0]
    new_expert = (i == 0) | (be_ref[i] != be_ref[jnp.maximum(i - 1, 0)])

    @pl.when(used & new_expert)
    def _():
        for r0 in range(0, D_MODEL, CAST_ROWS):
            rows = slice(r0, r0 + CAST_ROWS)
            wgb_ref[rows, :] = wg_ref[0, rows, :].astype(BF16)
            wub_ref[rows, :] = wu_ref[0, rows, :].astype(BF16)
        for r0 in range(0, D_EXPERT, CAST_ROWS):
            rows = slice(r0, r0 + CAST_ROWS)
            wdb_ref[rows, :] = wd_ref[0, rows, :].astype(BF16)

    @pl.when(used)
    def _():
        x = x_ref[...].astype(BF16)
        gate = _dot(x, wgb_ref[...])
        up = _dot(x, wub_ref[...])
        act = (gate * jax.nn.sigmoid(gate) * up).astype(BF16)
        o_ref[...] = _dot(act, wdb_ref[...])

    @pl.when(i >= nused_ref[0])
    def _():
        o_ref[...] = jnp.zeros_like(o_ref)


def _expert_ffn(xg, block_e, n_used, wg, wu, wd, layer, cap, tm):
    d = xg.shape[1]
    grid_spec = pltpu.PrefetchScalarGridSpec(
        num_scalar_prefetch=2,
        grid=(cap // tm,),
        in_specs=[pl.BlockSpec((tm, d), lambda i, be, nu: (i, 0)),
                  pl.BlockSpec((None, 1, d, D_EXPERT), lambda i, be, nu: (layer, be[i], 0, 0)),
                  pl.BlockSpec((None, 1, d, D_EXPERT), lambda i, be, nu: (layer, be[i], 0, 0)),
                  pl.BlockSpec((None, 1, D_EXPERT, d), lambda i, be, nu: (layer, be[i], 0, 0))],
        out_specs=pl.BlockSpec((tm, d), lambda i, be, nu: (i, 0)),
        scratch_shapes=[pltpu.VMEM((d, D_EXPERT), BF16), pltpu.VMEM((d, D_EXPERT), BF16),
                        pltpu.VMEM((D_EXPERT, d), BF16)],
    )
    return pl.pallas_call(
        _expert_kernel,
        grid_spec=grid_spec,
        out_shape=jax.ShapeDtypeStruct((cap, d), F32),
        compiler_params=pltpu.CompilerParams(dimension_semantics=("arbitrary",),
                                             vmem_limit_bytes=EXPERT_VMEM_LIMIT),
        name="moe_expert_ffn",
    )(block_e, n_used, xg, wg, wu, wd)


def _combine_kernel(dest_ref, x_ref, gt_ref, g2_ref, fg_ref, yb_ref, o_ref, ybuf_ref, sem,
                    *, n_ctx, tm, final_norm, first_tile):
    def start(t, carry):
        for k in range(TOP_K):
            _row_copy(yb_ref, dest_ref[0, k, t], ybuf_ref.at[k], t, sem).start()
        return carry

    lax.fori_loop(0, tm, start, 0, unroll=DMA_UNROLL)
    for k in range(TOP_K):
        pltpu.make_async_copy(yb_ref.at[pl.ds(0, tm), :], ybuf_ref.at[k], sem).wait()

    is_ctx = (pl.program_id(0) + first_tile) * tm < n_ctx
    g2 = jnp.where(is_ctx, g2_ref[1:2, :], g2_ref[0:1, :])
    gt = gt_ref[...]
    y = gt[:, 0:1] * ybuf_ref[0] + gt[:, 1:2] * ybuf_ref[1]
    x = x_ref[...] + g2 * y
    if final_norm:
        ms = jnp.mean(x * x, axis=-1, keepdims=True)
        x = x * lax.rsqrt(ms + RMS_EPS) * fg_ref[...]
    o_ref[...] = x


def _combine(xs, yb, dest_tiles, gates_t, g2, final_g, n_ctx, final_norm, tm):
    n_rows, d = xs.shape
    skip = n_ctx // tm if final_norm else 0
    return pl.pallas_call(
        functools.partial(_combine_kernel, n_ctx=n_ctx, tm=tm, final_norm=final_norm, first_tile=skip),
        grid=(n_rows // tm - skip,),
        in_specs=[pl.BlockSpec((1, TOP_K, tm), lambda i: (i + skip, 0, 0), memory_space=pltpu.SMEM),
                  pl.BlockSpec((tm, d), lambda i: (i + skip, 0)),
                  pl.BlockSpec((tm, TOP_K), lambda i: (i + skip, 0)),
                  pl.BlockSpec((8, d), lambda i: (0, 0)), pl.BlockSpec((1, d), lambda i: (0, 0)),
                  pl.BlockSpec(memory_space=pl.ANY)],
        out_specs=pl.BlockSpec((tm, d), lambda i: (i, 0)),
        out_shape=jax.ShapeDtypeStruct((n_rows - skip * tm, d), F32),
        scratch_shapes=[pltpu.VMEM((TOP_K, tm, yb.shape[1]), yb.dtype), pltpu.SemaphoreType.DMA(())],
        compiler_params=_params(1),
        name="moe_combine_residual",
    )(dest_tiles, xs, gates_t, g2, final_g.reshape(1, d), yb)


def _moe(xs, g, mod, g2, router_w, router_b, wg, wu, wd, layer, final_g, n_ctx, final_norm):
    n_rows, d = xs.shape
    tm = 256
    n_asg = n_rows * TOP_K
    n_blk = n_asg // tm + N_EXPERTS
    cap = n_blk * tm
    h, idx, gates, rank, counts, slots = _route(xs, g, mod, router_w, router_b, n_ctx, cap)
    counts = counts[:, 0]
    padded = (counts + tm - 1) // tm * tm
    pends = jnp.cumsum(padded)
    pstarts = pends - padded
    seg_start = jnp.sum(jnp.where(idx[..., None] == jnp.arange(N_EXPERTS), pstarts, 0), axis=-1)
    dest = seg_start + rank
    dest_tiles = dest.reshape(TOP_K, n_rows // tm, tm).transpose(1, 0, 2).astype(jnp.int32)
    block_e = jnp.minimum(jnp.sum(pends[None, :] <= (jnp.arange(n_blk) * tm)[:, None], axis=1), N_EXPERTS - 1)
    n_used = (pends[-1] // tm).reshape(1)
    xg = _dispatch(h, dest_tiles, slots, tm)
    yb = _expert_ffn(xg, block_e.astype(jnp.int32), n_used.astype(jnp.int32), wg, wu, wd, layer, cap, tm)
    return _combine(xs, yb, dest_tiles, gates.T, g2, final_g, n_ctx, final_norm, tm)


def _mod_rows(mod_l, lat_chunks, ctx_chunks):
    d = D_MODEL
    rows = [mod_l[0, c * d:(c + 1) * d] for c in lat_chunks] + [mod_l[1, c * d:(c + 1) * d] for c in ctx_chunks]
    out = jnp.zeros((8, d), F32)
    return out.at[:len(rows)].set(jnp.stack(rows))


def kernel(x, c, ctx, c_ctx, ada_w, ada_b, norm1_g, norm2_g, w_in, w_out, conv_w, attn_sinks, decay_base,
           decay_up, iclr_base, iclr_up, gate_up, k_k, k_a, r_k, lnx_w, lnx_b, router_w, router_b,
           expert_gate, expert_up, expert_down, final_norm_g):
    bsz, n_lat, d = x.shape
    n_ctx = ctx.shape[1]
    depth = ada_w.shape[0]
    assert bsz == 1 and d == D_MODEL and n_ctx % NORM_ROWS == 0 and n_lat % NORM_ROWS == 0

    xs = jnp.concatenate([ctx[0], x[0]], axis=0)
    mods = _ada_mod(c, c_ctx, ada_w, ada_b)
    cos, sin = _rope_tables(n_ctx, n_lat)

    for l in range(depth):
        last = l == depth - 1
        mod1 = _mod_rows(mods[l], (0, 1), (0, 1))
        gate1 = _mod_rows(mods[l], (2,), (2,))
        mod2 = _mod_rows(mods[l], (3, 4), (3, 4))
        gate2 = _mod_rows(mods[l], (5,), (5,))

        p = _norm_mod_matmul(xs, norm1_g[l], mod1, _in_proj_weights(w_in[l]), n_ctx)

        a_mix = _short_conv(p, conv_w[l], n_ctx)
        qr, kr, vb = _rope_qkv(p, cos, sin)
        b_mix = _window_attention(qr, kr, vb, attn_sinks[l], n_ctx, n_lat)
        yf, yb = _rwkv_scan(p, decay_base[l], decay_up[l], iclr_base[l], iclr_up[l], k_k[l], k_a[l], n_ctx)
        c_mix = _rwkv_out(yf, yb, p, iclr_base[l], iclr_up[l], k_a[l], r_k[l].reshape(-1), lnx_w[l], lnx_b[l],
                          gate_up[l])
        xs = _out_proj(a_mix, b_mix, c_mix, w_out[l].astype(BF16), xs, gate1, n_ctx)

        xs = _moe(xs, norm2_g[l], mod2, gate2, router_w, router_b, expert_gate, expert_up, expert_down, l,
                  final_norm_g, n_ctx, last)
    return xs.reshape(bsz, n_lat, d)
```

```python
import functools

import numpy as np
import jax
import jax.numpy as jnp
from jax import lax
from jax.experimental import pallas as pl
from jax.experimental.pallas import tpu as pltpu

F32 = jnp.float32
BF16 = jnp.bfloat16
HIGHEST = lax.Precision.HIGHEST

D_MODEL = 2048
GRID_W = 64
CONV_W = D_MODEL // 4
CONV_K = 3
HEAD_DIM = 64
ATT_HEADS = 12
ATT_KV_HEADS = 4
ATT_GROUP = ATT_HEADS // ATT_KV_HEADS
ATT_W = ATT_HEADS * HEAD_DIM
ATT_KV_W = ATT_KV_HEADS * HEAD_DIM
RWKV_HEADS = 12
RWKV_W = RWKV_HEADS * HEAD_DIM
WINDOW = 128
ATT_BLOCK = 128
ROPE_THETA = 10000.0
ROPE_FREQS = HEAD_DIM // 4
DECAY_RANK = 64
ICLR_RANK = 64
GATE_RANK = 128
RWKV_GN_EPS = 64e-5
N_EXPERTS = 32
N_EXPERT_GROUPS = 4
EXPERTS_PER_GROUP = N_EXPERTS // N_EXPERT_GROUPS
TOP_K = 2
D_EXPERT = 768
RMS_EPS = 1e-6
NEG_INF = -1e30

COL_Q = 0
COL_R = 768
COL_RK = 1536
COL_RV = 2304
COL_CB = 3072
COL_CC = 3584
COL_CH = 4096
COL_K = 4608
COL_V = 4864
COL_LR = 5120
LR_W = 512
IN_W_PAD = 5632

RWKV_CHUNK = 64
HEADS_PER_GROUP = 4
NORM_ROWS = 256
SUBLANES = 8
CAST_ROWS = 256
DMA_UNROLL = 8
VMEM_LIMIT = 56 * 1024 * 1024
EXPERT_VMEM_LIMIT = 62 * 1024 * 1024


def _params(n_axes):
    return pltpu.CompilerParams(dimension_semantics=("arbitrary",) * n_axes,
                                vmem_limit_bytes=VMEM_LIMIT)


def _row_tile(n_rows, candidates):
    for t in candidates:
        if n_rows % t == 0:
            return t
    raise ValueError(f"no row tile for {n_rows}")


def _in_proj_weights(w):
    o_cb, o_cc, o_ch = 0, 512, 1024
    o_q, o_k, o_v = 1536, 2304, 2560
    o_r, o_rk, o_rv = 2816, 3584, 4352
    o_lr = 5120
    segs = [(o_q, 768), (o_r, 768), (o_rk, 768), (o_rv, 768), (o_cb, 512), (o_cc, 512), (o_ch, 512),
            (o_k, 256), (o_v, 256), (o_lr, 384)]
    parts = [w[:, o:o + n].astype(BF16) for o, n in segs]
    parts.append(jnp.zeros((w.shape[0], IN_W_PAD - sum(n for _, n in segs)), BF16))
    return jnp.concatenate(parts, axis=1)


def _ada_kernel(s_ref, w_ref, b_ref, o_ref):
    o_ref[0] = jnp.dot(s_ref[...], w_ref[0], precision=HIGHEST, preferred_element_type=F32) + b_ref[0]


def _ada_mod(c, c_ctx, ada_w, ada_b):
    depth, d, n = ada_w.shape
    s = jnp.zeros((8, d), F32).at[0].set(c[0]).at[1].set(c_ctx)
    s = s * jax.nn.sigmoid(s)
    tn = 2048
    return pl.pallas_call(
        _ada_kernel,
        grid=(depth, n // tn),
        in_specs=[pl.BlockSpec((8, d), lambda l, j: (0, 0)),
                  pl.BlockSpec((1, d, tn), lambda l, j: (l, 0, j)),
                  pl.BlockSpec((1, 1, tn), lambda l, j: (l, 0, j))],
        out_specs=pl.BlockSpec((1, 8, tn), lambda l, j: (l, 0, j)),
        out_shape=jax.ShapeDtypeStruct((depth, 8, n), F32),
        compiler_params=_params(2),
        name="ada_mod",
    )(s, ada_w, ada_b.reshape(depth, 1, n))


def _norm_mod(x, g, mod, row0, n_ctx):
    ms = jnp.mean(x * x, axis=-1, keepdims=True)
    y = x * lax.rsqrt(ms + RMS_EPS) * g
    is_ctx = row0 < n_ctx
    shift = jnp.where(is_ctx, mod[2:3, :], mod[0:1, :])
    scale = jnp.where(is_ctx, mod[3:4, :], mod[1:2, :])
    return y * (1.0 + scale) + shift


def _nmm_kernel(x_ref, g_ref, mod_ref, w_ref, o_ref, h_ref, *, n_ctx, tm):
    i = pl.program_id(0)

    @pl.when(pl.program_id(1) == 0)
    def _():
        for r0 in range(0, tm, NORM_ROWS):
            rows = slice(r0, r0 + NORM_ROWS)
            h_ref[rows, :] = _norm_mod(x_ref[rows, :], g_ref[...], mod_ref[...], i * tm + r0, n_ctx).astype(BF16)

    o_ref[...] = jnp.dot(h_ref[...], w_ref[...], preferred_element_type=F32)


def _norm_mod_matmul(xs, g, mod, w_bf16, n_ctx):
    n_rows, d = xs.shape
    n_out = w_bf16.shape[1]
    tm = _row_tile(n_rows, (1280, 1024, 512, 256))
    tn = 512
    return pl.pallas_call(
        functools.partial(_nmm_kernel, n_ctx=n_ctx, tm=tm),
        grid=(n_rows // tm, n_out // tn),
        in_specs=[pl.BlockSpec((tm, d), lambda i, j: (i, 0)),
                  pl.BlockSpec((1, d), lambda i, j: (0, 0)),
                  pl.BlockSpec((8, d), lambda i, j: (0, 0)),
                  pl.BlockSpec((d, tn), lambda i, j: (0, j))],
        out_specs=pl.BlockSpec((tm, tn), lambda i, j: (i, j)),
        out_shape=jax.ShapeDtypeStruct((n_rows, n_out), F32),
        scratch_shapes=[pltpu.VMEM((tm, d), BF16)],
        compiler_params=_params(2),
        name="norm_mod_in_proj",
    )(xs, g.reshape(1, d), mod, w_bf16)


def _conv_kernel(cb_ref, cc_ref, ch_ref, ccp_ref, chp_ref, ccn_ref, chn_ref, w_ref, o_ref, *, n_ctx, n_rows, tm):
    i = pl.program_id(0)
    u = cc_ref[...] * ch_ref[...]
    u_prev_row = ccp_ref[7:8, :] * chp_ref[7:8, :]
    u_next_row = ccn_ref[0:1, :] * chn_ref[0:1, :]
    loc = lax.broadcasted_iota(jnp.int32, (tm, 1), 0)
    row = i * tm + loc
    up = jnp.where(loc == 0, u_prev_row, pltpu.roll(u, 1, axis=0))
    dn = jnp.where(loc == tm - 1, u_next_row, pltpu.roll(u, tm - 1, axis=0))
    up = jnp.where((row == 0) | (row == n_ctx), 0.0, up)
    dn = jnp.where((row == n_ctx - 1) | (row == n_rows - 1), 0.0, dn)
    w = w_ref[...]
    y = w[0:1, :] * up + w[1:2, :] * u + w[2:3, :] * dn
    o_ref[...] = (cb_ref[...] * y).astype(o_ref.dtype)


def _short_conv(p, conv_w, n_ctx):
    n_rows = p.shape[0]
    tm = _row_tile(n_rows, (1280, 1024, 512, 256))
    r8 = tm // 8
    last8 = n_rows // 8 - 1
    wpad = jnp.zeros((8, CONV_W), F32).at[:CONV_K].set(conv_w)
    blk = lambda c: pl.BlockSpec((tm, CONV_W), lambda i, c=c: (i, c))
    prev = lambda c: pl.BlockSpec((8, CONV_W), lambda i, c=c: (jnp.maximum(i * r8 - 1, 0), c))
    nxt = lambda c: pl.BlockSpec((8, CONV_W), lambda i, c=c: (jnp.minimum((i + 1) * r8, last8), c))
    cb, cc, ch = COL_CB // CONV_W, COL_CC // CONV_W, COL_CH // CONV_W
    return pl.pallas_call(
        functools.partial(_conv_kernel, n_ctx=n_ctx, n_rows=n_rows, tm=tm),
        grid=(n_rows // tm,),
        in_specs=[blk(cb), blk(cc), blk(ch), prev(cc), prev(ch), nxt(cc), nxt(ch),
                  pl.BlockSpec((8, CONV_W), lambda i: (0, 0))],
        out_specs=pl.BlockSpec((tm, CONV_W), lambda i: (i, 0)),
        out_shape=jax.ShapeDtypeStruct((n_rows, CONV_W), BF16),
        compiler_params=_params(1),
        name="short_conv",
    )(p, p, p, p, p, p, p, wpad)


def _swap_halves(x):
    n = x.shape[1]
    lane = lax.broadcasted_iota(jnp.int32, x.shape, 1)
    fwd = pltpu.roll(x, n - ROPE_FREQS, axis=1)
    bwd = pltpu.roll(x, ROPE_FREQS, axis=1)
    return jnp.where((lane % (2 * ROPE_FREQS)) < ROPE_FREQS, fwd, bwd)


def _rope_kernel(q_ref, k_ref, v_ref, cos_ref, sin_ref, qo_ref, ko_ref, vo_ref):
    cos = cos_ref[...]
    sin = sin_ref[...]
    q = q_ref[...]
    k = k_ref[...]
    cos_q = jnp.concatenate([cos] * (ATT_W // 128), axis=1)
    sin_q = jnp.concatenate([sin] * (ATT_W // 128), axis=1)
    cos_k = jnp.concatenate([cos] * (ATT_KV_W // 128), axis=1)
    sin_k = jnp.concatenate([sin] * (ATT_KV_W // 128), axis=1)
    qo_ref[...] = ((q * cos_q + _swap_halves(q) * sin_q) * (HEAD_DIM ** -0.5)).astype(BF16)
    ko_ref[...] = (k * cos_k + _swap_halves(k) * sin_k).astype(BF16)
    vo_ref[...] = v_ref[...].astype(BF16)


def _rope_tables(n_ctx, n_lat):
    row = jnp.repeat(jnp.arange(n_lat // GRID_W, dtype=jnp.int32), GRID_W)
    col = jnp.arange(n_lat, dtype=jnp.int32) % GRID_W
    inv = ROPE_THETA ** (-jnp.arange(ROPE_FREQS, dtype=F32) / ROPE_FREQS)
    ang_r = row[:, None].astype(F32) * inv[None, :]
    ang_c = col[:, None].astype(F32) * inv[None, :]
    cr, sr, cc, sc = jnp.cos(ang_r), jnp.sin(ang_r), jnp.cos(ang_c), jnp.sin(ang_c)
    cos = jnp.concatenate([cr, cr, cc, cc], axis=1)
    sin = jnp.concatenate([-sr, sr, -sc, sc], axis=1)
    cos = jnp.concatenate([jnp.ones((n_ctx, HEAD_DIM), F32), cos], axis=0)
    sin = jnp.concatenate([jnp.zeros((n_ctx, HEAD_DIM), F32), sin], axis=0)
    return jnp.tile(cos, (1, 2)), jnp.tile(sin, (1, 2))


def _rope_qkv(p, cos, sin):
    n_rows = p.shape[0]
    tm = _row_tile(n_rows, (1280, 1024, 512, 256))
    return pl.pallas_call(
        _rope_kernel,
        grid=(n_rows // tm,),
        in_specs=[pl.BlockSpec((tm, ATT_W), lambda i: (i, COL_Q // ATT_W)),
                  pl.BlockSpec((tm, ATT_KV_W), lambda i: (i, COL_K // ATT_KV_W)),
                  pl.BlockSpec((tm, ATT_KV_W), lambda i: (i, COL_V // ATT_KV_W)),
                  pl.BlockSpec((tm, 128), lambda i: (i, 0)),
                  pl.BlockSpec((tm, 128), lambda i: (i, 0))],
        out_specs=[pl.BlockSpec((tm, ATT_W), lambda i: (i, 0)),
                   pl.BlockSpec((tm, ATT_KV_W), lambda i: (i, 0)),
                   pl.BlockSpec((tm, ATT_KV_W), lambda i: (i, 0))],
        out_shape=[jax.ShapeDtypeStruct((n_rows, ATT_W), BF16),
                   jax.ShapeDtypeStruct((n_rows, ATT_KV_W), BF16),
                   jax.ShapeDtypeStruct((n_rows, ATT_KV_W), BF16)],
        compiler_params=_params(1),
        name="rope_qkv",
    )(p, p, p, cos, sin)


def _attn_kernel(sink_ref, bias_ref, q_ref, kp_ref, kc_ref, kn_ref, vp_ref, vc_ref, vn_ref, kx_ref, vx_ref, o_ref,
                 *, n_ctx, n_blocks):
    i = pl.program_id(0)
    blk = ATT_BLOCK
    ctx_blocks = n_ctx // blk
    ci = lax.broadcasted_iota(jnp.int32, (1, n_ctx + 3 * blk), 1)
    no_prev = i <= ctx_blocks
    no_cur = i < ctx_blocks
    no_next = no_cur | (i == n_blocks - 1)
    dead = (((ci >= n_ctx) & (ci < n_ctx + blk) & no_prev)
            | ((ci >= n_ctx + blk) & (ci < n_ctx + 2 * blk) & no_cur)
            | ((ci >= n_ctx + 2 * blk) & no_next))
    bias = bias_ref[...] + jnp.where(dead, NEG_INF, 0.0)

    q = q_ref[...]
    outs = [None] * ATT_HEADS
    for g in range(ATT_KV_HEADS):
        ks = slice(g * HEAD_DIM, (g + 1) * HEAD_DIM)
        kk = jnp.concatenate([kx_ref[:, ks], kp_ref[:, ks], kc_ref[:, ks], kn_ref[:, ks]], axis=0)
        vv = jnp.concatenate([vx_ref[:, ks], vp_ref[:, ks], vc_ref[:, ks], vn_ref[:, ks]], axis=0)
        heads = [g * ATT_GROUP + j for j in range(ATT_GROUP)]
        qq = jnp.concatenate([q[:, h * HEAD_DIM:(h + 1) * HEAD_DIM] for h in heads], axis=0)
        s = lax.dot_general(qq, kk, (((1,), (1,)), ((), ())), preferred_element_type=F32) + bias
        sink = jnp.concatenate([jnp.full((blk, 1), sink_ref[h], F32) for h in heads], axis=0)
        m = jnp.maximum(jnp.max(s, axis=-1, keepdims=True), sink)
        e = jnp.exp(s - m)
        den = jnp.sum(e, axis=-1, keepdims=True) + jnp.exp(sink - m)
        o = jnp.dot(e.astype(BF16), vv, preferred_element_type=F32) / den
        for j, h in enumerate(heads):
            outs[h] = o[j * blk:(j + 1) * blk, :]
    o_ref[...] = jnp.concatenate(outs, axis=1).astype(o_ref.dtype)


def _window_attention(qr, kr, vb, sinks, n_ctx, n_lat):
    n_rows = qr.shape[0]
    nb = n_rows // ATT_BLOCK
    q_spec = pl.BlockSpec((ATT_BLOCK, ATT_W), lambda i: (i, 0))
    prev = pl.BlockSpec((ATT_BLOCK, ATT_KV_W), lambda i: (jnp.maximum(i - 1, 0), 0))
    cur = pl.BlockSpec((ATT_BLOCK, ATT_KV_W), lambda i: (i, 0))
    nxt = pl.BlockSpec((ATT_BLOCK, ATT_KV_W), lambda i: (jnp.minimum(i + 1, nb - 1), 0))
    ctx = pl.BlockSpec((n_ctx, ATT_KV_W), lambda i: (0, 0))
    n_keys = n_ctx + 3 * ATT_BLOCK
    qi = jnp.arange(ATT_BLOCK)[:, None]
    ci = jnp.arange(n_keys)[None, :]
    in_window = (ci < n_ctx) | (jnp.abs(ci - n_ctx - ATT_BLOCK - qi) <= WINDOW)
    bias = jnp.tile(jnp.where(in_window, 0.0, NEG_INF).astype(F32), (ATT_GROUP, 1))
    return pl.pallas_call(
        functools.partial(_attn_kernel, n_ctx=n_ctx, n_blocks=nb),
        grid=(nb,),
        in_specs=[pl.BlockSpec(memory_space=pltpu.SMEM), pl.BlockSpec(bias.shape, lambda i: (0, 0)),
                  q_spec, prev, cur, nxt, prev, cur, nxt, ctx, ctx],
        out_specs=pl.BlockSpec((ATT_BLOCK, ATT_W), lambda i: (i, 0)),
        out_shape=jax.ShapeDtypeStruct((n_rows, ATT_W), BF16),
        compiler_params=_params(1),
        name="window_attention",
    )(sinks.astype(F32), bias, qr, kr, kr, kr, vb, vb, vb, kr, vb)


def _per_head(x, fn):
    return jnp.concatenate([fn(x[:, h * HEAD_DIM:(h + 1) * HEAD_DIM]) for h in range(RWKV_HEADS)], axis=1)


def _dot_nt(a, b, precision=None):
    return lax.dot_general(a, b, (((1,), (1,)), ((), ())), precision=precision, preferred_element_type=F32)


def _dot_tn(a, b, precision=None):
    return lax.dot_general(a, b, (((0,), (0,)), ((), ())), precision=precision, preferred_element_type=F32)


def _dot(a, b, precision=None):
    return jnp.dot(a, b, precision=precision, preferred_element_type=F32)


def _iclr(ad, ibase, iup):
    return jax.nn.sigmoid(ibase + _dot(ad.astype(BF16), iup.astype(BF16)))


def _block_diag(y, mask):
    return jnp.where(mask, jnp.concatenate([y] * HEADS_PER_GROUP, axis=0), jnp.zeros((), y.dtype))


def _rwkv_chunk_operands(d, r, k, v, lr, dbase_ref, dup_ref, ibase_ref, iup_ref, kk_scale, k_a):
    c = r.shape[0]
    wd = lr[:, d * DECAY_RANK:(d + 1) * DECAY_RANK]
    ad = lr[:, 2 * DECAY_RANK + d * ICLR_RANK:2 * DECAY_RANK + (d + 1) * ICLR_RANK]
    z = dbase_ref[d:d + 1, :] + _dot(jnp.tanh(wd).astype(BF16), dup_ref[d].astype(BF16))
    logw = -np.float32(np.exp(-0.5)) * jax.nn.sigmoid(z)
    a = _iclr(ad, ibase_ref[d:d + 1, :], iup_ref[d])
    kd = k * (1.0 + (a - 1.0) * k_a)
    b = kk_scale * a

    ti = lax.broadcasted_iota(jnp.int32, (c, c), 0)
    si = lax.broadcasted_iota(jnp.int32, (c, c), 1)
    tri = (si <= ti) if d == 0 else (si >= ti)
    cum = _dot(tri.astype(F32), logw, HIGHEST)
    last = c - 1 if d == 0 else 0
    cum_end = cum[last:last + 1, :]
    w_inv = jnp.exp(-cum)
    w_tail = jnp.exp(cum_end - cum)
    return dict(
        x=jnp.concatenate([(jnp.exp(cum - logw) * kk_scale).astype(BF16), (r * jnp.exp(cum)).astype(BF16)], axis=0),
        beta=(b * w_inv).astype(BF16),
        kappa=(kd * w_inv).astype(BF16),
        tail=jnp.concatenate([(kd * w_tail).astype(BF16), (-b * w_tail).astype(BF16)], axis=0),
        w_end=jnp.exp(cum_end),
        v=v.astype(BF16))


def _rwkv_chunk_update(ops, s_ref, y_refs, directions):
    c = RWKV_CHUNK
    gw = HEADS_PER_GROUP * HEAD_DIM
    n_groups = RWKV_HEADS // HEADS_PER_GROUP
    chains = [(d, g) for g in range(n_groups) for d in directions]
    tp = lax.broadcasted_iota(jnp.int32, (c, HEADS_PER_GROUP * c), 0)
    sp = lax.broadcasted_iota(jnp.int32, (c, HEADS_PER_GROUP * c), 1) % c
    before = [sp < tp, sp > tp]
    upto = [sp <= tp, sp >= tp]
    eye = (sp == tp).astype(F32)
    bi = lax.broadcasted_iota(jnp.int32, (gw, gw), 0) // HEAD_DIM
    bj = lax.broadcasted_iota(jnp.int32, (gw, gw), 1) // HEAD_DIM
    diag = bi == bj
    cols = lambda g: slice(g * gw, (g + 1) * gw)

    gram_b, gram_k, sx, s0 = {}, {}, {}, {}
    for ch in chains:
        d, g = ch
        x = ops[d]["x"][:, cols(g)]
        gram_b[ch] = _dot_nt(x, _block_diag(ops[d]["beta"][:, cols(g)], diag))
        gram_k[ch] = _dot_nt(x, _block_diag(ops[d]["kappa"][:, cols(g)], diag))
        s0[ch] = s_ref[d, g]
        sx[ch] = _dot_nt(x, s0[ch].astype(BF16))

    npow = {ch: jnp.where(before[ch[0]], -gram_b[ch][:c], 0.0) for ch in chains}
    tinv = {ch: eye + npow[ch] for ch in chains}
    nb = {ch: npow[ch].astype(BF16) for ch in chains}
    nbd = {ch: _block_diag(nb[ch], diag) for ch in chains}
    for _ in range(5):
        for ch in chains:
            nb[ch] = _dot(nb[ch], nbd[ch]).astype(BF16)
            nbd[ch] = _block_diag(nb[ch], diag)
        for ch in chains:
            tinv[ch] = tinv[ch] + _dot(tinv[ch].astype(BF16), nbd[ch])

    kv = {}
    for ch in chains:
        d, g = ch
        m2 = jnp.concatenate([jnp.where(before[d], gram_k[ch][:c], 0.0), jnp.where(upto[d], gram_k[ch][c:], 0.0)],
                             axis=0)
        kv[ch] = _dot(m2.astype(BF16), _block_diag(ops[d]["v"][:, cols(g)], diag))
    ub = {}
    for ch in chains:
        rhs = sx[ch][:c] + kv[ch][:c]
        ub[ch] = _dot(tinv[ch].astype(BF16), _block_diag(rhs.astype(BF16), diag)).astype(BF16)
    for ch in chains:
        d, g = ch
        rb = jnp.where(upto[d], gram_b[ch][c:], 0.0).astype(BF16)
        y_refs[d][:, cols(g)] = sx[ch][c:] + kv[ch][c:] - _dot(rb, _block_diag(ub[ch], diag))
    for ch in chains:
        d, g = ch
        upd = _dot_tn(jnp.concatenate([ops[d]["v"][:, cols(g)], ub[ch]], axis=0), ops[d]["tail"][:, cols(g)])
        s_ref[d, g] = s0[ch] * ops[d]["w_end"][:, cols(g)] + jnp.where(diag, upd, 0.0)


def _rwkv_scan_kernel(rf_ref, kf_ref, vf_ref, lf_ref, rb_ref, kb_ref, vb_ref, lb_ref,
                      dbase_ref, dup_ref, ibase_ref, iup_ref, kk_ref, ka_ref, yf_ref, yb_ref, s_ref):
    @pl.when(pl.program_id(0) == 0)
    def _():
        s_ref[...] = jnp.zeros_like(s_ref)

    k_k = kk_ref[...]
    k_a = ka_ref[...]
    ops = []
    for d, (r_ref, k_ref, v_ref, l_ref) in enumerate(
            ((rf_ref, kf_ref, vf_ref, lf_ref), (rb_ref, kb_ref, vb_ref, lb_ref))):
        k = k_ref[...]
        kk = k * k_k
        kk = _per_head(kk, lambda x: x / jnp.maximum(jnp.sqrt(jnp.sum(x * x, axis=-1, keepdims=True)), 1e-12))
        ops.append(_rwkv_chunk_operands(d, r_ref[...], k, v_ref[...], l_ref[...], dbase_ref, dup_ref, ibase_ref,
                                        iup_ref, kk, k_a))
    _rwkv_chunk_update(ops, s_ref, (yf_ref, yb_ref), (0, 1))


def _rwkv_scan(p, decay_base, decay_up, iclr_base, iclr_up, k_k, k_a, n_ctx):
    n_rows = p.shape[0]
    c = RWKV_CHUNK
    n_chunks = n_rows // c
    ctx_chunks = n_ctx // c

    def fwd(g):
        return g

    def bwd(g):
        return jnp.where(g < ctx_chunks, ctx_chunks - 1 - g, ctx_chunks + n_chunks - 1 - g)

    def specs(order):
        return [pl.BlockSpec((c, RWKV_W), lambda g: (order(g), COL_R // RWKV_W)),
                pl.BlockSpec((c, RWKV_W), lambda g: (order(g), COL_RK // RWKV_W)),
                pl.BlockSpec((c, RWKV_W), lambda g: (order(g), COL_RV // RWKV_W)),
                pl.BlockSpec((c, LR_W), lambda g: (order(g), COL_LR // LR_W))]

    full = lambda shape: pl.BlockSpec(shape, lambda g: (0,) * len(shape))
    return pl.pallas_call(
        _rwkv_scan_kernel,
        grid=(n_chunks,),
        in_specs=specs(fwd) + specs(bwd) + [
            full((2, RWKV_W)), full((2, DECAY_RANK, RWKV_W)), full((2, RWKV_W)), full((2, ICLR_RANK, RWKV_W)),
            full((1, RWKV_W)), full((1, RWKV_W))],
        out_specs=[pl.BlockSpec((c, RWKV_W), lambda g: (fwd(g), 0)),
                   pl.BlockSpec((c, RWKV_W), lambda g: (bwd(g), 0))],
        out_shape=[jax.ShapeDtypeStruct((n_rows, RWKV_W), F32)] * 2,
        scratch_shapes=[pltpu.VMEM((2, RWKV_HEADS // HEADS_PER_GROUP, HEADS_PER_GROUP * HEAD_DIM,
                                   HEADS_PER_GROUP * HEAD_DIM), F32)],
        compiler_params=_params(1),
        name="rwkv7_chunk_scan",
    )(p, p, p, p, p, p, p, p, decay_base, decay_up, iclr_base, iclr_up,
      k_k.reshape(1, RWKV_W), k_a.reshape(1, RWKV_W))


def _head_sums(x):
    gw = HEADS_PER_GROUP * HEAD_DIM
    bi = lax.broadcasted_iota(jnp.int32, (gw, gw), 0) // HEAD_DIM
    bj = lax.broadcasted_iota(jnp.int32, (gw, gw), 1) // HEAD_DIM
    ones = (bi == bj).astype(BF16)
    hi = x.astype(BF16)
    lo = (x - hi.astype(F32)).astype(BF16)
    parts = []
    for g in range(x.shape[1] // gw):
        gs = slice(g * gw, (g + 1) * gw)
        parts.append(_dot(hi[:, gs], ones) + _dot(lo[:, gs], ones))
    return jnp.concatenate(parts, axis=1)


def _rwkv_out_kernel(yf_ref, yb_ref, r_ref, k_ref, v_ref, lr_ref, ibase_ref, iup_ref, ka_ref, rk_ref,
                     lw_ref, lb_ref, gup_ref, o_ref):
    y = yf_ref[...] + yb_ref[...]
    r = r_ref[...]
    k = k_ref[...]
    v = v_ref[...]
    lr = lr_ref[...]
    k_a = ka_ref[...]

    yc = y - _head_sums(y) * (1.0 / HEAD_DIM)
    var = _head_sums(yc * yc) * (1.0 / HEAD_DIM)
    yn = yc * lax.rsqrt(var + RWKV_GN_EPS) * lw_ref[...] + lb_ref[...]
    kd_sum = jnp.zeros_like(k)
    for d in range(2):
        ad = lr[:, 2 * DECAY_RANK + d * ICLR_RANK:2 * DECAY_RANK + (d + 1) * ICLR_RANK]
        a = _iclr(ad, ibase_ref[d:d + 1, :], iup_ref[d])
        kd_sum = kd_sum + k * (1.0 + (a - 1.0) * k_a)
    rkk = r * kd_sum * rk_ref[...]
    bonus = _head_sums(rkk) * v
    gd = lr[:, 4 * DECAY_RANK:4 * DECAY_RANK + GATE_RANK]
    gate = _dot(jax.nn.sigmoid(gd).astype(BF16), gup_ref[...].astype(BF16))
    o_ref[...] = ((yn + bonus) * gate).astype(o_ref.dtype)


def _rwkv_out(yf, yb, p, iclr_base, iclr_up, k_a, r_k, lnx_w, lnx_b, gate_up):
    n_rows = p.shape[0]
    tm = 256
    row = lambda w, cb: pl.BlockSpec((tm, w), lambda i, cb=cb: (i, cb))
    full = lambda shape: pl.BlockSpec(shape, lambda i: (0,) * len(shape))
    vec = lambda t: t.reshape(1, RWKV_W)
    return pl.pallas_call(
        _rwkv_out_kernel,
        grid=(n_rows // tm,),
        in_specs=[row(RWKV_W, 0), row(RWKV_W, 0), row(RWKV_W, COL_R // RWKV_W), row(RWKV_W, COL_RK // RWKV_W),
                  row(RWKV_W, COL_RV // RWKV_W), row(LR_W, COL_LR // LR_W),
                  full((2, RWKV_W)), full((2, ICLR_RANK, RWKV_W)), full((1, RWKV_W)), full((1, RWKV_W)),
                  full((1, RWKV_W)), full((1, RWKV_W)), full((GATE_RANK, RWKV_W))],
        out_specs=pl.BlockSpec((tm, RWKV_W), lambda i: (i, 0)),
        out_shape=jax.ShapeDtypeStruct((n_rows, RWKV_W), BF16),
        compiler_params=_params(1),
        name="rwkv7_out",
    )(yf, yb, p, p, p, p, iclr_base, iclr_up, vec(k_a), vec(r_k), vec(lnx_w), vec(lnx_b), gate_up)


def _out_proj_kernel(a_ref, b_ref, c_ref, wa_ref, wb_ref, wc_ref, x_ref, gate_ref, o_ref, *, n_ctx, tm):
    acc = _dot(a_ref[...], wa_ref[...]) + _dot(b_ref[...], wb_ref[...]) + _dot(c_ref[...], wc_ref[...])
    row = pl.program_id(0) * tm + lax.broadcasted_iota(jnp.int32, (tm, 1), 0)
    gate = jnp.where(row < n_ctx, gate_ref[1:2, :], gate_ref[0:1, :])
    o_ref[...] = x_ref[...] + gate * acc


def _out_proj(a, b, c, w_out_bf16, xs, gate, n_ctx):
    n_rows, d = xs.shape
    tm = _row_tile(n_rows, (1280, 1024, 512, 256))
    tn = 512
    wa, wb, wc = w_out_bf16[:CONV_W], w_out_bf16[CONV_W:CONV_W + ATT_W], w_out_bf16[CONV_W + ATT_W:]
    return pl.pallas_call(
        functools.partial(_out_proj_kernel, n_ctx=n_ctx, tm=tm),
        grid=(n_rows // tm, d // tn),
        in_specs=[pl.BlockSpec((tm, CONV_W), lambda i, j: (i, 0)),
                  pl.BlockSpec((tm, ATT_W), lambda i, j: (i, 0)),
                  pl.BlockSpec((tm, RWKV_W), lambda i, j: (i, 0)),
                  pl.BlockSpec((CONV_W, tn), lambda i, j: (0, j)),
                  pl.BlockSpec((ATT_W, tn), lambda i, j: (0, j)),
                  pl.BlockSpec((RWKV_W, tn), lambda i, j: (0, j)),
                  pl.BlockSpec((tm, tn), lambda i, j: (i, j)),
                  pl.BlockSpec((8, tn), lambda i, j: (0, j))],
        out_specs=pl.BlockSpec((tm, tn), lambda i, j: (i, j)),
        out_shape=jax.ShapeDtypeStruct((n_rows, d), F32),
        compiler_params=_params(2),
        name="out_proj_residual",
    )(a, b, c, wa, wb, wc, xs, gate)


def _first_max(x, idx):
    m = jnp.max(x, axis=0, keepdims=True)
    first = jnp.min(jnp.where(x == m, idx, N_EXPERTS), axis=0, keepdims=True)
    return m, first


def _router_kernel(x_ref, g_ref, mod_ref, rw_ref, rb_ref, h_ref, idx_ref, gate_ref, rank_ref, cnt_ref, slots_ref,
                   base_ref, *, n_ctx, tm):
    @pl.when(pl.program_id(0) == 0)
    def _():
        base_ref[...] = jnp.zeros_like(base_ref)

    slots_ref[...] = jnp.zeros_like(slots_ref)

    h = _norm_mod(x_ref[...], g_ref[...], mod_ref[...], pl.program_id(0) * tm, n_ctx)
    h_ref[...] = h
    logits = _dot_nt(rw_ref[...], h, HIGHEST)
    scores = jax.nn.sigmoid(logits)
    sel = scores + rb_ref[...]
    eidx = lax.broadcasted_iota(jnp.int32, sel.shape, 0)
    best = best_score = None
    for g in range(N_EXPERT_GROUPS):
        rows = slice(g * EXPERTS_PER_GROUP, (g + 1) * EXPERTS_PER_GROUP)
        x = sel[rows]
        xi = g * EXPERTS_PER_GROUP + lax.broadcasted_iota(jnp.int32, x.shape, 0)
        m1, i1 = _first_max(x, xi)
        m2, _ = _first_max(jnp.where(xi == i1, -jnp.inf, x), xi)
        score = m1 + m2
        if g == 0:
            best, best_score = jnp.zeros_like(i1), score
        else:
            better = score > best_score
            best = jnp.where(better, g, best)
            best_score = jnp.where(better, score, best_score)
    masked = jnp.where(eidx // EXPERTS_PER_GROUP == best, sel, NEG_INF)
    _, e1 = _first_max(masked, eidx)
    _, e2 = _first_max(jnp.where(eidx == e1, -jnp.inf, masked), eidx)
    g1 = jnp.sum(jnp.where(eidx == e1, scores, 0.0), axis=0, keepdims=True)
    g2 = jnp.sum(jnp.where(eidx == e2, scores, 0.0), axis=0, keepdims=True)
    idx_ref[0:1, :] = e1
    idx_ref[1:2, :] = e2
    gate_ref[0:1, :] = g1 / (g1 + g2)
    gate_ref[1:2, :] = g2 / (g1 + g2)

    si = lax.broadcasted_iota(jnp.int32, (tm, tm), 0)
    ti = lax.broadcasted_iota(jnp.int32, (tm, tm), 1)
    prefix = (si <= ti).astype(BF16)
    base = base_ref[...]
    for kth, e in enumerate((e1, e2)):
        hit = eidx == e
        seen = _dot(hit.astype(BF16), prefix)
        rank = jnp.sum(jnp.where(hit, seen - 1.0 + base, 0.0), axis=0, keepdims=True)
        rank_ref[kth:kth + 1, :] = rank.astype(jnp.int32)
        base = base + seen[:, tm - 1:tm]
    base_ref[...] = base
    cnt_ref[...] = base.astype(jnp.int32)


def _route(xs, g, mod, router_w, router_b, n_ctx, cap):
    n_rows, d = xs.shape
    tm = 256
    n_tiles = n_rows // tm
    slab = -(-cap // (n_tiles * SUBLANES)) * SUBLANES
    return pl.pallas_call(
        functools.partial(_router_kernel, n_ctx=n_ctx, tm=tm),
        grid=(n_rows // tm,),
        in_specs=[pl.BlockSpec((tm, d), lambda i: (i, 0)),
                  pl.BlockSpec((1, d), lambda i: (0, 0)),
                  pl.BlockSpec((8, d), lambda i: (0, 0)),
                  pl.BlockSpec((N_EXPERTS, d), lambda i: (0, 0)),
                  pl.BlockSpec((N_EXPERTS, 1), lambda i: (0, 0))],
        out_specs=[pl.BlockSpec((tm, d), lambda i: (i, 0)),
                   pl.BlockSpec((TOP_K, tm), lambda i: (0, i)),
                   pl.BlockSpec((TOP_K, tm), lambda i: (0, i)),
                   pl.BlockSpec((TOP_K, tm), lambda i: (0, i)),
                   pl.BlockSpec((N_EXPERTS, 1), lambda i: (0, 0)),
                   pl.BlockSpec((slab, d), lambda i: (i, 0))],
        out_shape=[jax.ShapeDtypeStruct((n_rows, d), F32),
                   jax.ShapeDtypeStruct((TOP_K, n_rows), jnp.int32),
                   jax.ShapeDtypeStruct((TOP_K, n_rows), F32),
                   jax.ShapeDtypeStruct((TOP_K, n_rows), jnp.int32),
                   jax.ShapeDtypeStruct((N_EXPERTS, 1), jnp.int32),
                   jax.ShapeDtypeStruct((n_tiles * slab, d), F32)],
        scratch_shapes=[pltpu.VMEM((N_EXPERTS, 1), F32)],
        compiler_params=_params(1),
        name="moe_norm_route",
    )(xs, g.reshape(1, d), mod, router_w.T, router_b.reshape(N_EXPERTS, 1))


def _row_copy(src_ref, src_row, dst_ref, dst_row, sem):
    return pltpu.make_async_copy(src_ref.at[pl.ds(src_row, 1), :], dst_ref.at[pl.ds(dst_row, 1), :], sem)


def _dispatch_kernel(dest_ref, h_ref, init_ref, xg_ref, sem, *, tm):
    del init_ref

    def start(t, carry):
        for k in range(TOP_K):
            _row_copy(h_ref, t, xg_ref, dest_ref[0, k, t], sem).start()
        return carry

    lax.fori_loop(0, tm, start, 0, unroll=DMA_UNROLL)
    for _ in range(TOP_K):
        pltpu.make_async_copy(h_ref, xg_ref.at[pl.ds(0, tm), :], sem).wait()


def _dispatch(hp, dest_tiles, slots, tm):
    n_rows, w = hp.shape
    return pl.pallas_call(
        functools.partial(_dispatch_kernel, tm=tm),
        grid=(n_rows // tm,),
        in_specs=[pl.BlockSpec((1, TOP_K, tm), lambda i: (i, 0, 0), memory_space=pltpu.SMEM),
                  pl.BlockSpec((tm, w), lambda i: (i, 0)),
                  pl.BlockSpec(memory_space=pl.ANY)],
        out_specs=pl.BlockSpec(memory_space=pl.ANY),
        out_shape=jax.ShapeDtypeStruct(slots.shape, slots.dtype),
        scratch_shapes=[pltpu.SemaphoreType.DMA(())],
        input_output_aliases={2: 0},
        compiler_params=_params(1),
        name="moe_dispatch_rows",
    )(dest_tiles, hp, slots)


def _expert_kernel(be_ref, nused_ref, x_ref, wg_ref, wu_ref, wd_ref, o_ref, wgb_ref, wub_ref, wdb_ref):
    i = pl.program_id(0)
    used = i < nused_ref[0]
    new_expert = (i == 0) | (be_ref[i] != be_ref[jnp.maximum(i - 1, 0)])

    @pl.when(used & new_expert)
    def _():
        for r0 in range(0, D_MODEL, CAST_ROWS):
            rows = slice(r0, r0 + CAST_ROWS)
            wgb_ref[rows, :] = wg_ref[0, rows, :].astype(BF16)
            wub_ref[rows, :] = wu_ref[0, rows, :].astype(BF16)
        for r0 in range(0, D_EXPERT, CAST_ROWS):
            rows = slice(r0, r0 + CAST_ROWS)
            wdb_ref[rows, :] = wd_ref[0, rows, :].astype(BF16)

    @pl.when(used)
    def _():
        x = x_ref[...].astype(BF16)
        gate = _dot(x, wgb_ref[...])
        up = _dot(x, wub_ref[...])
        act = (gate * jax.nn.sigmoid(gate) * up).astype(BF16)
        o_ref[...] = _dot(act, wdb_ref[...])

    @pl.when(i >= nused_ref[0])
    def _():
        o_ref[...] = jnp.zeros_like(o_ref)


def _expert_ffn(xg, block_e, n_used, wg, wu, wd, layer, cap, tm):
    d = xg.shape[1]
    grid_spec = pltpu.PrefetchScalarGridSpec(
        num_scalar_prefetch=2,
        grid=(cap // tm,),
        in_specs=[pl.BlockSpec((tm, d), lambda i, be, nu: (i, 0)),
                  pl.BlockSpec((None, 1, d, D_EXPERT), lambda i, be, nu: (layer, be[i], 0, 0)),
                  pl.BlockSpec((None, 1, d, D_EXPERT), lambda i, be, nu: (layer, be[i], 0, 0)),
                  pl.BlockSpec((None, 1, D_EXPERT, d), lambda i, be, nu: (layer, be[i], 0, 0))],
        out_specs=pl.BlockSpec((tm, d), lambda i, be, nu: (i, 0)),
        scratch_shapes=[pltpu.VMEM((d, D_EXPERT), BF16), pltpu.VMEM((d, D_EXPERT), BF16),
                        pltpu.VMEM((D_EXPERT, d), BF16)],
    )
    return pl.pallas_call(
        _expert_kernel,
        grid_spec=grid_spec,
        out_shape=jax.ShapeDtypeStruct((cap, d), F32),
        compiler_params=pltpu.CompilerParams(dimension_semantics=("arbitrary",),
                                             vmem_limit_bytes=EXPERT_VMEM_LIMIT),
        name="moe_expert_ffn",
    )(block_e, n_used, xg, wg, wu, wd)


def _combine_kernel(dest_ref, x_ref, gt_ref, g2_ref, fg_ref, yb_ref, o_ref, ybuf_ref, sem,
                    *, n_ctx, tm, final_norm, first_tile):
    def start(t, carry):
        for k in range(TOP_K):
            _row_copy(yb_ref, dest_ref[0, k, t], ybuf_ref.at[k], t, sem).start()
        return carry

    lax.fori_loop(0, tm, start, 0, unroll=DMA_UNROLL)
    for k in range(TOP_K):
        pltpu.make_async_copy(yb_ref.at[pl.ds(0, tm), :], ybuf_ref.at[k], sem).wait()

    is_ctx = (pl.program_id(0) + first_tile) * tm < n_ctx
    g2 = jnp.where(is_ctx, g2_ref[1:2, :], g2_ref[0:1, :])
    gt = gt_ref[...]
    y = gt[:, 0:1] * ybuf_ref[0] + gt[:, 1:2] * ybuf_ref[1]
    x = x_ref[...] + g2 * y
    if final_norm:
        ms = jnp.mean(x * x, axis=-1, keepdims=True)
        x = x * lax.rsqrt(ms + RMS_EPS) * fg_ref[...]
    o_ref[...] = x


def _combine(xs, yb, dest_tiles, gates_t, g2, final_g, n_ctx, final_norm, tm):
    n_rows, d = xs.shape
    skip = n_ctx // tm if final_norm else 0
    return pl.pallas_call(
        functools.partial(_combine_kernel, n_ctx=n_ctx, tm=tm, final_norm=final_norm, first_tile=skip),
        grid=(n_rows // tm - skip,),
        in_specs=[pl.BlockSpec((1, TOP_K, tm), lambda i: (i + skip, 0, 0), memory_space=pltpu.SMEM),
                  pl.BlockSpec((tm, d), lambda i: (i + skip, 0)),
                  pl.BlockSpec((tm, TOP_K), lambda i: (i + skip, 0)),
                  pl.BlockSpec((8, d), lambda i: (0, 0)), pl.BlockSpec((1, d), lambda i: (0, 0)),
                  pl.BlockSpec(memory_space=pl.ANY)],
        out_specs=pl.BlockSpec((tm, d), lambda i: (i, 0)),
        out_shape=jax.ShapeDtypeStruct((n_rows - skip * tm, d), F32),
        scratch_shapes=[pltpu.VMEM((TOP_K, tm, yb.shape[1]), yb.dtype), pltpu.SemaphoreType.DMA(())],
        compiler_params=_params(1),
        name="moe_combine_residual",
    )(dest_tiles, xs, gates_t, g2, final_g.reshape(1, d), yb)


def _moe(xs, g, mod, g2, router_w, router_b, wg, wu, wd, layer, final_g, n_ctx, final_norm):
    n_rows, d = xs.shape
    tm = 256
    n_asg = n_rows * TOP_K
    n_blk = n_asg // tm + N_EXPERTS
    cap = n_blk * tm
    h, idx, gates, rank, counts, slots = _route(xs, g, mod, router_w, router_b, n_ctx, cap)
    counts = counts[:, 0]
    padded = (counts + tm - 1) // tm * tm
    pends = jnp.cumsum(padded)
    pstarts = pends - padded
    seg_start = jnp.sum(jnp.where(idx[..., None] == jnp.arange(N_EXPERTS), pstarts, 0), axis=-1)
    dest = seg_start + rank
    dest_tiles = dest.reshape(TOP_K, n_rows // tm, tm).transpose(1, 0, 2).astype(jnp.int32)
    block_e = jnp.minimum(jnp.sum(pends[None, :] <= (jnp.arange(n_blk) * tm)[:, None], axis=1), N_EXPERTS - 1)
    n_used = (pends[-1] // tm).reshape(1)
    xg = _dispatch(h, dest_tiles, slots, tm)
    yb = _expert_ffn(xg, block_e.astype(jnp.int32), n_used.astype(jnp.int32), wg, wu, wd, layer, cap, tm)
    return _combine(xs, yb, dest_tiles, gates.T, g2, final_g, n_ctx, final_norm, tm)


def _mod_rows(mod_l, lat_chunks, ctx_chunks):
    d = D_MODEL
    rows = [mod_l[0, c * d:(c + 1) * d] for c in lat_chunks] + [mod_l[1, c * d:(c + 1) * d] for c in ctx_chunks]
    out = jnp.zeros((8, d), F32)
    return out.at[:len(rows)].set(jnp.stack(rows))


def kernel(x, c, ctx, c_ctx, ada_w, ada_b, norm1_g, norm2_g, w_in, w_out, conv_w, attn_sinks, decay_base,
           decay_up, iclr_base, iclr_up, gate_up, k_k, k_a, r_k, lnx_w, lnx_b, router_w, router_b,
           expert_gate, expert_up, expert_down, final_norm_g):
    bsz, n_lat, d = x.shape
    n_ctx = ctx.shape[1]
    depth = ada_w.shape[0]
    assert bsz == 1 and d == D_MODEL and n_ctx % NORM_ROWS == 0 and n_lat % NORM_ROWS == 0

    xs = jnp.concatenate([ctx[0], x[0]], axis=0)
    mods = _ada_mod(c, c_ctx, ada_w, ada_b)
    cos, sin = _rope_tables(n_ctx, n_lat)

    for l in range(depth):
        last = l == depth - 1
        mod1 = _mod_rows(mods[l], (0, 1), (0, 1))
        gate1 = _mod_rows(mods[l], (2,), (2,))
        mod2 = _mod_rows(mods[l], (3, 4), (3, 4))
        gate2 = _mod_rows(mods[l], (5,), (5,))

        p = _norm_mod_matmul(xs, norm1_g[l], mod1, _in_proj_weights(w_in[l]), n_ctx)

        a_mix = _short_conv(p, conv_w[l], n_ctx)
        qr, kr, vb = _rope_qkv(p, cos, sin)
        b_mix = _window_attention(qr, kr, vb, attn_sinks[l], n_ctx, n_lat)
        yf, yb = _rwkv_scan(p, decay_base[l], decay_up[l], iclr_base[l], iclr_up[l], k_k[l], k_a[l], n_ctx)
        c_mix = _rwkv_out(yf, yb, p, iclr_base[l], iclr_up[l], k_a[l], r_k[l].reshape(-1), lnx_w[l], lnx_b[l],
                          gate_up[l])
        xs = _out_proj(a_mix, b_mix, c_mix, w_out[l].astype(BF16), xs, gate1, n_ctx)

        xs = _moe(xs, norm2_g[l], mod2, gate2, router_w, router_b, expert_gate, expert_up, expert_down, l,
                  final_norm_g, n_ctx, last)
    return xs.reshape(bsz, n_lat, d)
```

```python
import functools

import numpy as np
import jax
import jax.numpy as jnp
from jax import lax
from jax.experimental import pallas as pl
from jax.experimental.pallas import tpu as pltpu

F32 = jnp.float32
BF16 = jnp.bfloat16
HIGHEST = lax.Precision.HIGHEST

D_MODEL = 2048
GRID_W = 64
CONV_W = D_MODEL // 4
CONV_K = 3
HEAD_DIM = 64
ATT_HEADS = 12
ATT_KV_HEADS = 4
ATT_GROUP = ATT_HEADS // ATT_KV_HEADS
ATT_W = ATT_HEADS * HEAD_DIM
ATT_KV_W = ATT_KV_HEADS * HEAD_DIM
RWKV_HEADS = 12
RWKV_W = RWKV_HEADS * HEAD_DIM
WINDOW = 128
ATT_BLOCK = 128
ROPE_THETA = 10000.0
ROPE_FREQS = HEAD_DIM // 4
DECAY_RANK = 64
ICLR_RANK = 64
GATE_RANK = 128
RWKV_GN_EPS = 64e-5
N_EXPERTS = 32
N_EXPERT_GROUPS = 4
EXPERTS_PER_GROUP = N_EXPERTS // N_EXPERT_GROUPS
TOP_K = 2
D_EXPERT = 768
RMS_EPS = 1e-6
NEG_INF = -1e30

COL_Q = 0
COL_R = 768
COL_RK = 1536
COL_RV = 2304
COL_CB = 3072
COL_CC = 3584
COL_CH = 4096
COL_K = 4608
COL_V = 4864
COL_LR = 5120
LR_W = 512
IN_W_PAD = 5632

RWKV_CHUNK = 64
HEADS_PER_GROUP = 4
NORM_ROWS = 256
SUBLANES = 8
LANES = 128
CAST_ROWS = 256
DMA_UNROLL = 8
VMEM_LIMIT = 56 * 1024 * 1024


def _params(n_axes):
    return pltpu.CompilerParams(dimension_semantics=("arbitrary",) * n_axes,
                                vmem_limit_bytes=VMEM_LIMIT)


def _row_tile(n_rows, candidates):
    for t in candidates:
        if n_rows % t == 0:
            return t
    raise ValueError(f"no row tile for {n_rows}")


def _in_proj_weights(w):
    o_cb, o_cc, o_ch = 0, 512, 1024
    o_q, o_k, o_v = 1536, 2304, 2560
    o_r, o_rk, o_rv = 2816, 3584, 4352
    o_lr = 5120
    segs = [(o_q, 768), (o_r, 768), (o_rk, 768), (o_rv, 768), (o_cb, 512), (o_cc, 512), (o_ch, 512),
            (o_k, 256), (o_v, 256), (o_lr, 384)]
    parts = [w[:, o:o + n].astype(BF16) for o, n in segs]
    parts.append(jnp.zeros((w.shape[0], IN_W_PAD - sum(n for _, n in segs)), BF16))
    return jnp.concatenate(parts, axis=1)


def _ada_kernel(s_ref, w_ref, b_ref, o_ref):
    d, tn = w_ref.shape[1], w_ref.shape[2]
    reps = tn // LANES
    acc = [jnp.zeros((1, tn), F32) for _ in range(2)]
    for k0 in range(0, d, NORM_ROWS):
        wk = w_ref[0, k0:k0 + NORM_ROWS, :]
        for r in range(2):
            sb = jnp.concatenate([s_ref[r, k0:k0 + NORM_ROWS, :]] * reps, axis=1)
            acc[r] = acc[r] + jnp.sum(wk * sb, axis=0, keepdims=True)
    o_ref[0] = jnp.zeros(o_ref.shape[1:], F32)
    for r in range(2):
        o_ref[0, r:r + 1, :] = acc[r] + b_ref[0]


def _ada_mod(c, c_ctx, ada_w, ada_b):
    depth, d, n = ada_w.shape
    s = jnp.stack([c[0], c_ctx])
    s = s * jax.nn.sigmoid(s)
    s = jnp.broadcast_to(s[:, :, None], (2, d, LANES))
    tn = 1024
    return pl.pallas_call(
        _ada_kernel,
        grid=(depth, n // tn),
        in_specs=[pl.BlockSpec((2, d, LANES), lambda l, j: (0, 0, 0)),
                  pl.BlockSpec((1, d, tn), lambda l, j: (l, 0, j)),
                  pl.BlockSpec((1, 1, tn), lambda l, j: (l, 0, j))],
        out_specs=pl.BlockSpec((1, 8, tn), lambda l, j: (l, 0, j)),
        out_shape=jax.ShapeDtypeStruct((depth, 8, n), F32),
        compiler_params=_params(2),
        name="ada_mod",
    )(s, ada_w, ada_b.reshape(depth, 1, n))


def _norm_mod(x, g, mod, row0, n_ctx):
    ms = jnp.mean(x * x, axis=-1, keepdims=True)
    y = x * lax.rsqrt(ms + RMS_EPS) * g
    is_ctx = row0 < n_ctx
    shift = jnp.where(is_ctx, mod[2:3, :], mod[0:1, :])
    scale = jnp.where(is_ctx, mod[3:4, :], mod[1:2, :])
    return y * (1.0 + scale) + shift


def _nmm_kernel(x_ref, g_ref, mod_ref, w_ref, o_ref, h_ref, *, n_ctx, tm):
    i = pl.program_id(0)

    @pl.when(pl.program_id(1) == 0)
    def _():
        for r0 in range(0, tm, NORM_ROWS):
            rows = slice(r0, r0 + NORM_ROWS)
            h_ref[rows, :] = _norm_mod(x_ref[rows, :], g_ref[...], mod_ref[...], i * tm + r0, n_ctx).astype(BF16)

    o_ref[...] = jnp.dot(h_ref[...], w_ref[...], preferred_element_type=F32)


def _norm_mod_matmul(xs, g, mod, w_bf16, n_ctx):
    n_rows, d = xs.shape
    n_out = w_bf16.shape[1]
    tm = _row_tile(n_rows, (1280, 1024, 512, 256))
    tn = 512
    return pl.pallas_call(
        functools.partial(_nmm_kernel, n_ctx=n_ctx, tm=tm),
        grid=(n_rows // tm, n_out // tn),
        in_specs=[pl.BlockSpec((tm, d), lambda i, j: (i, 0)),
                  pl.BlockSpec((1, d), lambda i, j: (0, 0)),
                  pl.BlockSpec((8, d), lambda i, j: (0, 0)),
                  pl.BlockSpec((d, tn), lambda i, j: (0, j))],
        out_specs=pl.BlockSpec((tm, tn), lambda i, j: (i, j)),
        out_shape=jax.ShapeDtypeStruct((n_rows, n_out), F32),
        scratch_shapes=[pltpu.VMEM((tm, d), BF16)],
        compiler_params=_params(2),
        name="norm_mod_in_proj",
    )(xs, g.reshape(1, d), mod, w_bf16)


def _conv_kernel(cb_ref, cc_ref, ch_ref, ccp_ref, chp_ref, ccn_ref, chn_ref, w_ref, o_ref, *, n_ctx, n_rows, tm):
    i = pl.program_id(0)
    u = cc_ref[...] * ch_ref[...]
    u_prev_row = ccp_ref[7:8, :] * chp_ref[7:8, :]
    u_next_row = ccn_ref[0:1, :] * chn_ref[0:1, :]
    loc = lax.broadcasted_iota(jnp.int32, (tm, 1), 0)
    row = i * tm + loc
    up = jnp.where(loc == 0, u_prev_row, pltpu.roll(u, 1, axis=0))
    dn = jnp.where(loc == tm - 1, u_next_row, pltpu.roll(u, tm - 1, axis=0))
    up = jnp.where((row == 0) | (row == n_ctx), 0.0, up)
    dn = jnp.where((row == n_ctx - 1) | (row == n_rows - 1), 0.0, dn)
    w = w_ref[...]
    y = w[0:1, :] * up + w[1:2, :] * u + w[2:3, :] * dn
    o_ref[...] = (cb_ref[...] * y).astype(o_ref.dtype)


def _short_conv(p, conv_w, n_ctx):
    n_rows = p.shape[0]
    tm = _row_tile(n_rows, (1280, 1024, 512, 256))
    r8 = tm // 8
    last8 = n_rows // 8 - 1
    wpad = jnp.zeros((8, CONV_W), F32).at[:CONV_K].set(conv_w)
    blk = lambda c: pl.BlockSpec((tm, CONV_W), lambda i, c=c: (i, c))
    prev = lambda c: pl.BlockSpec((8, CONV_W), lambda i, c=c: (jnp.maximum(i * r8 - 1, 0), c))
    nxt = lambda c: pl.BlockSpec((8, CONV_W), lambda i, c=c: (jnp.minimum((i + 1) * r8, last8), c))
    cb, cc, ch = COL_CB // CONV_W, COL_CC // CONV_W, COL_CH // CONV_W
    return pl.pallas_call(
        functools.partial(_conv_kernel, n_ctx=n_ctx, n_rows=n_rows, tm=tm),
        grid=(n_rows // tm,),
        in_specs=[blk(cb), blk(cc), blk(ch), prev(cc), prev(ch), nxt(cc), nxt(ch),
                  pl.BlockSpec((8, CONV_W), lambda i: (0, 0))],
        out_specs=pl.BlockSpec((tm, CONV_W), lambda i: (i, 0)),
        out_shape=jax.ShapeDtypeStruct((n_rows, CONV_W), BF16),
        compiler_params=_params(1),
        name="short_conv",
    )(p, p, p, p, p, p, p, wpad)


def _swap_halves(x):
    n = x.shape[1]
    lane = lax.broadcasted_iota(jnp.int32, x.shape, 1)
    fwd = pltpu.roll(x, n - ROPE_FREQS, axis=1)
    bwd = pltpu.roll(x, ROPE_FREQS, axis=1)
    return jnp.where((lane % (2 * ROPE_FREQS)) < ROPE_FREQS, fwd, bwd)


def _rope_kernel(q_ref, k_ref, v_ref, cos_ref, sin_ref, qo_ref, ko_ref, vo_ref):
    cos = cos_ref[...]
    sin = sin_ref[...]
    q = q_ref[...]
    k = k_ref[...]
    cos_q = jnp.concatenate([cos] * (ATT_W // 128), axis=1)
    sin_q = jnp.concatenate([sin] * (ATT_W // 128), axis=1)
    cos_k = jnp.concatenate([cos] * (ATT_KV_W // 128), axis=1)
    sin_k = jnp.concatenate([sin] * (ATT_KV_W // 128), axis=1)
    qo_ref[...] = ((q * cos_q + _swap_halves(q) * sin_q) * (HEAD_DIM ** -0.5)).astype(BF16)
    ko_ref[...] = (k * cos_k + _swap_halves(k) * sin_k).astype(BF16)
    vo_ref[...] = v_ref[...].astype(BF16)


def _rope_tables(n_ctx, n_lat):
    row = jnp.repeat(jnp.arange(n_lat // GRID_W, dtype=jnp.int32), GRID_W)
    col = jnp.arange(n_lat, dtype=jnp.int32) % GRID_W
    inv = ROPE_THETA ** (-jnp.arange(ROPE_FREQS, dtype=F32) / ROPE_FREQS)
    ang_r = row[:, None].astype(F32) * inv[None, :]
    ang_c = col[:, None].astype(F32) * inv[None, :]
    cr, sr, cc, sc = jnp.cos(ang_r), jnp.sin(ang_r), jnp.cos(ang_c), jnp.sin(ang_c)
    cos = jnp.concatenate([cr, cr, cc, cc], axis=1)
    sin = jnp.concatenate([-sr, sr, -sc, sc], axis=1)
    cos = jnp.concatenate([jnp.ones((n_ctx, HEAD_DIM), F32), cos], axis=0)
    sin = jnp.concatenate([jnp.zeros((n_ctx, HEAD_DIM), F32), sin], axis=0)
    return jnp.tile(cos, (1, 2)), jnp.tile(sin, (1, 2))


def _rope_qkv(p, cos, sin):
    n_rows = p.shape[0]
    tm = _row_tile(n_rows, (1280, 1024, 512, 256))
    return pl.pallas_call(
        _rope_kernel,
        grid=(n_rows // tm,),
        in_specs=[pl.BlockSpec((tm, ATT_W), lambda i: (i, COL_Q // ATT_W)),
                  pl.BlockSpec((tm, ATT_KV_W), lambda i: (i, COL_K // ATT_KV_W)),
                  pl.BlockSpec((tm, ATT_KV_W), lambda i: (i, COL_V // ATT_KV_W)),
                  pl.BlockSpec((tm, 128), lambda i: (i, 0)),
                  pl.BlockSpec((tm, 128), lambda i: (i, 0))],
        out_specs=[pl.BlockSpec((tm, ATT_W), lambda i: (i, 0)),
                   pl.BlockSpec((tm, ATT_KV_W), lambda i: (i, 0)),
                   pl.BlockSpec((tm, ATT_KV_W), lambda i: (i, 0))],
        out_shape=[jax.ShapeDtypeStruct((n_rows, ATT_W), BF16),
                   jax.ShapeDtypeStruct((n_rows, ATT_KV_W), BF16),
                   jax.ShapeDtypeStruct((n_rows, ATT_KV_W), BF16)],
        compiler_params=_params(1),
        name="rope_qkv",
    )(p, p, p, cos, sin)


def _attn_kernel(sink_ref, bias_ref, q_ref, kp_ref, kc_ref, kn_ref, vp_ref, vc_ref, vn_ref, kx_ref, vx_ref, o_ref,
                 *, n_ctx, n_blocks):
    i = pl.program_id(0)
    blk = ATT_BLOCK
    ctx_blocks = n_ctx // blk
    ci = lax.broadcasted_iota(jnp.int32, (1, n_ctx + 3 * blk), 1)
    no_prev = i <= ctx_blocks
    no_cur = i < ctx_blocks
    no_next = no_cur | (i == n_blocks - 1)
    dead = (((ci >= n_ctx) & (ci < n_ctx + blk) & no_prev)
            | ((ci >= n_ctx + blk) & (ci < n_ctx + 2 * blk) & no_cur)
            | ((ci >= n_ctx + 2 * blk) & no_next))
    bias = bias_ref[...] + jnp.where(dead, NEG_INF, 0.0)

    q = q_ref[...]
    outs = [None] * ATT_HEADS
    for g in range(ATT_KV_HEADS):
        ks = slice(g * HEAD_DIM, (g + 1) * HEAD_DIM)
        kk = jnp.concatenate([kx_ref[:, ks], kp_ref[:, ks], kc_ref[:, ks], kn_ref[:, ks]], axis=0)
        vv = jnp.concatenate([vx_ref[:, ks], vp_ref[:, ks], vc_ref[:, ks], vn_ref[:, ks]], axis=0)
        heads = [g * ATT_GROUP + j for j in range(ATT_GROUP)]
        qq = jnp.concatenate([q[:, h * HEAD_DIM:(h + 1) * HEAD_DIM] for h in heads], axis=0)
        s = lax.dot_general(qq, kk, (((1,), (1,)), ((), ())), preferred_element_type=F32) + bias
        sink = jnp.concatenate([jnp.full((blk, 1), sink_ref[h], F32) for h in heads], axis=0)
        m = jnp.maximum(jnp.max(s, axis=-1, keepdims=True), sink)
        e = jnp.exp(s - m)
        den = jnp.sum(e, axis=-1, keepdims=True) + jnp.exp(sink - m)
        o = jnp.dot(e.astype(BF16), vv, preferred_element_type=F32) / den
        for j, h in enumerate(heads):
            outs[h] = o[j * blk:(j + 1) * blk, :]
    o_ref[...] = jnp.concatenate(outs, axis=1).astype(o_ref.dtype)


def _window_attention(qr, kr, vb, sinks, n_ctx, n_lat):
    n_rows = qr.shape[0]
    nb = n_rows // ATT_BLOCK
    q_spec = pl.BlockSpec((ATT_BLOCK, ATT_W), lambda i: (i, 0))
    prev = pl.BlockSpec((ATT_BLOCK, ATT_KV_W), lambda i: (jnp.maximum(i - 1, 0), 0))
    cur = pl.BlockSpec((ATT_BLOCK, ATT_KV_W), lambda i: (i, 0))
    nxt = pl.BlockSpec((ATT_BLOCK, ATT_KV_W), lambda i: (jnp.minimum(i + 1, nb - 1), 0))
    ctx = pl.BlockSpec((n_ctx, ATT_KV_W), lambda i: (0, 0))
    n_keys = n_ctx + 3 * ATT_BLOCK
    qi = jnp.arange(ATT_BLOCK)[:, None]
    ci = jnp.arange(n_keys)[None, :]
    in_window = (ci < n_ctx) | (jnp.abs(ci - n_ctx - ATT_BLOCK - qi) <= WINDOW)
    bias = jnp.tile(jnp.where(in_window, 0.0, NEG_INF).astype(F32), (ATT_GROUP, 1))
    return pl.pallas_call(
        functools.partial(_attn_kernel, n_ctx=n_ctx, n_blocks=nb),
        grid=(nb,),
        in_specs=[pl.BlockSpec(memory_space=pltpu.SMEM), pl.BlockSpec(bias.shape, lambda i: (0, 0)),
                  q_spec, prev, cur, nxt, prev, cur, nxt, ctx, ctx],
        out_specs=pl.BlockSpec((ATT_BLOCK, ATT_W), lambda i: (i, 0)),
        out_shape=jax.ShapeDtypeStruct((n_rows, ATT_W), BF16),
        compiler_params=_params(1),
        name="window_attention",
    )(sinks.astype(F32), bias, qr, kr, kr, kr, vb, vb, vb, kr, vb)


def _per_head(x, fn):
    return jnp.concatenate([fn(x[:, h * HEAD_DIM:(h + 1) * HEAD_DIM]) for h in range(RWKV_HEADS)], axis=1)


def _dot_nt(a, b, precision=None):
    return lax.dot_general(a, b, (((1,), (1,)), ((), ())), precision=precision, preferred_element_type=F32)


def _dot_tn(a, b, precision=None):
    return lax.dot_general(a, b, (((0,), (0,)), ((), ())), precision=precision, preferred_element_type=F32)


def _dot(a, b, precision=None):
    return jnp.dot(a, b, precision=precision, preferred_element_type=F32)


def _iclr(ad, ibase, iup):
    return jax.nn.sigmoid(ibase + _dot(ad.astype(BF16), iup.astype(BF16)))


def _block_diag(y, mask):
    return jnp.where(mask, jnp.concatenate([y] * HEADS_PER_GROUP, axis=0), jnp.zeros((), y.dtype))


def _rwkv_chunk_operands(d, r, k, v, lr, dbase_ref, dup_ref, ibase_ref, iup_ref, kk_scale, k_a):
    c = r.shape[0]
    wd = lr[:, d * DECAY_RANK:(d + 1) * DECAY_RANK]
    ad = lr[:, 2 * DECAY_RANK + d * ICLR_RANK:2 * DECAY_RANK + (d + 1) * ICLR_RANK]
    z = dbase_ref[d:d + 1, :] + _dot(jnp.tanh(wd).astype(BF16), dup_ref[d].astype(BF16))
    logw = -np.float32(np.exp(-0.5)) * jax.nn.sigmoid(z)
    a = _iclr(ad, ibase_ref[d:d + 1, :], iup_ref[d])
    kd = k * (1.0 + (a - 1.0) * k_a)
    b = kk_scale * a

    ti = lax.broadcasted_iota(jnp.int32, (c, c), 0)
    si = lax.broadcasted_iota(jnp.int32, (c, c), 1)
    tri = (si <= ti) if d == 0 else (si >= ti)
    cum = _dot(tri.astype(F32), logw, HIGHEST)
    last = c - 1 if d == 0 else 0
    cum_end = cum[last:last + 1, :]
    w_inv = jnp.exp(-cum)
    w_tail = jnp.exp(cum_end - cum)
    return dict(
        x=jnp.concatenate([(jnp.exp(cum - logw) * kk_scale).astype(BF16), (r * jnp.exp(cum)).astype(BF16)], axis=0),
        beta=(b * w_inv).astype(BF16),
        kappa=(kd * w_inv).astype(BF16),
        tail=jnp.concatenate([(kd * w_tail).astype(BF16), (-b * w_tail).astype(BF16)], axis=0),
        w_end=jnp.exp(cum_end),
        v=v.astype(BF16))


def _rwkv_chunk_update(ops, s_ref, y_refs, directions):
    c = RWKV_CHUNK
    gw = HEADS_PER_GROUP * HEAD_DIM
    n_groups = RWKV_HEADS // HEADS_PER_GROUP
    chains = [(d, g) for g in range(n_groups) for d in directions]
    tp = lax.broadcasted_iota(jnp.int32, (c, HEADS_PER_GROUP * c), 0)
    sp = lax.broadcasted_iota(jnp.int32, (c, HEADS_PER_GROUP * c), 1) % c
    before = [sp < tp, sp > tp]
    upto = [sp <= tp, sp >= tp]
    eye = (sp == tp).astype(F32)
    bi = lax.broadcasted_iota(jnp.int32, (gw, gw), 0) // HEAD_DIM
    bj = lax.broadcasted_iota(jnp.int32, (gw, gw), 1) // HEAD_DIM
    diag = bi == bj
    cols = lambda g: slice(g * gw, (g + 1) * gw)

    gram_b, gram_k, sx, s0 = {}, {}, {}, {}
    for ch in chains:
        d, g = ch
        x = ops[d]["x"][:, cols(g)]
        gram_b[ch] = _dot_nt(x, _block_diag(ops[d]["beta"][:, cols(g)], diag))
        gram_k[ch] = _dot_nt(x, _block_diag(ops[d]["kappa"][:, cols(g)], diag))
        s0[ch] = s_ref[d, g]
        sx[ch] = _dot_nt(x, s0[ch].astype(BF16))

    npow = {ch: jnp.where(before[ch[0]], -gram_b[ch][:c], 0.0) for ch in chains}
    tinv = {ch: eye + npow[ch] for ch in chains}
    nb = {ch: npow[ch].astype(BF16) for ch in chains}
    nbd = {ch: _block_diag(nb[ch], diag) for ch in chains}
    for _ in range(5):
        for ch in chains:
            nb[ch] = _dot(nb[ch], nbd[ch]).astype(BF16)
            nbd[ch] = _block_diag(nb[ch], diag)
        for ch in chains:
            tinv[ch] = tinv[ch] + _dot(tinv[ch].astype(BF16), nbd[ch])

    kv = {}
    for ch in chains:
        d, g = ch
        m2 = jnp.concatenate([jnp.where(before[d], gram_k[ch][:c], 0.0), jnp.where(upto[d], gram_k[ch][c:], 0.0)],
                             axis=0)
        kv[ch] = _dot(m2.astype(BF16), _block_diag(ops[d]["v"][:, cols(g)], diag))
    ub = {}
    for ch in chains:
        rhs = sx[ch][:c] + kv[ch][:c]
        ub[ch] = _dot(tinv[ch].astype(BF16), _block_diag(rhs.astype(BF16), diag)).astype(BF16)
    for ch in chains:
        d, g = ch
        rb = jnp.where(upto[d], gram_b[ch][c:], 0.0).astype(BF16)
        y_refs[d][:, cols(g)] = sx[ch][c:] + kv[ch][c:] - _dot(rb, _block_diag(ub[ch], diag))
    for ch in chains:
        d, g = ch
        upd = _dot_tn(jnp.concatenate([ops[d]["v"][:, cols(g)], ub[ch]], axis=0), ops[d]["tail"][:, cols(g)])
        s_ref[d, g] = s0[ch] * ops[d]["w_end"][:, cols(g)] + jnp.where(diag, upd, 0.0)


def _rwkv_scan_kernel(rf_ref, kf_ref, vf_ref, lf_ref, rb_ref, kb_ref, vb_ref, lb_ref,
                      dbase_ref, dup_ref, ibase_ref, iup_ref, kk_ref, ka_ref, yf_ref, yb_ref, s_ref):
    @pl.when(pl.program_id(0) == 0)
    def _():
        s_ref[...] = jnp.zeros_like(s_ref)

    k_k = kk_ref[...]
    k_a = ka_ref[...]
    ops = []
    for d, (r_ref, k_ref, v_ref, l_ref) in enumerate(
            ((rf_ref, kf_ref, vf_ref, lf_ref), (rb_ref, kb_ref, vb_ref, lb_ref))):
        k = k_ref[...]
        kk = k * k_k
        kk = _per_head(kk, lambda x: x / jnp.maximum(jnp.sqrt(jnp.sum(x * x, axis=-1, keepdims=True)), 1e-12))
        ops.append(_rwkv_chunk_operands(d, r_ref[...], k, v_ref[...], l_ref[...], dbase_ref, dup_ref, ibase_ref,
                                        iup_ref, kk, k_a))
    _rwkv_chunk_update(ops, s_ref, (yf_ref, yb_ref), (0, 1))


def _rwkv_scan(p, decay_base, decay_up, iclr_base, iclr_up, k_k, k_a, n_ctx):
    n_rows = p.shape[0]
    c = RWKV_CHUNK
    n_chunks = n_rows // c
    ctx_chunks = n_ctx // c

    def fwd(g):
        return g

    def bwd(g):
        return jnp.where(g < ctx_chunks, ctx_chunks - 1 - g, ctx_chunks + n_chunks - 1 - g)

    def specs(order):
        return [pl.BlockSpec((c, RWKV_W), lambda g: (order(g), COL_R // RWKV_W)),
                pl.BlockSpec((c, RWKV_W), lambda g: (order(g), COL_RK // RWKV_W)),
                pl.BlockSpec((c, RWKV_W), lambda g: (order(g), COL_RV // RWKV_W)),
                pl.BlockSpec((c, LR_W), lambda g: (order(g), COL_LR // LR_W))]

    full = lambda shape: pl.BlockSpec(shape, lambda g: (0,) * len(shape))
    return pl.pallas_call(
        _rwkv_scan_kernel,
        grid=(n_chunks,),
        in_specs=specs(fwd) + specs(bwd) + [
            full((2, RWKV_W)), full((2, DECAY_RANK, RWKV_W)), full((2, RWKV_W)), full((2, ICLR_RANK, RWKV_W)),
            full((1, RWKV_W)), full((1, RWKV_W))],
        out_specs=[pl.BlockSpec((c, RWKV_W), lambda g: (fwd(g), 0)),
                   pl.BlockSpec((c, RWKV_W), lambda g: (bwd(g), 0))],
        out_shape=[jax.ShapeDtypeStruct((n_rows, RWKV_W), F32)] * 2,
        scratch_shapes=[pltpu.VMEM((2, RWKV_HEADS // HEADS_PER_GROUP, HEADS_PER_GROUP * HEAD_DIM,
                                   HEADS_PER_GROUP * HEAD_DIM), F32)],
        compiler_params=_params(1),
        name="rwkv7_chunk_scan",
    )(p, p, p, p, p, p, p, p, decay_base, decay_up, iclr_base, iclr_up,
      k_k.reshape(1, RWKV_W), k_a.reshape(1, RWKV_W))


def _head_sums(x):
    gw = HEADS_PER_GROUP * HEAD_DIM
    bi = lax.broadcasted_iota(jnp.int32, (gw, gw), 0) // HEAD_DIM
    bj = lax.broadcasted_iota(jnp.int32, (gw, gw), 1) // HEAD_DIM
    ones = (bi == bj).astype(BF16)
    hi = x.astype(BF16)
    lo = (x - hi.astype(F32)).astype(BF16)
    parts = []
    for g in range(x.shape[1] // gw):
        gs = slice(g * gw, (g + 1) * gw)
        parts.append(_dot(hi[:, gs], ones) + _dot(lo[:, gs], ones))
    return jnp.concatenate(parts, axis=1)


def _rwkv_out_kernel(yf_ref, yb_ref, r_ref, k_ref, v_ref, lr_ref, ibase_ref, iup_ref, ka_ref, rk_ref,
                     lw_ref, lb_ref, gup_ref, o_ref):
    y = yf_ref[...] + yb_ref[...]
    r = r_ref[...]
    k = k_ref[...]
    v = v_ref[...]
    lr = lr_ref[...]
    k_a = ka_ref[...]

    yc = y - _head_sums(y) * (1.0 / HEAD_DIM)
    var = _head_sums(yc * yc) * (1.0 / HEAD_DIM)
    yn = yc * lax.rsqrt(var + RWKV_GN_EPS) * lw_ref[...] + lb_ref[...]
    kd_sum = jnp.zeros_like(k)
    for d in range(2):
        ad = lr[:, 2 * DECAY_RANK + d * ICLR_RANK:2 * DECAY_RANK + (d + 1) * ICLR_RANK]
        a = _iclr(ad, ibase_ref[d:d + 1, :], iup_ref[d])
        kd_sum = kd_sum + k * (1.0 + (a - 1.0) * k_a)
    rkk = r * kd_sum * rk_ref[...]
    bonus = _head_sums(rkk) * v
    gd = lr[:, 4 * DECAY_RANK:4 * DECAY_RANK + GATE_RANK]
    gate = _dot(jax.nn.sigmoid(gd).astype(BF16), gup_ref[...].astype(BF16))
    o_ref[...] = ((yn + bonus) * gate).astype(o_ref.dtype)


def _rwkv_out(yf, yb, p, iclr_base, iclr_up, k_a, r_k, lnx_w, lnx_b, gate_up):
    n_rows = p.shape[0]
    tm = 256
    row = lambda w, cb: pl.BlockSpec((tm, w), lambda i, cb=cb: (i, cb))
    full = lambda shape: pl.BlockSpec(shape, lambda i: (0,) * len(shape))
    vec = lambda t: t.reshape(1, RWKV_W)
    return pl.pallas_call(
        _rwkv_out_kernel,
        grid=(n_rows // tm,),
        in_specs=[row(RWKV_W, 0), row(RWKV_W, 0), row(RWKV_W, COL_R // RWKV_W), row(RWKV_W, COL_RK // RWKV_W),
                  row(RWKV_W, COL_RV // RWKV_W), row(LR_W, COL_LR // LR_W),
                  full((2, RWKV_W)), full((2, ICLR_RANK, RWKV_W)), full((1, RWKV_W)), full((1, RWKV_W)),
                  full((1, RWKV_W)), full((1, RWKV_W)), full((GATE_RANK, RWKV_W))],
        out_specs=pl.BlockSpec((tm, RWKV_W), lambda i: (i, 0)),
        out_shape=jax.ShapeDtypeStruct((n_rows, RWKV_W), BF16),
        compiler_params=_params(1),
        name="rwkv7_out",
    )(yf, yb, p, p, p, p, iclr_base, iclr_up, vec(k_a), vec(r_k), vec(lnx_w), vec(lnx_b), gate_up)


def _out_proj_kernel(a_ref, b_ref, c_ref, wa_ref, wb_ref, wc_ref, x_ref, gate_ref, o_ref, *, n_ctx, tm):
    acc = _dot(a_ref[...], wa_ref[...]) + _dot(b_ref[...], wb_ref[...]) + _dot(c_ref[...], wc_ref[...])
    row = pl.program_id(0) * tm + lax.broadcasted_iota(jnp.int32, (tm, 1), 0)
    gate = jnp.where(row < n_ctx, gate_ref[1:2, :], gate_ref[0:1, :])
    o_ref[...] = x_ref[...] + gate * acc


def _out_proj(a, b, c, w_out_bf16, xs, gate, n_ctx):
    n_rows, d = xs.shape
    tm = _row_tile(n_rows, (1280, 1024, 512, 256))
    tn = 512
    wa, wb, wc = w_out_bf16[:CONV_W], w_out_bf16[CONV_W:CONV_W + ATT_W], w_out_bf16[CONV_W + ATT_W:]
    return pl.pallas_call(
        functools.partial(_out_proj_kernel, n_ctx=n_ctx, tm=tm),
        grid=(n_rows // tm, d // tn),
        in_specs=[pl.BlockSpec((tm, CONV_W), lambda i, j: (i, 0)),
                  pl.BlockSpec((tm, ATT_W), lambda i, j: (i, 0)),
                  pl.BlockSpec((tm, RWKV_W), lambda i, j: (i, 0)),
                  pl.BlockSpec((CONV_W, tn), lambda i, j: (0, j)),
                  pl.BlockSpec((ATT_W, tn), lambda i, j: (0, j)),
                  pl.BlockSpec((RWKV_W, tn), lambda i, j: (0, j)),
                  pl.BlockSpec((tm, tn), lambda i, j: (i, j)),
                  pl.BlockSpec((8, tn), lambda i, j: (0, j))],
        out_specs=pl.BlockSpec((tm, tn), lambda i, j: (i, j)),
        out_shape=jax.ShapeDtypeStruct((n_rows, d), F32),
        compiler_params=_params(2),
        name="out_proj_residual",
    )(a, b, c, wa, wb, wc, xs, gate)


def _first_max(x, idx):
    m = jnp.max(x, axis=0, keepdims=True)
    first = jnp.min(jnp.where(x == m, idx, N_EXPERTS), axis=0, keepdims=True)
    return m, first


def _router_kernel(x_ref, g_ref, mod_ref, rw_ref, rb_ref, h_ref, idx_ref, gate_ref, rank_ref, cnt_ref, slots_ref,
                   base_ref, *, n_ctx, tm):
    @pl.when(pl.program_id(0) == 0)
    def _():
        base_ref[...] = jnp.zeros_like(base_ref)

    slots_ref[...] = jnp.zeros_like(slots_ref)

    h = _norm_mod(x_ref[...], g_ref[...], mod_ref[...], pl.program_id(0) * tm, n_ctx)
    h_ref[...] = h
    logits = _dot_nt(rw_ref[...], h, HIGHEST)
    scores = jax.nn.sigmoid(logits)
    sel = scores + rb_ref[...]
    eidx = lax.broadcasted_iota(jnp.int32, sel.shape, 0)
    best = best_score = None
    for g in range(N_EXPERT_GROUPS):
        rows = slice(g * EXPERTS_PER_GROUP, (g + 1) * EXPERTS_PER_GROUP)
        x = sel[rows]
        xi = g * EXPERTS_PER_GROUP + lax.broadcasted_iota(jnp.int32, x.shape, 0)
        m1, i1 = _first_max(x, xi)
        m2, _ = _first_max(jnp.where(xi == i1, -jnp.inf, x), xi)
        score = m1 + m2
        if g == 0:
            best, best_score = jnp.zeros_like(i1), score
        else:
            better = score > best_score
            best = jnp.where(better, g, best)
            best_score = jnp.where(better, score, best_score)
    masked = jnp.where(eidx // EXPERTS_PER_GROUP == best, sel, NEG_INF)
    _, e1 = _first_max(masked, eidx)
    _, e2 = _first_max(jnp.where(eidx == e1, -jnp.inf, masked), eidx)
    g1 = jnp.sum(jnp.where(eidx == e1, scores, 0.0), axis=0, keepdims=True)
    g2 = jnp.sum(jnp.where(eidx == e2, scores, 0.0), axis=0, keepdims=True)
    idx_ref[0:1, :] = e1
    idx_ref[1:2, :] = e2
    gate_ref[0:1, :] = g1 / (g1 + g2)
    gate_ref[1:2, :] = g2 / (g1 + g2)

    si = lax.broadcasted_iota(jnp.int32, (tm, tm), 0)
    ti = lax.broadcasted_iota(jnp.int32, (tm, tm), 1)
    prefix = (si <= ti).astype(BF16)
    base = base_ref[...]
    for kth, e in enumerate((e1, e2)):
        hit = eidx == e
        seen = _dot(hit.astype(BF16), prefix)
        rank = jnp.sum(jnp.where(hit, seen - 1.0 + base, 0.0), axis=0, keepdims=True)
        rank_ref[kth:kth + 1, :] = rank.astype(jnp.int32)
        base = base + seen[:, tm - 1:tm]
    base_ref[...] = base
    cnt_ref[...] = base.astype(jnp.int32)


def _route(xs, g, mod, router_w, router_b, n_ctx, cap):
    n_rows, d = xs.shape
    tm = 256
    n_tiles = n_rows // tm
    slab = -(-cap // (n_tiles * SUBLANES)) * SUBLANES
    return pl.pallas_call(
        functools.partial(_router_kernel, n_ctx=n_ctx, tm=tm),
        grid=(n_rows // tm,),
        in_specs=[pl.BlockSpec((tm, d), lambda i: (i, 0)),
                  pl.BlockSpec((1, d), lambda i: (0, 0)),
                  pl.BlockSpec((8, d), lambda i: (0, 0)),
                  pl.BlockSpec((N_EXPERTS, d), lambda i: (0, 0)),
                  pl.BlockSpec((N_EXPERTS, 1), lambda i: (0, 0))],
        out_specs=[pl.BlockSpec((tm, d), lambda i: (i, 0)),
                   pl.BlockSpec((TOP_K, tm), lambda i: (0, i)),
                   pl.BlockSpec((TOP_K, tm), lambda i: (0, i)),
                   pl.BlockSpec((TOP_K, tm), lambda i: (0, i)),
                   pl.BlockSpec((N_EXPERTS, 1), lambda i: (0, 0)),
                   pl.BlockSpec((slab, d), lambda i: (i, 0))],
        out_shape=[jax.ShapeDtypeStruct((n_rows, d), F32),
                   jax.ShapeDtypeStruct((TOP_K, n_rows), jnp.int32),
                   jax.ShapeDtypeStruct((TOP_K, n_rows), F32),
                   jax.ShapeDtypeStruct((TOP_K, n_rows), jnp.int32),
                   jax.ShapeDtypeStruct((N_EXPERTS, 1), jnp.int32),
                   jax.ShapeDtypeStruct((n_tiles * slab, d), F32)],
        scratch_shapes=[pltpu.VMEM((N_EXPERTS, 1), F32)],
        compiler_params=_params(1),
        name="moe_norm_route",
    )(xs, g.reshape(1, d), mod, router_w.T, router_b.reshape(N_EXPERTS, 1))


def _row_copy(src_ref, src_row, dst_ref, dst_row, sem):
    return pltpu.make_async_copy(src_ref.at[pl.ds(src_row, 1), :], dst_ref.at[pl.ds(dst_row, 1), :], sem)


def _dispatch_kernel(dest_ref, h_ref, init_ref, xg_ref, sem, *, tm):
    del init_ref

    def start(t, carry):
        for k in range(TOP_K):
            _row_copy(h_ref, t, xg_ref, dest_ref[0, k, t], sem).start()
        return carry

    lax.fori_loop(0, tm, start, 0, unroll=DMA_UNROLL)
    for _ in range(TOP_K):
        pltpu.make_async_copy(h_ref, xg_ref.at[pl.ds(0, tm), :], sem).wait()


def _dispatch(hp, dest_tiles, slots, tm):
    n_rows, w = hp.shape
    return pl.pallas_call(
        functools.partial(_dispatch_kernel, tm=tm),
        grid=(n_rows // tm,),
        in_specs=[pl.BlockSpec((1, TOP_K, tm), lambda i: (i, 0, 0), memory_space=pltpu.SMEM),
                  pl.BlockSpec((tm, w), lambda i: (i, 0)),
                  pl.BlockSpec(memory_space=pl.ANY)],
        out_specs=pl.BlockSpec(memory_space=pl.ANY),
        out_shape=jax.ShapeDtypeStruct(slots.shape, slots.dtype),
        scratch_shapes=[pltpu.SemaphoreType.DMA(())],
        input_output_aliases={2: 0},
        compiler_params=_params(1),
        name="moe_dispatch_rows",
    )(dest_tiles, hp, slots)


def _expert_kernel(be_ref, nused_ref, next_ref, x_ref, wg_hbm, wu_hbm, wd_hbm, o_ref,
                   sg_ref, su_ref, sd_ref, wgb_ref, wub_ref, wdb_ref, sems, *, layer):
    i = pl.program_id(0)
    used = i < nused_ref[0]
    expert = be_ref[i]
    new_expert = (i == 0) | (expert != be_ref[jnp.maximum(i - 1, 0)])

    def weight_copies(e):
        return (pltpu.make_async_copy(wg_hbm.at[layer, e], sg_ref, sems.at[0]),
                pltpu.make_async_copy(wu_hbm.at[layer, e], su_ref, sems.at[1]),
                pltpu.make_async_copy(wd_hbm.at[layer, e], sd_ref, sems.at[2]))

    @pl.when(used & (i == 0))
    def _():
        for cp in weight_copies(expert):
            cp.start()

    @pl.when(used & new_expert)
    def _():
        for cp in weight_copies(expert):
            cp.wait()
        for r0 in range(0, D_MODEL, CAST_ROWS):
            rows = slice(r0, r0 + CAST_ROWS)
            wgb_ref[rows, :] = sg_ref[rows, :].astype(BF16)
            wub_ref[rows, :] = su_ref[rows, :].astype(BF16)
        for r0 in range(0, D_EXPERT, CAST_ROWS):
            rows = slice(r0, r0 + CAST_ROWS)
            wdb_ref[rows, :] = sd_ref[rows, :].astype(BF16)
        following = next_ref[expert]

        @pl.when(following >= 0)
        def _():
            for cp in weight_copies(following):
                cp.start()

    @pl.when(used)
    def _():
        x = x_ref[...].astype(BF16)
        gate = _dot(x, wgb_ref[...])
        up = _dot(x, wub_ref[...])
        act = (gate * jax.nn.sigmoid(gate) * up).astype(BF16)
        o_ref[...] = _dot(act, wdb_ref[...])

    @pl.when(i >= nused_ref[0])
    def _():
        o_ref[...] = jnp.zeros_like(o_ref)


def _expert_ffn(xg, block_e, n_used, next_expert, wg, wu, wd, layer, cap, tm):
    d = xg.shape[1]
    hbm = pl.BlockSpec(memory_space=pl.ANY)
    grid_spec = pltpu.PrefetchScalarGridSpec(
        num_scalar_prefetch=3,
        grid=(cap // tm,),
        in_specs=[pl.BlockSpec((tm, d), lambda i, be, nu, nx: (i, 0)), hbm, hbm, hbm],
        out_specs=pl.BlockSpec((tm, d), lambda i, be, nu, nx: (i, 0)),
        scratch_shapes=[pltpu.VMEM((d, D_EXPERT), F32), pltpu.VMEM((d, D_EXPERT), F32),
                        pltpu.VMEM((D_EXPERT, d), F32),
                        pltpu.VMEM((d, D_EXPERT), BF16), pltpu.VMEM((d, D_EXPERT), BF16),
                        pltpu.VMEM((D_EXPERT, d), BF16),
                        pltpu.SemaphoreType.DMA((3,))],
    )
    return pl.pallas_call(
        functools.partial(_expert_kernel, layer=layer),
        grid_spec=grid_spec,
        out_shape=jax.ShapeDtypeStruct((cap, d), F32),
        compiler_params=_params(1),
        name="moe_expert_ffn",
    )(block_e, n_used, next_expert, xg, wg, wu, wd)


def _combine_kernel(dest_ref, x_ref, gt_ref, g2_ref, fg_ref, yb_ref, o_ref, ybuf_ref, sem,
                    *, n_ctx, tm, final_norm, first_tile):
    def start(t, carry):
        for k in range(TOP_K):
            _row_copy(yb_ref, dest_ref[0, k, t], ybuf_ref.at[k], t, sem).start()
        return carry

    lax.fori_loop(0, tm, start, 0, unroll=DMA_UNROLL)
    for k in range(TOP_K):
        pltpu.make_async_copy(yb_ref.at[pl.ds(0, tm), :], ybuf_ref.at[k], sem).wait()

    is_ctx = (pl.program_id(0) + first_tile) * tm < n_ctx
    g2 = jnp.where(is_ctx, g2_ref[1:2, :], g2_ref[0:1, :])
    gt = gt_ref[...]
    y = gt[:, 0:1] * ybuf_ref[0] + gt[:, 1:2] * ybuf_ref[1]
    x = x_ref[...] + g2 * y
    if final_norm:
        ms = jnp.mean(x * x, axis=-1, keepdims=True)
        x = x * lax.rsqrt(ms + RMS_EPS) * fg_ref[...]
    o_ref[...] = x


def _combine(xs, yb, dest_tiles, gates_t, g2, final_g, n_ctx, final_norm, tm):
    n_rows, d = xs.shape
    skip = n_ctx // tm if final_norm else 0
    return pl.pallas_call(
        functools.partial(_combine_kernel, n_ctx=n_ctx, tm=tm, final_norm=final_norm, first_tile=skip),
        grid=(n_rows // tm - skip,),
        in_specs=[pl.BlockSpec((1, TOP_K, tm), lambda i: (i + skip, 0, 0), memory_space=pltpu.SMEM),
                  pl.BlockSpec((tm, d), lambda i: (i + skip, 0)),
                  pl.BlockSpec((tm, TOP_K), lambda i: (i + skip, 0)),
                  pl.BlockSpec((8, d), lambda i: (0, 0)), pl.BlockSpec((1, d), lambda i: (0, 0)),
                  pl.BlockSpec(memory_space=pl.ANY)],
        out_specs=pl.BlockSpec((tm, d), lambda i: (i, 0)),
        out_shape=jax.ShapeDtypeStruct((n_rows - skip * tm, d), F32),
        scratch_shapes=[pltpu.VMEM((TOP_K, tm, yb.shape[1]), yb.dtype), pltpu.SemaphoreType.DMA(())],
        compiler_params=_params(1),
        name="moe_combine_residual",
    )(dest_tiles, xs, gates_t, g2, final_g.reshape(1, d), yb)


def _moe(xs, g, mod, g2, router_w, router_b, wg, wu, wd, layer, final_g, n_ctx, final_norm):
    n_rows, d = xs.shape
    tm = 256
    n_asg = n_rows * TOP_K
    n_blk = n_asg // tm + N_EXPERTS
    cap = n_blk * tm
    h, idx, gates, rank, counts, slots = _route(xs, g, mod, router_w, router_b, n_ctx, cap)
    counts = counts[:, 0]
    padded = (counts + tm - 1) // tm * tm
    pends = jnp.cumsum(padded)
    pstarts = pends - padded
    seg_start = jnp.sum(jnp.where(idx[..., None] == jnp.arange(N_EXPERTS), pstarts, 0), axis=-1)
    dest = seg_start + rank
    dest_tiles = dest.reshape(TOP_K, n_rows // tm, tm).transpose(1, 0, 2).astype(jnp.int32)
    block_e = jnp.minimum(jnp.sum(pends[None, :] <= (jnp.arange(n_blk) * tm)[:, None], axis=1), N_EXPERTS - 1)
    n_used = (pends[-1] // tm).reshape(1)
    xg = _dispatch(h, dest_tiles, slots, tm)
    eid = jnp.arange(N_EXPERTS)
    later = (eid[None, :] > eid[:, None]) & (counts[None, :] > 0)
    next_expert = jnp.min(jnp.where(later, eid[None, :], N_EXPERTS), axis=1)
    next_expert = jnp.where(next_expert == N_EXPERTS, -1, next_expert).astype(jnp.int32)
    yb = _expert_ffn(xg, block_e.astype(jnp.int32), n_used.astype(jnp.int32), next_expert, wg, wu, wd, layer,
                     cap, tm)
    return _combine(xs, yb, dest_tiles, gates.T, g2, final_g, n_ctx, final_norm, tm)


def _mod_rows(mod_l, lat_chunks, ctx_chunks):
    d = D_MODEL
    rows = [mod_l[0, c * d:(c + 1) * d] for c in lat_chunks] + [mod_l[1, c * d:(c + 1) * d] for c in ctx_chunks]
    out = jnp.zeros((8, d), F32)
    return out.at[:len(rows)].set(jnp.stack(rows))


def kernel(x, c, ctx, c_ctx, ada_w, ada_b, norm1_g, norm2_g, w_in, w_out, conv_w, attn_sinks, decay_base,
           decay_up, iclr_base, iclr_up, gate_up, k_k, k_a, r_k, lnx_w, lnx_b, router_w, router_b,
           expert_gate, expert_up, expert_down, final_norm_g):
    bsz, n_lat, d = x.shape
    n_ctx = ctx.shape[1]
    depth = ada_w.shape[0]
    assert bsz == 1 and d == D_MODEL and n_ctx % NORM_ROWS == 0 and n_lat % NORM_ROWS == 0

    xs = jnp.concatenate([ctx[0], x[0]], axis=0)
    mods = _ada_mod(c, c_ctx, ada_w, ada_b)
    cos, sin = _rope_tables(n_ctx, n_lat)

    for l in range(depth):
        last = l == depth - 1
        mod1 = _mod_rows(mods[l], (0, 1), (0, 1))
        gate1 = _mod_rows(mods[l], (2,), (2,))
        mod2 = _mod_rows(mods[l], (3, 4), (3, 4))
        gate2 = _mod_rows(mods[l], (5,), (5,))

        p = _norm_mod_matmul(xs, norm1_g[l], mod1, _in_proj_weights(w_in[l]), n_ctx)

        a_mix = _short_conv(p, conv_w[l], n_ctx)
        qr, kr, vb = _rope_qkv(p, cos, sin)
        b_mix = _window_attention(qr, kr, vb, attn_sinks[l], n_ctx, n_lat)
        yf, yb = _rwkv_scan(p, decay_base[l], decay_up[l], iclr_base[l], iclr_up[l], k_k[l], k_a[l], n_ctx)
        c_mix = _rwkv_out(yf, yb, p, iclr_base[l], iclr_up[l], k_a[l], r_k[l].reshape(-1), lnx_w[l], lnx_b[l],
                          gate_up[l])
        xs = _out_proj(a_mix, b_mix, c_mix, w_out[l].astype(BF16), xs, gate1, n_ctx)

        xs = _moe(xs, norm2_g[l], mod2, gate2, router_w, router_b, expert_gate, expert_up, expert_down, l,
                  final_norm_g, n_ctx, last)
    return xs.reshape(bsz, n_lat, d)
```

```python
import functools

import numpy as np
import jax
import jax.numpy as jnp
from jax import lax
from jax.experimental import pallas as pl
from jax.experimental.pallas import tpu as pltpu

F32 = jnp.float32
BF16 = jnp.bfloat16
HIGHEST = lax.Precision.HIGHEST

D_MODEL = 2048
GRID_W = 64
CONV_W = D_MODEL // 4
CONV_K = 3
HEAD_DIM = 64
ATT_HEADS = 12
ATT_KV_HEADS = 4
ATT_GROUP = ATT_HEADS // ATT_KV_HEADS
ATT_W = ATT_HEADS * HEAD_DIM
ATT_KV_W = ATT_KV_HEADS * HEAD_DIM
RWKV_HEADS = 12
RWKV_W = RWKV_HEADS * HEAD_DIM
WINDOW = 128
ATT_BLOCK = 128
ROPE_THETA = 10000.0
ROPE_FREQS = HEAD_DIM // 4
DECAY_RANK = 64
ICLR_RANK = 64
GATE_RANK = 128
RWKV_GN_EPS = 64e-5
N_EXPERTS = 32
N_EXPERT_GROUPS = 4
EXPERTS_PER_GROUP = N_EXPERTS // N_EXPERT_GROUPS
TOP_K = 2
D_EXPERT = 768
RMS_EPS = 1e-6
NEG_INF = -1e30

COL_Q = 0
COL_R = 768
COL_RK = 1536
COL_RV = 2304
COL_CB = 3072
COL_CC = 3584
COL_CH = 4096
COL_K = 4608
COL_V = 4864
COL_LR = 5120
LR_W = 512
IN_W_PAD = 5632

RWKV_CHUNK = 64
HEADS_PER_GROUP = 4
NORM_ROWS = 256
SUBLANES = 8
LANES = 128
CAST_ROWS = 256
DMA_UNROLL = 8
VMEM_LIMIT = 56 * 1024 * 1024


def _params(n_axes):
    return pltpu.CompilerParams(dimension_semantics=("arbitrary",) * n_axes,
                                vmem_limit_bytes=VMEM_LIMIT)


def _row_tile(n_rows, candidates):
    for t in candidates:
        if n_rows % t == 0:
            return t
    raise ValueError(f"no row tile for {n_rows}")


def _in_proj_weights(w):
    o_cb, o_cc, o_ch = 0, 512, 1024
    o_q, o_k, o_v = 1536, 2304, 2560
    o_r, o_rk, o_rv = 2816, 3584, 4352
    o_lr = 5120
    segs = [(o_q, 768), (o_r, 768), (o_rk, 768), (o_rv, 768), (o_cb, 512), (o_cc, 512), (o_ch, 512),
            (o_k, 256), (o_v, 256), (o_lr, 384)]
    parts = [w[:, o:o + n].astype(BF16) for o, n in segs]
    parts.append(jnp.zeros((w.shape[0], IN_W_PAD - sum(n for _, n in segs)), BF16))
    return jnp.concatenate(parts, axis=1)


def _ada_kernel(s_ref, w_ref, b_ref, o_ref):
    d, tn = w_ref.shape[1], w_ref.shape[2]
    reps = tn // LANES
    acc = [jnp.zeros((1, tn), F32) for _ in range(2)]
    for k0 in range(0, d, NORM_ROWS):
        wk = w_ref[0, k0:k0 + NORM_ROWS, :]
        for r in range(2):
            sb = jnp.concatenate([s_ref[r, k0:k0 + NORM_ROWS, :]] * reps, axis=1)
            acc[r] = acc[r] + jnp.sum(wk * sb, axis=0, keepdims=True)
    o_ref[0] = jnp.zeros(o_ref.shape[1:], F32)
    for r in range(2):
        o_ref[0, r:r + 1, :] = acc[r] + b_ref[0]


def _ada_mod(c, c_ctx, ada_w, ada_b):
    depth, d, n = ada_w.shape
    s = jnp.stack([c[0], c_ctx])
    s = s * jax.nn.sigmoid(s)
    s = jnp.broadcast_to(s[:, :, None], (2, d, LANES))
    tn = 1024
    return pl.pallas_call(
        _ada_kernel,
        grid=(depth, n // tn),
        in_specs=[pl.BlockSpec((2, d, LANES), lambda l, j: (0, 0, 0)),
                  pl.BlockSpec((1, d, tn), lambda l, j: (l, 0, j)),
                  pl.BlockSpec((1, 1, tn), lambda l, j: (l, 0, j))],
        out_specs=pl.BlockSpec((1, 8, tn), lambda l, j: (l, 0, j)),
        out_shape=jax.ShapeDtypeStruct((depth, 8, n), F32),
        compiler_params=_params(2),
        name="ada_mod",
    )(s, ada_w, ada_b.reshape(depth, 1, n))


def _norm_mod(x, g, mod, row0, n_ctx):
    ms = jnp.mean(x * x, axis=-1, keepdims=True)
    y = x * lax.rsqrt(ms + RMS_EPS) * g
    is_ctx = row0 < n_ctx
    shift = jnp.where(is_ctx, mod[2:3, :], mod[0:1, :])
    scale = jnp.where(is_ctx, mod[3:4, :], mod[1:2, :])
    return y * (1.0 + scale) + shift


def _nmm_kernel(x_ref, g_ref, mod_ref, w_ref, o_ref, h_ref, *, n_ctx, tm):
    i = pl.program_id(0)

    @pl.when(pl.program_id(1) == 0)
    def _():
        for r0 in range(0, tm, NORM_ROWS):
            rows = slice(r0, r0 + NORM_ROWS)
            h_ref[rows, :] = _norm_mod(x_ref[rows, :], g_ref[...], mod_ref[...], i * tm + r0, n_ctx).astype(BF16)

    o_ref[...] = jnp.dot(h_ref[...], w_ref[...], preferred_element_type=F32)


def _norm_mod_matmul(xs, g, mod, w_bf16, n_ctx):
    n_rows, d = xs.shape
    n_out = w_bf16.shape[1]
    tm = _row_tile(n_rows, (1280, 1024, 512, 256))
    tn = 512
    return pl.pallas_call(
        functools.partial(_nmm_kernel, n_ctx=n_ctx, tm=tm),
        grid=(n_rows // tm, n_out // tn),
        in_specs=[pl.BlockSpec((tm, d), lambda i, j: (i, 0)),
                  pl.BlockSpec((1, d), lambda i, j: (0, 0)),
                  pl.BlockSpec((8, d), lambda i, j: (0, 0)),
                  pl.BlockSpec((d, tn), lambda i, j: (0, j))],
        out_specs=pl.BlockSpec((tm, tn), lambda i, j: (i, j)),
        out_shape=jax.ShapeDtypeStruct((n_rows, n_out), F32),
        scratch_shapes=[pltpu.VMEM((tm, d), BF16)],
        compiler_params=_params(2),
        name="norm_mod_in_proj",
    )(xs, g.reshape(1, d), mod, w_bf16)


def _conv_kernel(cb_ref, cc_ref, ch_ref, ccp_ref, chp_ref, ccn_ref, chn_ref, w_ref, o_ref, *, n_ctx, n_rows, tm):
    i = pl.program_id(0)
    u = cc_ref[...] * ch_ref[...]
    u_prev_row = ccp_ref[7:8, :] * chp_ref[7:8, :]
    u_next_row = ccn_ref[0:1, :] * chn_ref[0:1, :]
    loc = lax.broadcasted_iota(jnp.int32, (tm, 1), 0)
    row = i * tm + loc
    up = jnp.where(loc == 0, u_prev_row, pltpu.roll(u, 1, axis=0))
    dn = jnp.where(loc == tm - 1, u_next_row, pltpu.roll(u, tm - 1, axis=0))
    up = jnp.where((row == 0) | (row == n_ctx), 0.0, up)
    dn = jnp.where((row == n_ctx - 1) | (row == n_rows - 1), 0.0, dn)
    w = w_ref[...]
    y = w[0:1, :] * up + w[1:2, :] * u + w[2:3, :] * dn
    o_ref[...] = (cb_ref[...] * y).astype(o_ref.dtype)


def _short_conv(p, conv_w, n_ctx):
    n_rows = p.shape[0]
    tm = _row_tile(n_rows, (1280, 1024, 512, 256))
    r8 = tm // 8
    last8 = n_rows // 8 - 1
    wpad = jnp.zeros((8, CONV_W), F32).at[:CONV_K].set(conv_w)
    blk = lambda c: pl.BlockSpec((tm, CONV_W), lambda i, c=c: (i, c))
    prev = lambda c: pl.BlockSpec((8, CONV_W), lambda i, c=c: (jnp.maximum(i * r8 - 1, 0), c))
    nxt = lambda c: pl.BlockSpec((8, CONV_W), lambda i, c=c: (jnp.minimum((i + 1) * r8, last8), c))
    cb, cc, ch = COL_CB // CONV_W, COL_CC // CONV_W, COL_CH // CONV_W
    return pl.pallas_call(
        functools.partial(_conv_kernel, n_ctx=n_ctx, n_rows=n_rows, tm=tm),
        grid=(n_rows // tm,),
        in_specs=[blk(cb), blk(cc), blk(ch), prev(cc), prev(ch), nxt(cc), nxt(ch),
                  pl.BlockSpec((8, CONV_W), lambda i: (0, 0))],
        out_specs=pl.BlockSpec((tm, CONV_W), lambda i: (i, 0)),
        out_shape=jax.ShapeDtypeStruct((n_rows, CONV_W), BF16),
        compiler_params=_params(1),
        name="short_conv",
    )(p, p, p, p, p, p, p, wpad)


def _swap_halves(x):
    n = x.shape[1]
    lane = lax.broadcasted_iota(jnp.int32, x.shape, 1)
    fwd = pltpu.roll(x, n - ROPE_FREQS, axis=1)
    bwd = pltpu.roll(x, ROPE_FREQS, axis=1)
    return jnp.where((lane % (2 * ROPE_FREQS)) < ROPE_FREQS, fwd, bwd)


def _rope_kernel(q_ref, k_ref, v_ref, cos_ref, sin_ref, qo_ref, ko_ref, vo_ref):
    cos = cos_ref[...]
    sin = sin_ref[...]
    q = q_ref[...]
    k = k_ref[...]
    cos_q = jnp.concatenate([cos] * (ATT_W // 128), axis=1)
    sin_q = jnp.concatenate([sin] * (ATT_W // 128), axis=1)
    cos_k = jnp.concatenate([cos] * (ATT_KV_W // 128), axis=1)
    sin_k = jnp.concatenate([sin] * (ATT_KV_W // 128), axis=1)
    qo_ref[...] = ((q * cos_q + _swap_halves(q) * sin_q) * (HEAD_DIM ** -0.5)).astype(BF16)
    ko_ref[...] = (k * cos_k + _swap_halves(k) * sin_k).astype(BF16)
    vo_ref[...] = v_ref[...].astype(BF16)


def _rope_tables(n_ctx, n_lat):
    row = jnp.repeat(jnp.arange(n_lat // GRID_W, dtype=jnp.int32), GRID_W)
    col = jnp.arange(n_lat, dtype=jnp.int32) % GRID_W
    inv = ROPE_THETA ** (-jnp.arange(ROPE_FREQS, dtype=F32) / ROPE_FREQS)
    ang_r = row[:, None].astype(F32) * inv[None, :]
    ang_c = col[:, None].astype(F32) * inv[None, :]
    cr, sr, cc, sc = jnp.cos(ang_r), jnp.sin(ang_r), jnp.cos(ang_c), jnp.sin(ang_c)
    cos = jnp.concatenate([cr, cr, cc, cc], axis=1)
    sin = jnp.concatenate([-sr, sr, -sc, sc], axis=1)
    cos = jnp.concatenate([jnp.ones((n_ctx, HEAD_DIM), F32), cos], axis=0)
    sin = jnp.concatenate([jnp.zeros((n_ctx, HEAD_DIM), F32), sin], axis=0)
    return jnp.tile(cos, (1, 2)), jnp.tile(sin, (1, 2))


def _rope_qkv(p, cos, sin):
    n_rows = p.shape[0]
    tm = _row_tile(n_rows, (1280, 1024, 512, 256))
    return pl.pallas_call(
        _rope_kernel,
        grid=(n_rows // tm,),
        in_specs=[pl.BlockSpec((tm, ATT_W), lambda i: (i, COL_Q // ATT_W)),
                  pl.BlockSpec((tm, ATT_KV_W), lambda i: (i, COL_K // ATT_KV_W)),
                  pl.BlockSpec((tm, ATT_KV_W), lambda i: (i, COL_V // ATT_KV_W)),
                  pl.BlockSpec((tm, 128), lambda i: (i, 0)),
                  pl.BlockSpec((tm, 128), lambda i: (i, 0))],
        out_specs=[pl.BlockSpec((tm, ATT_W), lambda i: (i, 0)),
                   pl.BlockSpec((tm, ATT_KV_W), lambda i: (i, 0)),
                   pl.BlockSpec((tm, ATT_KV_W), lambda i: (i, 0))],
        out_shape=[jax.ShapeDtypeStruct((n_rows, ATT_W), BF16),
                   jax.ShapeDtypeStruct((n_rows, ATT_KV_W), BF16),
                   jax.ShapeDtypeStruct((n_rows, ATT_KV_W), BF16)],
        compiler_params=_params(1),
        name="rope_qkv",
    )(p, p, p, cos, sin)


def _attn_kernel(sink_ref, bias_ref, q_ref, kp_ref, kc_ref, kn_ref, vp_ref, vc_ref, vn_ref, kx_ref, vx_ref, o_ref,
                 *, n_ctx, n_blocks):
    i = pl.program_id(0)
    blk = ATT_BLOCK
    ctx_blocks = n_ctx // blk
    ci = lax.broadcasted_iota(jnp.int32, (1, n_ctx + 3 * blk), 1)
    no_prev = i <= ctx_blocks
    no_cur = i < ctx_blocks
    no_next = no_cur | (i == n_blocks - 1)
    dead = (((ci >= n_ctx) & (ci < n_ctx + blk) & no_prev)
            | ((ci >= n_ctx + blk) & (ci < n_ctx + 2 * blk) & no_cur)
            | ((ci >= n_ctx + 2 * blk) & no_next))
    bias = bias_ref[...] + jnp.where(dead, NEG_INF, 0.0)

    q = q_ref[...]
    outs = [None] * ATT_HEADS
    for g in range(ATT_KV_HEADS):
        ks = slice(g * HEAD_DIM, (g + 1) * HEAD_DIM)
        kk = jnp.concatenate([kx_ref[:, ks], kp_ref[:, ks], kc_ref[:, ks], kn_ref[:, ks]], axis=0)
        vv = jnp.concatenate([vx_ref[:, ks], vp_ref[:, ks], vc_ref[:, ks], vn_ref[:, ks]], axis=0)
        heads = [g * ATT_GROUP + j for j in range(ATT_GROUP)]
        qq = jnp.concatenate([q[:, h * HEAD_DIM:(h + 1) * HEAD_DIM] for h in heads], axis=0)
        s = lax.dot_general(qq, kk, (((1,), (1,)), ((), ())), preferred_element_type=F32) + bias
        sink = jnp.concatenate([jnp.full((blk, 1), sink_ref[h], F32) for h in heads], axis=0)
        m = jnp.maximum(jnp.max(s, axis=-1, keepdims=True), sink)
        e = jnp.exp(s - m)
        den = jnp.sum(e, axis=-1, keepdims=True) + jnp.exp(sink - m)
        o = jnp.dot(e.astype(BF16), vv, preferred_element_type=F32) / den
        for j, h in enumerate(heads):
            outs[h] = o[j * blk:(j + 1) * blk, :]
    o_ref[...] = jnp.concatenate(outs, axis=1).astype(o_ref.dtype)


def _window_attention(qr, kr, vb, sinks, n_ctx, n_lat):
    n_rows = qr.shape[0]
    nb = n_rows // ATT_BLOCK
    q_spec = pl.BlockSpec((ATT_BLOCK, ATT_W), lambda i: (i, 0))
    prev = pl.BlockSpec((ATT_BLOCK, ATT_KV_W), lambda i: (jnp.maximum(i - 1, 0), 0))
    cur = pl.BlockSpec((ATT_BLOCK, ATT_KV_W), lambda i: (i, 0))
    nxt = pl.BlockSpec((ATT_BLOCK, ATT_KV_W), lambda i: (jnp.minimum(i + 1, nb - 1), 0))
    ctx = pl.BlockSpec((n_ctx, ATT_KV_W), lambda i: (0, 0))
    n_keys = n_ctx + 3 * ATT_BLOCK
    qi = jnp.arange(ATT_BLOCK)[:, None]
    ci = jnp.arange(n_keys)[None, :]
    in_window = (ci < n_ctx) | (jnp.abs(ci - n_ctx - ATT_BLOCK - qi) <= WINDOW)
    bias = jnp.tile(jnp.where(in_window, 0.0, NEG_INF).astype(F32), (ATT_GROUP, 1))
    return pl.pallas_call(
        functools.partial(_attn_kernel, n_ctx=n_ctx, n_blocks=nb),
        grid=(nb,),
        in_specs=[pl.BlockSpec(memory_space=pltpu.SMEM), pl.BlockSpec(bias.shape, lambda i: (0, 0)),
                  q_spec, prev, cur, nxt, prev, cur, nxt, ctx, ctx],
        out_specs=pl.BlockSpec((ATT_BLOCK, ATT_W), lambda i: (i, 0)),
        out_shape=jax.ShapeDtypeStruct((n_rows, ATT_W), BF16),
        compiler_params=_params(1),
        name="window_attention",
    )(sinks.astype(F32), bias, qr, kr, kr, kr, vb, vb, vb, kr, vb)


def _per_head(x, fn):
    return jnp.concatenate([fn(x[:, h * HEAD_DIM:(h + 1) * HEAD_DIM]) for h in range(RWKV_HEADS)], axis=1)


def _dot_nt(a, b, precision=None):
    return lax.dot_general(a, b, (((1,), (1,)), ((), ())), precision=precision, preferred_element_type=F32)


def _dot_tn(a, b, precision=None):
    return lax.dot_general(a, b, (((0,), (0,)), ((), ())), precision=precision, preferred_element_type=F32)


def _dot(a, b, precision=None):
    return jnp.dot(a, b, precision=precision, preferred_element_type=F32)


def _iclr(ad, ibase, iup):
    return jax.nn.sigmoid(ibase + _dot(ad.astype(BF16), iup.astype(BF16)))


def _block_diag(y, mask):
    return jnp.where(mask, jnp.concatenate([y] * HEADS_PER_GROUP, axis=0), jnp.zeros((), y.dtype))


def _rwkv_chunk_operands(d, r, k, v, lr, dbase_ref, dup_ref, ibase_ref, iup_ref, kk_scale, k_a):
    c = r.shape[0]
    wd = lr[:, d * DECAY_RANK:(d + 1) * DECAY_RANK]
    ad = lr[:, 2 * DECAY_RANK + d * ICLR_RANK:2 * DECAY_RANK + (d + 1) * ICLR_RANK]
    z = dbase_ref[d:d + 1, :] + _dot(jnp.tanh(wd).astype(BF16), dup_ref[d].astype(BF16))
    logw = -np.float32(np.exp(-0.5)) * jax.nn.sigmoid(z)
    a = _iclr(ad, ibase_ref[d:d + 1, :], iup_ref[d])
    kd = k * (1.0 + (a - 1.0) * k_a)
    b = kk_scale * a

    ti = lax.broadcasted_iota(jnp.int32, (c, c), 0)
    si = lax.broadcasted_iota(jnp.int32, (c, c), 1)
    tri = (si <= ti) if d == 0 else (si >= ti)
    cum = _dot(tri.astype(F32), logw, HIGHEST)
    last = c - 1 if d == 0 else 0
    cum_end = cum[last:last + 1, :]
    w_inv = jnp.exp(-cum)
    w_tail = jnp.exp(cum_end - cum)
    return dict(
        x=jnp.concatenate([(jnp.exp(cum - logw) * kk_scale).astype(BF16), (r * jnp.exp(cum)).astype(BF16)], axis=0),
        beta=(b * w_inv).astype(BF16),
        kappa=(kd * w_inv).astype(BF16),
        tail=jnp.concatenate([(kd * w_tail).astype(BF16), (-b * w_tail).astype(BF16)], axis=0),
        w_end=jnp.exp(cum_end),
        v=v.astype(BF16))


def _rwkv_chunk_update(ops, s_ref, y_refs, directions):
    c = RWKV_CHUNK
    gw = HEADS_PER_GROUP * HEAD_DIM
    n_groups = RWKV_HEADS // HEADS_PER_GROUP
    chains = [(d, g) for g in range(n_groups) for d in directions]
    tp = lax.broadcasted_iota(jnp.int32, (c, HEADS_PER_GROUP * c), 0)
    sp = lax.broadcasted_iota(jnp.int32, (c, HEADS_PER_GROUP * c), 1) % c
    before = [sp < tp, sp > tp]
    upto = [sp <= tp, sp >= tp]
    eye = (sp == tp).astype(F32)
    bi = lax.broadcasted_iota(jnp.int32, (gw, gw), 0) // HEAD_DIM
    bj = lax.broadcasted_iota(jnp.int32, (gw, gw), 1) // HEAD_DIM
    diag = bi == bj
    cols = lambda g: slice(g * gw, (g + 1) * gw)

    gram_b, gram_k, sx, s0 = {}, {}, {}, {}
    for ch in chains:
        d, g = ch
        x = ops[d]["x"][:, cols(g)]
        gram_b[ch] = _dot_nt(x, _block_diag(ops[d]["beta"][:, cols(g)], diag))
        gram_k[ch] = _dot_nt(x, _block_diag(ops[d]["kappa"][:, cols(g)], diag))
        s0[ch] = s_ref[d, g]
        sx[ch] = _dot_nt(x, s0[ch].astype(BF16))

    npow = {ch: jnp.where(before[ch[0]], -gram_b[ch][:c], 0.0) for ch in chains}
    tinv = {ch: eye + npow[ch] for ch in chains}
    nb = {ch: npow[ch].astype(BF16) for ch in chains}
    nbd = {ch: _block_diag(nb[ch], diag) for ch in chains}
    for _ in range(5):
        for ch in chains:
            nb[ch] = _dot(nb[ch], nbd[ch]).astype(BF16)
            nbd[ch] = _block_diag(nb[ch], diag)
        for ch in chains:
            tinv[ch] = tinv[ch] + _dot(tinv[ch].astype(BF16), nbd[ch])

    kv = {}
    for ch in chains:
        d, g = ch
        m2 = jnp.concatenate([jnp.where(before[d], gram_k[ch][:c], 0.0), jnp.where(upto[d], gram_k[ch][c:], 0.0)],
                             axis=0)
        kv[ch] = _dot(m2.astype(BF16), _block_diag(ops[d]["v"][:, cols(g)], diag))
    ub = {}
    for ch in chains:
        rhs = sx[ch][:c] + kv[ch][:c]
        ub[ch] = _dot(tinv[ch].astype(BF16), _block_diag(rhs.astype(BF16), diag)).astype(BF16)
    for ch in chains:
        d, g = ch
        rb = jnp.where(upto[d], gram_b[ch][c:], 0.0).astype(BF16)
        y_refs[d][:, cols(g)] = sx[ch][c:] + kv[ch][c:] - _dot(rb, _block_diag(ub[ch], diag))
    for ch in chains:
        d, g = ch
        upd = _dot_tn(jnp.concatenate([ops[d]["v"][:, cols(g)], ub[ch]], axis=0), ops[d]["tail"][:, cols(g)])
        s_ref[d, g] = s0[ch] * ops[d]["w_end"][:, cols(g)] + jnp.where(diag, upd, 0.0)


def _rwkv_scan_kernel(rf_ref, kf_ref, vf_ref, lf_ref, rb_ref, kb_ref, vb_ref, lb_ref,
                      dbase_ref, dup_ref, ibase_ref, iup_ref, kk_ref, ka_ref, yf_ref, yb_ref, s_ref):
    @pl.when(pl.program_id(0) == 0)
    def _():
        s_ref[...] = jnp.zeros_like(s_ref)

    k_k = kk_ref[...]
    k_a = ka_ref[...]
    ops = []
    for d, (r_ref, k_ref, v_ref, l_ref) in enumerate(
            ((rf_ref, kf_ref, vf_ref, lf_ref), (rb_ref, kb_ref, vb_ref, lb_ref))):
        k = k_ref[...]
        kk = k * k_k
        kk = _per_head(kk, lambda x: x / jnp.maximum(jnp.sqrt(jnp.sum(x * x, axis=-1, keepdims=True)), 1e-12))
        ops.append(_rwkv_chunk_operands(d, r_ref[...], k, v_ref[...], l_ref[...], dbase_ref, dup_ref, ibase_ref,
                                        iup_ref, kk, k_a))
    _rwkv_chunk_update(ops, s_ref, (yf_ref, yb_ref), (0, 1))


def _rwkv_scan(p, decay_base, decay_up, iclr_base, iclr_up, k_k, k_a, n_ctx):
    n_rows = p.shape[0]
    c = RWKV_CHUNK
    n_chunks = n_rows // c
    ctx_chunks = n_ctx // c

    def fwd(g):
        return g

    def bwd(g):
        return jnp.where(g < ctx_chunks, ctx_chunks - 1 - g, ctx_chunks + n_chunks - 1 - g)

    def specs(order):
        return [pl.BlockSpec((c, RWKV_W), lambda g: (order(g), COL_R // RWKV_W)),
                pl.BlockSpec((c, RWKV_W), lambda g: (order(g), COL_RK // RWKV_W)),
                pl.BlockSpec((c, RWKV_W), lambda g: (order(g), COL_RV // RWKV_W)),
                pl.BlockSpec((c, LR_W), lambda g: (order(g), COL_LR // LR_W))]

    full = lambda shape: pl.BlockSpec(shape, lambda g: (0,) * len(shape))
    return pl.pallas_call(
        _rwkv_scan_kernel,
        grid=(n_chunks,),
        in_specs=specs(fwd) + specs(bwd) + [
            full((2, RWKV_W)), full((2, DECAY_RANK, RWKV_W)), full((2, RWKV_W)), full((2, ICLR_RANK, RWKV_W)),
            full((1, RWKV_W)), full((1, RWKV_W))],
        out_specs=[pl.BlockSpec((c, RWKV_W), lambda g: (fwd(g), 0)),
                   pl.BlockSpec((c, RWKV_W), lambda g: (bwd(g), 0))],
        out_shape=[jax.ShapeDtypeStruct((n_rows, RWKV_W), F32)] * 2,
        scratch_shapes=[pltpu.VMEM((2, RWKV_HEADS // HEADS_PER_GROUP, HEADS_PER_GROUP * HEAD_DIM,
                                   HEADS_PER_GROUP * HEAD_DIM), F32)],
        compiler_params=_params(1),
        name="rwkv7_chunk_scan",
    )(p, p, p, p, p, p, p, p, decay_base, decay_up, iclr_base, iclr_up,
      k_k.reshape(1, RWKV_W), k_a.reshape(1, RWKV_W))


def _head_sums(x):
    gw = HEADS_PER_GROUP * HEAD_DIM
    bi = lax.broadcasted_iota(jnp.int32, (gw, gw), 0) // HEAD_DIM
    bj = lax.broadcasted_iota(jnp.int32, (gw, gw), 1) // HEAD_DIM
    ones = (bi == bj).astype(BF16)
    hi = x.astype(BF16)
    lo = (x - hi.astype(F32)).astype(BF16)
    parts = []
    for g in range(x.shape[1] // gw):
        gs = slice(g * gw, (g + 1) * gw)
        parts.append(_dot(hi[:, gs], ones) + _dot(lo[:, gs], ones))
    return jnp.concatenate(parts, axis=1)


def _rwkv_out_kernel(yf_ref, yb_ref, r_ref, k_ref, v_ref, lr_ref, ibase_ref, iup_ref, ka_ref, rk_ref,
                     lw_ref, lb_ref, gup_ref, o_ref):
    y = yf_ref[...] + yb_ref[...]
    r = r_ref[...]
    k = k_ref[...]
    v = v_ref[...]
    lr = lr_ref[...]
    k_a = ka_ref[...]

    yc = y - _head_sums(y) * (1.0 / HEAD_DIM)
    var = _head_sums(yc * yc) * (1.0 / HEAD_DIM)
    yn = yc * lax.rsqrt(var + RWKV_GN_EPS) * lw_ref[...] + lb_ref[...]
    kd_sum = jnp.zeros_like(k)
    for d in range(2):
        ad = lr[:, 2 * DECAY_RANK + d * ICLR_RANK:2 * DECAY_RANK + (d + 1) * ICLR_RANK]
        a = _iclr(ad, ibase_ref[d:d + 1, :], iup_ref[d])
        kd_sum = kd_sum + k * (1.0 + (a - 1.0) * k_a)
    rkk = r * kd_sum * rk_ref[...]
    bonus = _head_sums(rkk) * v
    gd = lr[:, 4 * DECAY_RANK:4 * DECAY_RANK + GATE_RANK]
    gate = _dot(jax.nn.sigmoid(gd).astype(BF16), gup_ref[...].astype(BF16))
    o_ref[...] = ((yn + bonus) * gate).astype(o_ref.dtype)


def _rwkv_out(yf, yb, p, iclr_base, iclr_up, k_a, r_k, lnx_w, lnx_b, gate_up):
    n_rows = p.shape[0]
    tm = 256
    row = lambda w, cb: pl.BlockSpec((tm, w), lambda i, cb=cb: (i, cb))
    full = lambda shape: pl.BlockSpec(shape, lambda i: (0,) * len(shape))
    vec = lambda t: t.reshape(1, RWKV_W)
    return pl.pallas_call(
        _rwkv_out_kernel,
        grid=(n_rows // tm,),
        in_specs=[row(RWKV_W, 0), row(RWKV_W, 0), row(RWKV_W, COL_R // RWKV_W), row(RWKV_W, COL_RK // RWKV_W),
                  row(RWKV_W, COL_RV // RWKV_W), row(LR_W, COL_LR // LR_W),
                  full((2, RWKV_W)), full((2, ICLR_RANK, RWKV_W)), full((1, RWKV_W)), full((1, RWKV_W)),
                  full((1, RWKV_W)), full((1, RWKV_W)), full((GATE_RANK, RWKV_W))],
        out_specs=pl.BlockSpec((tm, RWKV_W), lambda i: (i, 0)),
        out_shape=jax.ShapeDtypeStruct((n_rows, RWKV_W), BF16),
        compiler_params=_params(1),
        name="rwkv7_out",
    )(yf, yb, p, p, p, p, iclr_base, iclr_up, vec(k_a), vec(r_k), vec(lnx_w), vec(lnx_b), gate_up)


def _out_proj_kernel(a_ref, b_ref, c_ref, wa_ref, wb_ref, wc_ref, x_ref, gate_ref, o_ref, *, n_ctx, tm):
    acc = _dot(a_ref[...], wa_ref[...]) + _dot(b_ref[...], wb_ref[...]) + _dot(c_ref[...], wc_ref[...])
    row = pl.program_id(0) * tm + lax.broadcasted_iota(jnp.int32, (tm, 1), 0)
    gate = jnp.where(row < n_ctx, gate_ref[1:2, :], gate_ref[0:1, :])
    o_ref[...] = x_ref[...] + gate * acc


def _out_proj(a, b, c, w_out_bf16, xs, gate, n_ctx):
    n_rows, d = xs.shape
    tm = _row_tile(n_rows, (1280, 1024, 512, 256))
    tn = 512
    wa, wb, wc = w_out_bf16[:CONV_W], w_out_bf16[CONV_W:CONV_W + ATT_W], w_out_bf16[CONV_W + ATT_W:]
    return pl.pallas_call(
        functools.partial(_out_proj_kernel, n_ctx=n_ctx, tm=tm),
        grid=(n_rows // tm, d // tn),
        in_specs=[pl.BlockSpec((tm, CONV_W), lambda i, j: (i, 0)),
                  pl.BlockSpec((tm, ATT_W), lambda i, j: (i, 0)),
                  pl.BlockSpec((tm, RWKV_W), lambda i, j: (i, 0)),
                  pl.BlockSpec((CONV_W, tn), lambda i, j: (0, j)),
                  pl.BlockSpec((ATT_W, tn), lambda i, j: (0, j)),
                  pl.BlockSpec((RWKV_W, tn), lambda i, j: (0, j)),
                  pl.BlockSpec((tm, tn), lambda i, j: (i, j)),
                  pl.BlockSpec((8, tn), lambda i, j: (0, j))],
        out_specs=pl.BlockSpec((tm, tn), lambda i, j: (i, j)),
        out_shape=jax.ShapeDtypeStruct((n_rows, d), F32),
        compiler_params=_params(2),
        name="out_proj_residual",
    )(a, b, c, wa, wb, wc, xs, gate)


def _first_max(x, idx):
    m = jnp.max(x, axis=0, keepdims=True)
    first = jnp.min(jnp.where(x == m, idx, N_EXPERTS), axis=0, keepdims=True)
    return m, first


def _router_kernel(x_ref, g_ref, mod_ref, rw_ref, rb_ref, h_ref, idx_ref, gate_ref, rank_ref, cnt_ref, slots_ref,
                   base_ref, *, n_ctx, tm):
    @pl.when(pl.program_id(0) == 0)
    def _():
        base_ref[...] = jnp.zeros_like(base_ref)

    slots_ref[...] = jnp.zeros_like(slots_ref)

    h = _norm_mod(x_ref[...], g_ref[...], mod_ref[...], pl.program_id(0) * tm, n_ctx)
    h_ref[...] = h
    rw = rw_ref[...]
    rw_hi = rw.astype(BF16)
    rw_lo = (rw - rw_hi.astype(F32)).astype(BF16)
    h_hi = h.astype(BF16)
    h_lo = (h - h_hi.astype(F32)).astype(BF16)
    logits = _dot_nt(rw_hi, h_hi) + (_dot_nt(rw_hi, h_lo) + _dot_nt(rw_lo, h_hi))
    scores = jax.nn.sigmoid(logits)
    sel = scores + rb_ref[...]
    eidx = lax.broadcasted_iota(jnp.int32, sel.shape, 0)
    best = best_score = None
    for g in range(N_EXPERT_GROUPS):
        rows = slice(g * EXPERTS_PER_GROUP, (g + 1) * EXPERTS_PER_GROUP)
        x = sel[rows]
        xi = g * EXPERTS_PER_GROUP + lax.broadcasted_iota(jnp.int32, x.shape, 0)
        m1, i1 = _first_max(x, xi)
        m2, _ = _first_max(jnp.where(xi == i1, -jnp.inf, x), xi)
        score = m1 + m2
        if g == 0:
            best, best_score = jnp.zeros_like(i1), score
        else:
            better = score > best_score
            best = jnp.where(better, g, best)
            best_score = jnp.where(better, score, best_score)
    masked = jnp.where(eidx // EXPERTS_PER_GROUP == best, sel, NEG_INF)
    _, e1 = _first_max(masked, eidx)
    _, e2 = _first_max(jnp.where(eidx == e1, -jnp.inf, masked), eidx)
    g1 = jnp.sum(jnp.where(eidx == e1, scores, 0.0), axis=0, keepdims=True)
    g2 = jnp.sum(jnp.where(eidx == e2, scores, 0.0), axis=0, keepdims=True)
    idx_ref[0:1, :] = e1
    idx_ref[1:2, :] = e2
    gate_ref[0:1, :] = g1 / (g1 + g2)
    gate_ref[1:2, :] = g2 / (g1 + g2)

    si = lax.broadcasted_iota(jnp.int32, (tm, tm), 0)
    ti = lax.broadcasted_iota(jnp.int32, (tm, tm), 1)
    prefix = (si <= ti).astype(BF16)
    base = base_ref[...]
    for kth, e in enumerate((e1, e2)):
        hit = eidx == e
        seen = _dot(hit.astype(BF16), prefix)
        rank = jnp.sum(jnp.where(hit, seen - 1.0 + base, 0.0), axis=0, keepdims=True)
        rank_ref[kth:kth + 1, :] = rank.astype(jnp.int32)
        base = base + seen[:, tm - 1:tm]
    base_ref[...] = base
    cnt_ref[...] = base.astype(jnp.int32)


def _route(xs, g, mod, router_w, router_b, n_ctx, cap):
    n_rows, d = xs.shape
    tm = 256
    n_tiles = n_rows // tm
    slab = -(-cap // (n_tiles * SUBLANES)) * SUBLANES
    return pl.pallas_call(
        functools.partial(_router_kernel, n_ctx=n_ctx, tm=tm),
        grid=(n_rows // tm,),
        in_specs=[pl.BlockSpec((tm, d), lambda i: (i, 0)),
                  pl.BlockSpec((1, d), lambda i: (0, 0)),
                  pl.BlockSpec((8, d), lambda i: (0, 0)),
                  pl.BlockSpec((N_EXPERTS, d), lambda i: (0, 0)),
                  pl.BlockSpec((N_EXPERTS, 1), lambda i: (0, 0))],
        out_specs=[pl.BlockSpec((tm, d), lambda i: (i, 0)),
                   pl.BlockSpec((TOP_K, tm), lambda i: (0, i)),
                   pl.BlockSpec((TOP_K, tm), lambda i: (0, i)),
                   pl.BlockSpec((TOP_K, tm), lambda i: (0, i)),
                   pl.BlockSpec((N_EXPERTS, 1), lambda i: (0, 0)),
                   pl.BlockSpec((slab, d), lambda i: (i, 0))],
        out_shape=[jax.ShapeDtypeStruct((n_rows, d), F32),
                   jax.ShapeDtypeStruct((TOP_K, n_rows), jnp.int32),
                   jax.ShapeDtypeStruct((TOP_K, n_rows), F32),
                   jax.ShapeDtypeStruct((TOP_K, n_rows), jnp.int32),
                   jax.ShapeDtypeStruct((N_EXPERTS, 1), jnp.int32),
                   jax.ShapeDtypeStruct((n_tiles * slab, d), F32)],
        scratch_shapes=[pltpu.VMEM((N_EXPERTS, 1), F32)],
        compiler_params=_params(1),
        name="moe_norm_route",
    )(xs, g.reshape(1, d), mod, router_w.T, router_b.reshape(N_EXPERTS, 1))


def _row_copy(src_ref, src_row, dst_ref, dst_row, sem):
    return pltpu.make_async_copy(src_ref.at[pl.ds(src_row, 1), :], dst_ref.at[pl.ds(dst_row, 1), :], sem)


def _dispatch_kernel(dest_ref, h_ref, init_ref, xg_ref, sem, *, tm):
    del init_ref

    def start(t, carry):
        for k in range(TOP_K):
            _row_copy(h_ref, t, xg_ref, dest_ref[0, k, t], sem).start()
        return carry

    lax.fori_loop(0, tm, start, 0, unroll=DMA_UNROLL)
    for _ in range(TOP_K):
        pltpu.make_async_copy(h_ref, xg_ref.at[pl.ds(0, tm), :], sem).wait()


def _dispatch(hp, dest_tiles, slots, tm):
    n_rows, w = hp.shape
    return pl.pallas_call(
        functools.partial(_dispatch_kernel, tm=tm),
        grid=(n_rows // tm,),
        in_specs=[pl.BlockSpec((1, TOP_K, tm), lambda i: (i, 0, 0), memory_space=pltpu.SMEM),
                  pl.BlockSpec((tm, w), lambda i: (i, 0)),
                  pl.BlockSpec(memory_space=pl.ANY)],
        out_specs=pl.BlockSpec(memory_space=pl.ANY),
        out_shape=jax.ShapeDtypeStruct(slots.shape, slots.dtype),
        scratch_shapes=[pltpu.SemaphoreType.DMA(())],
        input_output_aliases={2: 0},
        compiler_params=_params(1),
        name="moe_dispatch_rows",
    )(dest_tiles, hp, slots)


def _expert_kernel(be_ref, nused_ref, next_ref, x_ref, wg_hbm, wu_hbm, wd_hbm, o_ref,
                   sg_ref, su_ref, sd_ref, wgb_ref, wub_ref, wdb_ref, sems, *, layer):
    i = pl.program_id(0)
    used = i < nused_ref[0]
    expert = be_ref[i]
    new_expert = (i == 0) | (expert != be_ref[jnp.maximum(i - 1, 0)])

    def weight_copies(e):
        return (pltpu.make_async_copy(wg_hbm.at[layer, e], sg_ref, sems.at[0]),
                pltpu.make_async_copy(wu_hbm.at[layer, e], su_ref, sems.at[1]),
                pltpu.make_async_copy(wd_hbm.at[layer, e], sd_ref, sems.at[2]))

    @pl.when(used & (i == 0))
    def _():
        for cp in weight_copies(expert):
            cp.start()

    @pl.when(used & new_expert)
    def _():
        for cp in weight_copies(expert):
            cp.wait()
        for r0 in range(0, D_MODEL, CAST_ROWS):
            rows = slice(r0, r0 + CAST_ROWS)
            wgb_ref[rows, :] = sg_ref[rows, :].astype(BF16)
            wub_ref[rows, :] = su_ref[rows, :].astype(BF16)
        for r0 in range(0, D_EXPERT, CAST_ROWS):
            rows = slice(r0, r0 + CAST_ROWS)
            wdb_ref[rows, :] = sd_ref[rows, :].astype(BF16)
        following = next_ref[expert]

        @pl.when(following >= 0)
        def _():
            for cp in weight_copies(following):
                cp.start()

    @pl.when(used)
    def _():
        x = x_ref[...].astype(BF16)
        gate = _dot(x, wgb_ref[...])
        up = _dot(x, wub_ref[...])
        act = (gate * jax.nn.sigmoid(gate) * up).astype(BF16)
        o_ref[...] = _dot(act, wdb_ref[...])

    @pl.when(i >= nused_ref[0])
    def _():
        o_ref[...] = jnp.zeros_like(o_ref)


def _expert_ffn(xg, block_e, n_used, next_expert, wg, wu, wd, layer, cap, tm):
    d = xg.shape[1]
    hbm = pl.BlockSpec(memory_space=pl.ANY)
    grid_spec = pltpu.PrefetchScalarGridSpec(
        num_scalar_prefetch=3,
        grid=(cap // tm,),
        in_specs=[pl.BlockSpec((tm, d), lambda i, be, nu, nx: (i, 0)), hbm, hbm, hbm],
        out_specs=pl.BlockSpec((tm, d), lambda i, be, nu, nx: (i, 0)),
        scratch_shapes=[pltpu.VMEM((d, D_EXPERT), F32), pltpu.VMEM((d, D_EXPERT), F32),
                        pltpu.VMEM((D_EXPERT, d), F32),
                        pltpu.VMEM((d, D_EXPERT), BF16), pltpu.VMEM((d, D_EXPERT), BF16),
                        pltpu.VMEM((D_EXPERT, d), BF16),
                        pltpu.SemaphoreType.DMA((3,))],
    )
    return pl.pallas_call(
        functools.partial(_expert_kernel, layer=layer),
        grid_spec=grid_spec,
        out_shape=jax.ShapeDtypeStruct((cap, d), F32),
        compiler_params=_params(1),
        name="moe_expert_ffn",
    )(block_e, n_used, next_expert, xg, wg, wu, wd)


def _combine_kernel(dest_ref, x_ref, gt_ref, g2_ref, fg_ref, yb_ref, o_ref, ybuf_ref, sem,
                    *, n_ctx, tm, final_norm, first_tile):
    def start(t, carry):
        for k in range(TOP_K):
            _row_copy(yb_ref, dest_ref[0, k, t], ybuf_ref.at[k], t, sem).start()
        return carry

    lax.fori_loop(0, tm, start, 0, unroll=DMA_UNROLL)
    for k in range(TOP_K):
        pltpu.make_async_copy(yb_ref.at[pl.ds(0, tm), :], ybuf_ref.at[k], sem).wait()

    is_ctx = (pl.program_id(0) + first_tile) * tm < n_ctx
    g2 = jnp.where(is_ctx, g2_ref[1:2, :], g2_ref[0:1, :])
    gt = gt_ref[...]
    y = gt[:, 0:1] * ybuf_ref[0] + gt[:, 1:2] * ybuf_ref[1]
    x = x_ref[...] + g2 * y
    if final_norm:
        ms = jnp.mean(x * x, axis=-1, keepdims=True)
        x = x * lax.rsqrt(ms + RMS_EPS) * fg_ref[...]
    o_ref[...] = x


def _combine(xs, yb, dest_tiles, gates_t, g2, final_g, n_ctx, final_norm, tm):
    n_rows, d = xs.shape
    skip = n_ctx // tm if final_norm else 0
    return pl.pallas_call(
        functools.partial(_combine_kernel, n_ctx=n_ctx, tm=tm, final_norm=final_norm, first_tile=skip),
        grid=(n_rows // tm - skip,),
        in_specs=[pl.BlockSpec((1, TOP_K, tm), lambda i: (i + skip, 0, 0), memory_space=pltpu.SMEM),
                  pl.BlockSpec((tm, d), lambda i: (i + skip, 0)),
                  pl.BlockSpec((tm, TOP_K), lambda i: (i + skip, 0)),
                  pl.BlockSpec((8, d), lambda i: (0, 0)), pl.BlockSpec((1, d), lambda i: (0, 0)),
                  pl.BlockSpec(memory_space=pl.ANY)],
        out_specs=pl.BlockSpec((tm, d), lambda i: (i, 0)),
        out_shape=jax.ShapeDtypeStruct((n_rows - skip * tm, d), F32),
        scratch_shapes=[pltpu.VMEM((TOP_K, tm, yb.shape[1]), yb.dtype), pltpu.SemaphoreType.DMA(())],
        compiler_params=_params(1),
        name="moe_combine_residual",
    )(dest_tiles, xs, gates_t, g2, final_g.reshape(1, d), yb)


def _moe(xs, g, mod, g2, router_w, router_b, wg, wu, wd, layer, final_g, n_ctx, final_norm):
    n_rows, d = xs.shape
    tm = 256
    n_asg = n_rows * TOP_K
    n_blk = n_asg // tm + N_EXPERTS
    cap = n_blk * tm
    h, idx, gates, rank, counts, slots = _route(xs, g, mod, router_w, router_b, n_ctx, cap)
    counts = counts[:, 0]
    padded = (counts + tm - 1) // tm * tm
    pends = jnp.cumsum(padded)
    pstarts = pends - padded
    seg_start = jnp.sum(jnp.where(idx[..., None] == jnp.arange(N_EXPERTS), pstarts, 0), axis=-1)
    dest = seg_start + rank
    dest_tiles = dest.reshape(TOP_K, n_rows // tm, tm).transpose(1, 0, 2).astype(jnp.int32)
    block_e = jnp.minimum(jnp.sum(pends[None, :] <= (jnp.arange(n_blk) * tm)[:, None], axis=1), N_EXPERTS - 1)
    n_used = (pends[-1] // tm).reshape(1)
    xg = _dispatch(h, dest_tiles, slots, tm)
    eid = jnp.arange(N_EXPERTS)
    later = (eid[None, :] > eid[:, None]) & (counts[None, :] > 0)
    next_expert = jnp.min(jnp.where(later, eid[None, :], N_EXPERTS), axis=1)
    next_expert = jnp.where(next_expert == N_EXPERTS, -1, next_expert).astype(jnp.int32)
    yb = _expert_ffn(xg, block_e.astype(jnp.int32), n_used.astype(jnp.int32), next_expert, wg, wu, wd, layer,
                     cap, tm)
    return _combine(xs, yb, dest_tiles, gates.T, g2, final_g, n_ctx, final_norm, tm)


def _mod_rows(mod_l, lat_chunks, ctx_chunks):
    d = D_MODEL
    rows = [mod_l[0, c * d:(c + 1) * d] for c in lat_chunks] + [mod_l[1, c * d:(c + 1) * d] for c in ctx_chunks]
    out = jnp.zeros((8, d), F32)
    return out.at[:len(rows)].set(jnp.stack(rows))


def kernel(x, c, ctx, c_ctx, ada_w, ada_b, norm1_g, norm2_g, w_in, w_out, conv_w, attn_sinks, decay_base,
           decay_up, iclr_base, iclr_up, gate_up, k_k, k_a, r_k, lnx_w, lnx_b, router_w, router_b,
           expert_gate, expert_up, expert_down, final_norm_g):
    bsz, n_lat, d = x.shape
    n_ctx = ctx.shape[1]
    depth = ada_w.shape[0]
    assert bsz == 1 and d == D_MODEL and n_ctx % NORM_ROWS == 0 and n_lat % NORM_ROWS == 0

    xs = jnp.concatenate([ctx[0], x[0]], axis=0)
    mods = _ada_mod(c, c_ctx, ada_w, ada_b)
    cos, sin = _rope_tables(n_ctx, n_lat)

    for l in range(depth):
        last = l == depth - 1
        mod1 = _mod_rows(mods[l], (0, 1), (0, 1))
        gate1 = _mod_rows(mods[l], (2,), (2,))
        mod2 = _mod_rows(mods[l], (3, 4), (3, 4))
        gate2 = _mod_rows(mods[l], (5,), (5,))

        p = _norm_mod_matmul(xs, norm1_g[l], mod1, _in_proj_weights(w_in[l]), n_ctx)

        a_mix = _short_conv(p, conv_w[l], n_ctx)
        qr, kr, vb = _rope_qkv(p, cos, sin)
        b_mix = _window_attention(qr, kr, vb, attn_sinks[l], n_ctx, n_lat)
        yf, yb = _rwkv_scan(p, decay_base[l], decay_up[l], iclr_base[l], iclr_up[l], k_k[l], k_a[l], n_ctx)
        c_mix = _rwkv_out(yf, yb, p, iclr_base[l], iclr_up[l], k_a[l], r_k[l].reshape(-1), lnx_w[l], lnx_b[l],
                          gate_up[l])
        xs = _out_proj(a_mix, b_mix, c_mix, w_out[l].astype(BF16), xs, gate1, n_ctx)

        xs = _moe(xs, norm2_g[l], mod2, gate2, router_w, router_b, expert_gate, expert_up, expert_down, l,
                  final_norm_g, n_ctx, last)
    return xs.reshape(bsz, n_lat, d)
```
